```python
import jax, jax.numpy as jnp
from jax import lax
import numpy as np

D_MODEL = 1024
BATCH = 8
SEQ = 2048
DEPTH = 2

CHUNK = 64
Q_BLOCK = 128
A_HEADS = 8
A_HEAD_DIM = D_MODEL // 16
A_WIDTH = A_HEADS * A_HEAD_DIM
POOL_WINDOWS = (2, 4, 8, 16)
POOL_GROUPS = len(POOL_WINDOWS)
POOL_WIDTH = D_MODEL // 4
POOL_GROUP_DIM = POOL_WIDTH // POOL_GROUPS
CONV_WIDTH = D_MODEL // 4
CONV_K = 3
N_BRANCH = 3
MIX_WIDTH = A_WIDTH + POOL_WIDTH + CONV_WIDTH
D_FF = 4 * D_MODEL
RMS_EPS = 1e-6
NEG_INF = -1e30
IN_COLS = 3 * A_WIDTH + A_HEADS + POOL_WIDTH + 3 * CONV_WIDTH + N_BRANCH * D_MODEL

kernel_name = 'hybrid_fox_pool_conv_block'


def rmsnorm(x, g):
    xf = x.astype(jnp.float32)
    y = xf * lax.rsqrt(jnp.mean(xf * xf, axis=-1, keepdims=True) + RMS_EPS)
    return (y * g.astype(jnp.float32)).astype(x.dtype)


def forgetting_attention(q, k, v, f_logit):
    b, s, h, dh = q.shape
    q = q.transpose(0, 2, 1, 3)
    k = k.transpose(0, 2, 1, 3)
    v = v.transpose(0, 2, 1, 3)
    log_f = jax.nn.log_sigmoid(f_logit.astype(jnp.float32))
    cum_f = jnp.cumsum(log_f, axis=1).transpose(0, 2, 1)
    scale = dh ** -0.5
    outs = []
    for i in range(s // Q_BLOCK):
        qs, qe = i * Q_BLOCK, (i + 1) * Q_BLOCK
        qb = q[:, :, qs:qe]
        kb = k[:, :, :qe]
        vb = v[:, :, :qe]
        logits = jnp.einsum('bhqd,bhkd->bhqk', qb, kb).astype(jnp.float32) * scale
        logits = logits + cum_f[:, :, qs:qe, None] - cum_f[:, :, None, :qe]
        causal = jnp.arange(qs, qe)[:, None] >= jnp.arange(qe)[None, :]
        logits = jnp.where(causal[None, None], logits, NEG_INF)
        p = jax.nn.softmax(logits, axis=-1).astype(v.dtype)
        outs.append(jnp.einsum('bhqk,bhkd->bhqd', p, vb))
    o = jnp.concatenate(outs, axis=2)
    return o.transpose(0, 2, 1, 3).reshape(b, s, h * dh)


def pool_mixer(u, w_pool, pool_scale):
    b, s, _ = u.shape
    uf = u.astype(jnp.float32)
    cs = jnp.cumsum(uf, axis=1)
    groups = []
    for g, w in enumerate(POOL_WINDOWS):
        sl = slice(g * POOL_GROUP_DIM, (g + 1) * POOL_GROUP_DIM)
        cs_g = cs[..., sl]
        lagged = jnp.pad(cs_g, ((0, 0), (w, 0), (0, 0)))[:, :s]
        count = jnp.minimum(jnp.arange(1, s + 1, dtype=jnp.float32), float(w))
        groups.append((cs_g - lagged) / count[None, :, None] - uf[..., sl])
    p = jnp.stack(groups, axis=2).astype(u.dtype)
    y = jnp.einsum('bsgc,gcd->bsgd', p, w_pool).reshape(b, s, POOL_WIDTH)
    return y * pool_scale


def short_conv(h, b_gate, c_gate, conv_w):
    u = c_gate * h
    y = lax.conv_general_dilated(
        u, conv_w[:, None, :].astype(u.dtype), window_strides=(1,), padding=[(CONV_K - 1, 0)],
        dimension_numbers=('NWC', 'WIO', 'NWC'), feature_group_count=CONV_WIDTH)
    return b_gate * y


def hybrid_layer(x, c, w_ada, b_ada, g_mix_pre, g_mix_post, g_ff_pre, g_ff_post, w_in, b_f,
                 w_pool, pool_scale, conv_w, w_branch, w_out, w_ff1, w_ff2):
    b, s, d = x.shape
    mod = jax.nn.silu(c) @ w_ada + b_ada
    shift_m, scale_m, gate_m, shift_f, scale_f, gate_f = jnp.split(mod, 6, axis=-1)

    h = rmsnorm(x, g_mix_pre) * (1.0 + scale_m[:, None]) + shift_m[:, None]
    z = h @ w_in
    sizes = [A_WIDTH, A_WIDTH, A_WIDTH, A_HEADS, POOL_WIDTH, CONV_WIDTH, CONV_WIDTH, CONV_WIDTH]
    cuts = [int(v) for v in np.cumsum(sizes)]
    q, k, v, fl, pu, ch, cb, cc, gl = jnp.split(z, cuts, axis=-1)
    heads = (b, s, A_HEADS, A_HEAD_DIM)
    br_a = forgetting_attention(q.reshape(heads), k.reshape(heads), v.reshape(heads), fl + b_f)
    br_b = pool_mixer(pu, w_pool, pool_scale)
    br_c = short_conv(ch, cb, cc, conv_w)
    gates = jax.nn.sigmoid(gl).reshape(b, s, N_BRANCH, d)
    wa = w_branch[:A_WIDTH]
    wb = w_branch[A_WIDTH:A_WIDTH + POOL_WIDTH]
    wc = w_branch[A_WIDTH + POOL_WIDTH:]
    merged = gates[:, :, 0] * (br_a @ wa) + gates[:, :, 1] * (br_b @ wb) + gates[:, :, 2] * (br_c @ wc)
    y = merged @ w_out
    x = x + gate_m[:, None] * rmsnorm(y, g_mix_post)

    h2 = rmsnorm(x, g_ff_pre) * (1.0 + scale_f[:, None]) + shift_f[:, None]
    y2 = jnp.square(jax.nn.relu(h2 @ w_ff1)) @ w_ff2
    return x + gate_f[:, None] * rmsnorm(y2, g_ff_post)


def setup_inputs(seed: int = 0) -> dict:
    key = jax.random.key(seed)
    ks = jax.random.split(key, 20)

    def nrm(k, shape, std):
        return jax.random.normal(k, shape, jnp.float32) * std

    d = D_MODEL
    branch_row_scale = jnp.concatenate([
        jnp.full((A_WIDTH,), A_WIDTH ** -0.5, jnp.float32),
        jnp.full((POOL_WIDTH,), POOL_WIDTH ** -0.5, jnp.float32),
        jnp.full((CONV_WIDTH,), CONV_WIDTH ** -0.5, jnp.float32)])
    return {
        'x': nrm(ks[0], (BATCH, SEQ, d), 1.0),
        'c': nrm(ks[1], (BATCH, d), 1.0),
        'w_ada': nrm(ks[2], (DEPTH, d, 6 * d), 0.5 * d ** -0.5),
        'b_ada': nrm(ks[3], (DEPTH, 6 * d), 0.02),
        'g_mix_pre': 1.0 + nrm(ks[4], (DEPTH, d), 0.05),
        'g_mix_post': 1.0 + nrm(ks[5], (DEPTH, d), 0.05),
        'g_ff_pre': 1.0 + nrm(ks[6], (DEPTH, d), 0.05),
        'g_ff_post': 1.0 + nrm(ks[7], (DEPTH, d), 0.05),
        'w_in': nrm(ks[8], (DEPTH, d, IN_COLS), d ** -0.5),
        'b_f': 3.0 + nrm(ks[9], (DEPTH, A_HEADS), 0.1),
        'w_pool': nrm(ks[10], (DEPTH, POOL_GROUPS, POOL_GROUP_DIM, POOL_GROUP_DIM), POOL_GROUP_DIM ** -0.5),
        'pool_scale': 1.0 + nrm(ks[11], (DEPTH, POOL_WIDTH), 0.1),
        'conv_w': nrm(ks[12], (DEPTH, CONV_K, CONV_WIDTH), CONV_K ** -0.5),
        'w_branch': nrm(ks[13], (DEPTH, MIX_WIDTH, d), 1.0) * branch_row_scale[None, :, None],
        'w_out': nrm(ks[14], (DEPTH, d, d), d ** -0.5),
        'w_ff1': nrm(ks[15], (DEPTH, d, D_FF), d ** -0.5),
        'w_ff2': nrm(ks[16], (DEPTH, D_FF, d), D_FF ** -0.5),
    }


def reference(x, c, w_ada, b_ada, g_mix_pre, g_mix_post, g_ff_pre, g_ff_post, w_in, b_f,
              w_pool, pool_scale, conv_w, w_branch, w_out, w_ff1, w_ff2):
    for l in range(DEPTH):
        x = hybrid_layer(x, c, w_ada[l], b_ada[l], g_mix_pre[l], g_mix_post[l], g_ff_pre[l], g_ff_post[l],
                         w_in[l], b_f[l], w_pool[l], pool_scale[l], conv_w[l], w_branch[l], w_out[l],
                         w_ff1[l], w_ff2[l])
    return x
```

```python
import functools

import jax
import jax.numpy as jnp
from jax import lax
from jax.experimental import pallas as pl
from jax.experimental.pallas import tpu as pltpu

F32 = jnp.float32
BF16 = jnp.bfloat16

LANES = 128
HEAD_DIM = 64
HEADS = 8
PAIRS = HEADS // 2
A_WIDTH = HEADS * HEAD_DIM
POOL_WINDOWS = (2, 4, 8, 16)
POOL_WIDTH = 256
CONV_WIDTH = 256
HALO = 16
RMS_EPS = 1e-6
NEG_INF = -1e30
VMEM_LIMIT = 56 * 1024 * 1024

TM = 512
TQ = 512
TK = 512
FF_CHUNK = 1024


def _const_spec(shape):
    n = len(shape)
    return pl.BlockSpec(shape, lambda *_: (0,) * n, pipeline_mode=pl.Buffered(1))


def _rms(x, g):
    ms = jnp.mean(x * x, axis=-1, keepdims=True)
    return x * lax.rsqrt(ms + RMS_EPS) * g


def _ada_kernel(c_ref, w_ref, b_ref, o_ref):
    c = c_ref[...]
    sc = c * (1.0 / (1.0 + jnp.exp(-c)))
    o_ref[0, 0] = jnp.dot(sc, w_ref[0], precision=lax.Precision.HIGHEST,
                          preferred_element_type=F32) + b_ref[0, 0]


def _ada(c, w_ada, b_ada):
    depth, d, d6 = w_ada.shape
    nb = c.shape[0]
    n = d6 // d
    return pl.pallas_call(
        _ada_kernel,
        grid=(depth, n),
        in_specs=[
            pl.BlockSpec((nb, d), lambda l, j: (0, 0)),
            pl.BlockSpec((1, d, d), lambda l, j: (l, 0, j)),
            pl.BlockSpec((1, 1, 1, d), lambda l, j: (l, j, 0, 0)),
        ],
        out_specs=pl.BlockSpec((1, 1, nb, d), lambda l, j: (l, j, 0, 0)),
        out_shape=jax.ShapeDtypeStruct((depth, n, nb, d), F32),
        compiler_params=pltpu.CompilerParams(
            dimension_semantics=("arbitrary", "arbitrary"), vmem_limit_bytes=VMEM_LIMIT),
        name="ada",
    )(c, w_ada, b_ada.reshape(depth, n, 1, d))


def _inproj_kernel(tiles_per_seq, x_ref, mod_ref, g_ref, wm_ref, wvt_ref, wf_ref, bf_ref,
                   q_ref, k_ref, vt_ref, frow_ref, fcol_ref, pu_ref, cv_ref, gl_ref, carry_ref):
    i = pl.program_id(0)
    x = x_ref[...]
    shift = mod_ref[0, 0]
    scale = mod_ref[1, 0]
    h = _rms(x, g_ref[...]) * (1.0 + scale) + shift
    hb = h.astype(BF16)

    def proj(lo, hi):
        return jnp.dot(hb, wm_ref[:, lo:hi], preferred_element_type=F32)

    zq = proj(0, A_WIDTH) * (HEAD_DIM ** -0.5)
    zk = proj(A_WIDTH, 2 * A_WIDTH)
    for p in range(PAIRS):
        q_ref[p] = zq[:, p * LANES:(p + 1) * LANES].astype(BF16)
        k_ref[p] = zk[:, p * LANES:(p + 1) * LANES].astype(BF16)
    vt_ref[0] = lax.dot_general(wvt_ref[...], hb, (((1,), (1,)), ((), ())),
                                preferred_element_type=F32).astype(BF16)
    o = 2 * A_WIDTH
    pu_ref[...] = proj(o, o + POOL_WIDTH).astype(BF16)
    o += POOL_WIDTH
    cv_ref[...] = proj(o, o + 3 * CONV_WIDTH).astype(BF16)
    o += 3 * CONV_WIDTH
    d = x.shape[1]
    for j in range(3):
        gl_ref[:, j * d:(j + 1) * d] = proj(o + j * d, o + (j + 1) * d).astype(BF16)

    zf = jnp.dot(hb, wf_ref[...], preferred_element_type=F32) + bf_ref[...]
    lf = jnp.minimum(zf, 0.0) - jnp.log(1.0 + jnp.exp(-jnp.abs(zf)))
    tm = lf.shape[0]
    row = lax.broadcasted_iota(jnp.int32, lf.shape, 0)
    step = 1
    while step < tm:
        lf = lf + jnp.where(row >= step, pltpu.roll(lf, step, 0), 0.0)
        step *= 2

    @pl.when(i % tiles_per_seq == 0)
    def _():
        carry_ref[...] = jnp.zeros_like(carry_ref)

    fc = lf + carry_ref[0:1, :]
    carry_ref[...] = jnp.broadcast_to(fc[tm - 1:tm, :], carry_ref.shape)
    frow_ref[0] = fc.T[0:HEADS, :]
    fcol_ref[0] = fc
    for p in range(1, PAIRS):
        fcol_ref[p] = pltpu.roll(fc, LANES - 2 * p, 1)


def _inproj(x2, mod, g, wm, wvt, wf, bf, seq):
    t, d = x2.shape
    nt = t // TM
    tiles_per_seq = seq // TM
    nb = t // seq
    ncols = wm.shape[1]
    out_shape = (
        jax.ShapeDtypeStruct((PAIRS, t, LANES), BF16),
        jax.ShapeDtypeStruct((PAIRS, t, LANES), BF16),
        jax.ShapeDtypeStruct((nt, A_WIDTH, TM), BF16),
        jax.ShapeDtypeStruct((nb, HEADS, seq), F32),
        jax.ShapeDtypeStruct((PAIRS, t, LANES), F32),
        jax.ShapeDtypeStruct((t, POOL_WIDTH), BF16),
        jax.ShapeDtypeStruct((t, 3 * CONV_WIDTH), BF16),
        jax.ShapeDtypeStruct((t, 3 * d), BF16),
    )
    out_specs = (
        pl.BlockSpec((PAIRS, TM, LANES), lambda i: (0, i, 0)),
        pl.BlockSpec((PAIRS, TM, LANES), lambda i: (0, i, 0)),
        pl.BlockSpec((1, A_WIDTH, TM), lambda i: (i, 0, 0)),
        pl.BlockSpec((1, HEADS, TM), lambda i: (i // tiles_per_seq, 0, i % tiles_per_seq)),
        pl.BlockSpec((PAIRS, TM, LANES), lambda i: (0, i, 0)),
        pl.BlockSpec((TM, POOL_WIDTH), lambda i: (i, 0)),
        pl.BlockSpec((TM, 3 * CONV_WIDTH), lambda i: (i, 0)),
        pl.BlockSpec((TM, 3 * d), lambda i: (i, 0)),
    )
    return pl.pallas_call(
        functools.partial(_inproj_kernel, tiles_per_seq),
        grid=(nt,),
        in_specs=[
            pl.BlockSpec((TM, d), lambda i: (i, 0)),
            pl.BlockSpec((2, 1, 1, d), lambda i: (0, i // tiles_per_seq, 0, 0)),
            _const_spec((1, d)),
            _const_spec((d, ncols)),
            _const_spec((A_WIDTH, d)),
            _const_spec((d, LANES)),
            _const_spec((1, LANES)),
        ],
        out_specs=out_specs,
        out_shape=out_shape,
        scratch_shapes=[pltpu.VMEM((8, LANES), F32)],
        compiler_params=pltpu.CompilerParams(
            dimension_semantics=("arbitrary",), vmem_limit_bytes=VMEM_LIMIT),
        name="inproj",
    )(x2, mod, g, wm, wvt, wf, bf)


def _attn_kernel(q_ref, k_ref, vt_ref, frow_ref, fcol_ref, o_ref):
    p = pl.program_id(1)
    qi = pl.program_id(2)
    q = q_ref[0]
    lane = lax.broadcasted_iota(jnp.int32, q.shape, 1)
    key_idx = lax.broadcasted_iota(jnp.int32, (TK, TQ), 0)
    qry_idx = lax.broadcasted_iota(jnp.int32, (TK, TQ), 1)
    outs = []
    for hh in range(2):
        qm = jnp.where((lane >= hh * HEAD_DIM) & (lane < (hh + 1) * HEAD_DIM), q, jnp.zeros_like(q))
        fq = frow_ref[0, pl.ds(2 * p + hh, 1), :]

        def tile(kt, carry, masked):
            m, l, acc = carry
            off = pl.multiple_of(kt * TK, TK)
            kb = k_ref[0, pl.ds(off, TK), :]
            st = lax.dot_general(kb, qm, (((1,), (1,)), ((), ())), preferred_element_type=F32)
            fk = fcol_ref[0, pl.ds(off, TK), hh:hh + 1]
            st = st + (fq - fk)
            if masked:
                st = jnp.where(key_idx <= qry_idx, st, NEG_INF)
            m_new = jnp.maximum(m, jnp.max(st, axis=0, keepdims=True))
            alpha = jnp.exp(m - m_new)
            pt = jnp.exp(st - m_new)
            l = alpha * l + jnp.sum(pt, axis=0, keepdims=True)
            acc = alpha * acc + jnp.dot(vt_ref[kt], pt.astype(BF16), preferred_element_type=F32)
            return m_new, l, acc

        init = (jnp.full((1, TQ), NEG_INF, F32), jnp.zeros((1, TQ), F32), jnp.zeros((LANES, TQ), F32))
        carry = lax.fori_loop(0, qi, lambda kt, c: tile(kt, c, False), init)
        m, l, acc = tile(qi, carry, True)
        outs.append(acc / l)
    rowi = lax.broadcasted_iota(jnp.int32, (LANES, TQ), 0)
    ot = jnp.where(rowi < HEAD_DIM, outs[0], outs[1])
    o_ref[...] = ot.T.astype(BF16)


def _attention(q4, k4, vt, frow, fcol4, seq):
    _, t, _ = q4.shape
    nb = t // seq
    nq = seq // TQ
    nk = seq // TK
    return pl.pallas_call(
        _attn_kernel,
        grid=(nb, PAIRS, nq),
        in_specs=[
            pl.BlockSpec((1, TQ, LANES), lambda b, p, qi: (p, b * nq + qi, 0)),
            pl.BlockSpec((1, seq, LANES), lambda b, p, qi: (p, b, 0)),
            pl.BlockSpec((nk, LANES, TK), lambda b, p, qi: (b, p, 0)),
            pl.BlockSpec((1, HEADS, TQ), lambda b, p, qi: (b, 0, qi)),
            pl.BlockSpec((1, seq, LANES), lambda b, p, qi: (p, b, 0)),
        ],
        out_specs=pl.BlockSpec((TQ, LANES), lambda b, p, qi: (b * nq + qi, p)),
        out_shape=jax.ShapeDtypeStruct((t, A_WIDTH), BF16),
        compiler_params=pltpu.CompilerParams(
            dimension_semantics=("arbitrary", "arbitrary", "arbitrary"), vmem_limit_bytes=VMEM_LIMIT),
        name="attn",
    )(q4, k4, vt, frow, fcol4)


def _merge_kernel(tiles_per_seq, o_ref, pu_ref, puh_ref, cv_ref, cvh_ref, gl_ref, x_ref, gate_ref,
                  wpool_ref, pscale_ref, convw_ref, wbr_ref, wout_ref, g_ref, out_ref):
    i = pl.program_id(0)
    tile_in_seq = i % tiles_per_seq
    keep = (tile_in_seq != 0).astype(F32)
    tm = pu_ref.shape[0]

    ext = jnp.concatenate([puh_ref[...].astype(F32) * keep, pu_ref[...].astype(F32)], axis=0)
    a2 = ext + pltpu.roll(ext, 1, 0)
    a4 = a2 + pltpu.roll(a2, 2, 0)
    a8 = a4 + pltpu.roll(a4, 4, 0)
    a16 = a8 + pltpu.roll(a8, 8, 0)
    lane = lax.broadcasted_iota(jnp.int32, ext.shape, 1)
    row = lax.broadcasted_iota(jnp.int32, ext.shape, 0)
    gd = POOL_WIDTH // len(POOL_WINDOWS)
    win = jnp.where(lane < gd, a2, jnp.where(lane < 2 * gd, a4, jnp.where(lane < 3 * gd, a8, a16)))
    wsz = jnp.where(lane < gd, 2.0, jnp.where(lane < 2 * gd, 4.0, jnp.where(lane < 3 * gd, 8.0, 16.0)))
    frames = (tile_in_seq * tm + row - (HALO - 1)).astype(F32)
    cnt = jnp.maximum(jnp.minimum(frames, wsz), 1.0)
    pm = (win / cnt - ext)[HALO:]
    br_b = jnp.dot(pm.astype(BF16), wpool_ref[...], preferred_element_type=F32) * pscale_ref[...]

    cve = jnp.concatenate([cvh_ref[...].astype(F32) * keep, cv_ref[...].astype(F32)], axis=0)
    u = cve[:, 2 * CONV_WIDTH:] * cve[:, :CONV_WIDTH]
    cw = convw_ref[...]
    y = cw[0:1] * pltpu.roll(u, 2, 0) + cw[1:2] * pltpu.roll(u, 1, 0) + cw[2:3] * u
    br_c = (cve[:, CONV_WIDTH:2 * CONV_WIDTH] * y)[HALO:]

    d = x_ref.shape[1]

    def gate(j):
        gl = gl_ref[:, j * d:(j + 1) * d].astype(F32)
        return 1.0 / (1.0 + jnp.exp(-gl))

    o1 = A_WIDTH + POOL_WIDTH
    merged = gate(0) * jnp.dot(o_ref[...], wbr_ref[0:A_WIDTH, :], preferred_element_type=F32)
    merged += gate(1) * jnp.dot(br_b.astype(BF16), wbr_ref[A_WIDTH:o1, :], preferred_element_type=F32)
    merged += gate(2) * jnp.dot(br_c.astype(BF16), wbr_ref[o1:, :], preferred_element_type=F32)
    yo = jnp.dot(merged.astype(BF16), wout_ref[...], preferred_element_type=F32)
    out_ref[...] = x_ref[...] + gate_ref[0, 0] * _rms(yo, g_ref[...])


def _merge(o, pu, cv, gl, x2, gate, wpool, pscale, convw, wbr, wout, g, seq):
    t, d = x2.shape
    nt = t // TM
    tiles_per_seq = seq // TM
    hb = TM // HALO

    def halo_map(i):
        return (jnp.maximum(i * hb - 1, 0), 0)

    return pl.pallas_call(
        functools.partial(_merge_kernel, tiles_per_seq),
        grid=(nt,),
        in_specs=[
            pl.BlockSpec((TM, A_WIDTH), lambda i: (i, 0)),
            pl.BlockSpec((TM, POOL_WIDTH), lambda i: (i, 0)),
            pl.BlockSpec((HALO, POOL_WIDTH), halo_map),
            pl.BlockSpec((TM, 3 * CONV_WIDTH), lambda i: (i, 0)),
            pl.BlockSpec((HALO, 3 * CONV_WIDTH), halo_map),
            pl.BlockSpec((TM, 3 * d), lambda i: (i, 0)),
            pl.BlockSpec((TM, d), lambda i: (i, 0)),
            pl.BlockSpec((1, 1, 1, d), lambda i: (0, i // tiles_per_seq, 0, 0)),
            _const_spec(wpool.shape),
            _const_spec(pscale.shape),
            _const_spec(convw.shape),
            _const_spec(wbr.shape),
            _const_spec(wout.shape),
            _const_spec(g.shape),
        ],
        out_specs=pl.BlockSpec((TM, d), lambda i: (i, 0)),
        out_shape=jax.ShapeDtypeStruct((t, d), F32),
        compiler_params=pltpu.CompilerParams(
            dimension_semantics=("arbitrary",), vmem_limit_bytes=VMEM_LIMIT),
        name="merge",
    )(o, pu, pu, cv, cv, gl, x2, gate, wpool, pscale, convw, wbr, wout, g)


def _mlp_kernel(x_ref, mod_ref, gpre_ref, w1_ref, w2_ref, gpost_ref, out_ref):
    x = x_ref[...]
    shift = mod_ref[0, 0]
    scale = mod_ref[1, 0]
    gate = mod_ref[2, 0]
    hb = (_rms(x, gpre_ref[...]) * (1.0 + scale) + shift).astype(BF16)
    dff = w1_ref.shape[1]
    acc = jnp.zeros(x.shape, F32)
    for c in range(dff // FF_CHUNK):
        up = jnp.dot(hb, w1_ref[:, c * FF_CHUNK:(c + 1) * FF_CHUNK], preferred_element_type=F32)
        act = jnp.square(jnp.maximum(up, 0.0)).astype(BF16)
        acc += jnp.dot(act, w2_ref[c * FF_CHUNK:(c + 1) * FF_CHUNK, :], preferred_element_type=F32)
    out_ref[...] = x + gate * _rms(acc, gpost_ref[...])


def _mlp(x2, mod, gpre, w1, w2, gpost, seq):
    t, d = x2.shape
    nt = t // TM
    tiles_per_seq = seq // TM
    return pl.pallas_call(
        _mlp_kernel,
        grid=(nt,),
        in_specs=[
            pl.BlockSpec((TM, d), lambda i: (i, 0)),
            pl.BlockSpec((3, 1, 1, d), lambda i: (0, i // tiles_per_seq, 0, 0)),
            _const_spec(gpre.shape),
            _const_spec(w1.shape),
            _const_spec(w2.shape),
            _const_spec(gpost.shape),
        ],
        out_specs=pl.BlockSpec((TM, d), lambda i: (i, 0)),
        out_shape=jax.ShapeDtypeStruct((t, d), F32),
        compiler_params=pltpu.CompilerParams(
            dimension_semantics=("arbitrary",), vmem_limit_bytes=VMEM_LIMIT),
        name="mlp",
    )(x2, mod, gpre, w1, w2, gpost)


def _block_diag(w):
    g, c, dd = w.shape
    eye = jnp.eye(g, dtype=w.dtype)
    return (eye[:, None, :, None] * w[:, :, None, :]).reshape(g * c, g * dd)


def kernel(x, c, w_ada, b_ada, g_mix_pre, g_mix_post, g_ff_pre, g_ff_post, w_in, b_f, w_pool, pool_scale,
           conv_w, w_branch, w_out, w_ff1, w_ff2):
    nb, seq, d = x.shape
    depth = w_ada.shape[0]
    assert seq % TM == 0 and seq % TQ == 0 and seq % TK == 0 and d % LANES == 0
    t = nb * seq
    mod = _ada(c, w_ada, b_ada).reshape(depth, 6, nb, 1, d)
    x2 = x.reshape(t, d)
    vo = 2 * A_WIDTH
    fo = 3 * A_WIDTH
    ro = fo + HEADS
    for l in range(depth):
        wi = w_in[l]
        wm = jnp.concatenate([wi[:, :vo], wi[:, ro:]], axis=1).astype(BF16)
        wvt = wi[:, vo:fo].T.astype(BF16)
        wf = jnp.pad(wi[:, fo:ro], ((0, 0), (0, LANES - HEADS))).astype(BF16)
        bf = jnp.pad(b_f[l], (0, LANES - HEADS)).reshape(1, LANES)
        row = lambda a: a[l].reshape(1, -1)
        q4, k4, vt, frow, fcol4, pu, cv, gl = _inproj(x2, mod[l, 0:2], row(g_mix_pre), wm, wvt, wf, bf, seq)
        o = _attention(q4, k4, vt, frow, fcol4, seq)
        x2 = _merge(o, pu, cv, gl, x2, mod[l, 2:3], _block_diag(w_pool[l]).astype(BF16), row(pool_scale),
                    conv_w[l], w_branch[l].astype(BF16), w_out[l].astype(BF16), row(g_mix_post), seq)
        x2 = _mlp(x2, mod[l, 3:6], row(g_ff_pre), w_ff1[l].astype(BF16), w_ff2[l].astype(BF16),
                  row(g_ff_post), seq)
    return x2.reshape(nb, seq, d)
```

```python
import functools

import jax
import jax.numpy as jnp
import numpy as np
from jax import lax
from jax.experimental import pallas as pl
from jax.experimental.pallas import tpu as pltpu

F32 = jnp.float32
BF16 = jnp.bfloat16

LANES = 128
HEAD_DIM = 64
HEADS = 8
PAIRS = HEADS // 2
A_WIDTH = HEADS * HEAD_DIM
POOL_WINDOWS = (2, 4, 8, 16)
POOL_WIDTH = 256
CONV_WIDTH = 256
HALO = 16
RMS_EPS = 1e-6
NEG_INF = -1e30
LOG2E = 1.4426950408889634
AUG = 6
VMEM_LIMIT = 56 * 1024 * 1024

TM = 512
TQ = 512
TK = 512
KC = 256
FF_CHUNK = 1024


def _const_spec(shape):
    n = len(shape)
    return pl.BlockSpec(shape, lambda *_: (0,) * n, pipeline_mode=pl.Buffered(1))


def _rms(x, g):
    ms = jnp.mean(x * x, axis=-1, keepdims=True)
    return x * lax.rsqrt(ms + RMS_EPS) * g


def _ada_kernel(c_ref, w_ref, b_ref, o_ref):
    c = c_ref[...]
    sc = c * (1.0 / (1.0 + jnp.exp(-c)))
    o_ref[0, 0] = jnp.dot(sc, w_ref[0], precision=lax.Precision.HIGHEST,
                          preferred_element_type=F32) + b_ref[0, 0]


def _ada(c, w_ada, b_ada):
    depth, d, d6 = w_ada.shape
    nb = c.shape[0]
    n = d6 // d
    return pl.pallas_call(
        _ada_kernel,
        grid=(depth, n),
        in_specs=[
            pl.BlockSpec((nb, d), lambda l, j: (0, 0)),
            pl.BlockSpec((1, d, d), lambda l, j: (l, 0, j)),
            pl.BlockSpec((1, 1, 1, d), lambda l, j: (l, j, 0, 0)),
        ],
        out_specs=pl.BlockSpec((1, 1, nb, d), lambda l, j: (l, j, 0, 0)),
        out_shape=jax.ShapeDtypeStruct((depth, n, nb, d), F32),
        compiler_params=pltpu.CompilerParams(
            dimension_semantics=("arbitrary", "arbitrary"), vmem_limit_bytes=VMEM_LIMIT),
        name="ada",
    )(c, w_ada, b_ada.reshape(depth, n, 1, d))


def _aug_base(h):
    return (HEAD_DIM if h % 2 == 0 else 0) + AUG * (h // 2)


def _inproj_kernel(tiles_per_seq, x_ref, mod_ref, g_ref, wm_ref, wvt_ref, wf_ref, bf_ref,
                   q_ref, k_ref, vt_ref, pu_ref, cv_ref, gl_ref, carry_ref):
    i = pl.program_id(0)
    x = x_ref[...]
    shift = mod_ref[0, 0]
    scale = mod_ref[1, 0]
    h = _rms(x, g_ref[...]) * (1.0 + scale) + shift
    hb = h.astype(BF16)
    tm = x.shape[0]

    def proj(lo, hi):
        return jnp.dot(hb, wm_ref[:, lo:hi], preferred_element_type=F32)

    zf = jnp.dot(hb, wf_ref[...], preferred_element_type=F32) + bf_ref[...]
    lf = jnp.minimum(zf, 0.0) - jnp.log(1.0 + jnp.exp(-jnp.abs(zf)))
    row = lax.broadcasted_iota(jnp.int32, lf.shape, 0)
    step = 1
    while step < tm:
        lf = lf + jnp.where(row >= step, pltpu.roll(lf, step, 0), 0.0)
        step *= 2

    @pl.when(i % tiles_per_seq == 0)
    def _():
        carry_ref[...] = jnp.zeros_like(carry_ref)

    fc = lf + carry_ref[0:1, :]
    carry_ref[...] = jnp.broadcast_to(fc[tm - 1:tm, :], carry_ref.shape)

    f2 = fc * LOG2E
    hi = f2.astype(BF16).astype(F32)
    rem = f2 - hi
    mid = rem.astype(BF16).astype(F32)
    lo = rem - mid
    lane = lax.broadcasted_iota(jnp.int32, f2.shape, 1)
    j = (lane % HEAD_DIM) % AUG
    part = jnp.where(j % 3 == 0, hi, jnp.where(j % 3 == 1, mid, lo))
    aug_q = jnp.where(j < 3, 1.0, part)
    aug_k = jnp.where(j < 3, -part, 1.0)

    zq = proj(0, A_WIDTH) * (HEAD_DIM ** -0.5 * LOG2E)
    zk = proj(A_WIDTH, 2 * A_WIDTH)
    for hd in range(HEADS):
        p = hd // 2
        in_head = (lane >= (hd % 2) * HEAD_DIM) & (lane < (hd % 2 + 1) * HEAD_DIM)
        base = _aug_base(hd)
        in_aug = (lane >= base) & (lane < base + AUG)
        q_ref[hd] = jnp.where(in_head, zq[:, p * LANES:(p + 1) * LANES],
                              jnp.where(in_aug, aug_q, 0.0)).astype(BF16)
        k_ref[hd] = jnp.where(in_head, zk[:, p * LANES:(p + 1) * LANES],
                              jnp.where(in_aug, aug_k, 0.0)).astype(BF16)

    vt = lax.dot_general(wvt_ref[...], hb, (((1,), (1,)), ((), ())), preferred_element_type=F32)
    ones_row = (lax.broadcasted_iota(jnp.int32, (LANES - HEAD_DIM, tm), 0) == 0).astype(BF16)
    for hd in range(HEADS):
        vt_ref[0, hd * LANES:hd * LANES + HEAD_DIM, :] = vt[hd * HEAD_DIM:(hd + 1) * HEAD_DIM, :].astype(BF16)
        vt_ref[0, hd * LANES + HEAD_DIM:(hd + 1) * LANES, :] = ones_row

    o = 2 * A_WIDTH
    pu_ref[...] = proj(o, o + POOL_WIDTH).astype(BF16)
    o += POOL_WIDTH
    cv_ref[...] = proj(o, o + 3 * CONV_WIDTH).astype(BF16)
    o += 3 * CONV_WIDTH
    d = x.shape[1]
    for jj in range(3):
        gl_ref[:, jj * d:(jj + 1) * d] = proj(o + jj * d, o + (jj + 1) * d).astype(BF16)


def _inproj(x2, mod, g, wm, wvt, wf, bf, seq):
    t, d = x2.shape
    nt = t // TM
    tiles_per_seq = seq // TM
    ncols = wm.shape[1]
    out_shape = (
        jax.ShapeDtypeStruct((HEADS, t, LANES), BF16),
        jax.ShapeDtypeStruct((HEADS, t, LANES), BF16),
        jax.ShapeDtypeStruct((nt, HEADS * LANES, TM), BF16),
        jax.ShapeDtypeStruct((t, POOL_WIDTH), BF16),
        jax.ShapeDtypeStruct((t, 3 * CONV_WIDTH), BF16),
        jax.ShapeDtypeStruct((t, 3 * d), BF16),
    )
    out_specs = (
        pl.BlockSpec((HEADS, TM, LANES), lambda i: (0, i, 0)),
        pl.BlockSpec((HEADS, TM, LANES), lambda i: (0, i, 0)),
        pl.BlockSpec((1, HEADS * LANES, TM), lambda i: (i, 0, 0)),
        pl.BlockSpec((TM, POOL_WIDTH), lambda i: (i, 0)),
        pl.BlockSpec((TM, 3 * CONV_WIDTH), lambda i: (i, 0)),
        pl.BlockSpec((TM, 3 * d), lambda i: (i, 0)),
    )
    return pl.pallas_call(
        functools.partial(_inproj_kernel, tiles_per_seq),
        grid=(nt,),
        in_specs=[
            pl.BlockSpec((TM, d), lambda i: (i, 0)),
            pl.BlockSpec((2, 1, 1, d), lambda i: (0, i // tiles_per_seq, 0, 0)),
            _const_spec((1, d)),
            _const_spec((d, ncols)),
            _const_spec((A_WIDTH, d)),
            _const_spec((d, LANES)),
            _const_spec((1, LANES)),
        ],
        out_specs=out_specs,
        out_shape=out_shape,
        scratch_shapes=[pltpu.VMEM((8, LANES), F32)],
        compiler_params=pltpu.CompilerParams(
            dimension_semantics=("arbitrary",), vmem_limit_bytes=VMEM_LIMIT),
        name="inproj",
    )(x2, mod, g, wm, wvt, wf, bf)


def _nt_dot(a, b):
    return lax.dot_general(a, b, (((1,), (1,)), ((), ())), preferred_element_type=F32)


def _attn_kernel(q_ref, k_ref, vt_ref, o_ref):
    seq = q_ref.shape[1]
    nq = seq // TQ
    half = TK // 2
    steps = [(qi, kt) for qi in range(nq) for kt in range(qi + 1)]
    mask0 = (lax.broadcasted_iota(jnp.int32, (half, TQ), 0)
             <= lax.broadcasted_iota(jnp.int32, (half, TQ), 1))
    mask1 = (lax.broadcasted_iota(jnp.int32, (half, half), 0)
             <= lax.broadcasted_iota(jnp.int32, (half, half), 1))

    def logits(step, hh):
        qi, kt = step
        q = q_ref[hh, qi * TQ:(qi + 1) * TQ, :]
        if kt < qi:
            st = _nt_dot(k_ref[hh, kt * TK:(kt + 1) * TK, :], q)
            return (st,), jnp.max(st, axis=0, keepdims=True)
        st0 = jnp.where(mask0, _nt_dot(k_ref[hh, kt * TK:kt * TK + half, :], q), NEG_INF)
        st1 = jnp.where(mask1, _nt_dot(k_ref[hh, kt * TK + half:(kt + 1) * TK, :], q[half:]), NEG_INF)
        c0 = jnp.max(st0, axis=0, keepdims=True)
        c1 = jnp.max(st1, axis=0, keepdims=True)
        return (st0, st1), jnp.concatenate([c0[:, :half], jnp.maximum(c0[:, half:], c1)], axis=1)

    def update(step, hh, sts, cmax, m, acc):
        qi, kt = step
        rows = slice(hh * LANES, (hh + 1) * LANES)
        m_new = cmax if m is None else jnp.maximum(m, cmax)
        if kt < qi:
            pt = jnp.exp2(sts[0] - m_new).astype(BF16)
            pv = jnp.dot(vt_ref[kt, rows, :], pt, preferred_element_type=F32)
        else:
            pt0 = jnp.exp2(sts[0] - m_new).astype(BF16)
            pt1 = jnp.exp2(sts[1] - m_new[:, half:]).astype(BF16)
            pv = jnp.dot(vt_ref[kt, rows, 0:half], pt0, preferred_element_type=F32)
            pv1 = jnp.dot(vt_ref[kt, rows, half:], pt1, preferred_element_type=F32)
            pv = jnp.concatenate([pv[:, :half], pv[:, half:] + pv1], axis=1)
        if m is not None:
            pv = jnp.exp2(m - m_new) * acc + pv
        return m_new, pv

    cur = [logits(steps[0], hh) for hh in range(2)]
    state = [(None, None), (None, None)]
    for s, step in enumerate(steps):
        qi, kt = step
        nxt = [None, None]
        for hh in range(2):
            if s + 1 < len(steps):
                nxt[hh] = logits(steps[s + 1], hh)
            state[hh] = update(step, hh, *cur[hh], *state[hh])
        cur = nxt
        if kt == qi:
            outs = [acc[0:HEAD_DIM] / acc[HEAD_DIM:HEAD_DIM + 1] for _, acc in state]
            o_ref[qi * TQ:(qi + 1) * TQ, :] = jnp.concatenate(outs, axis=0).T.astype(BF16)
            state = [(None, None), (None, None)]


def _attention(q8, k8, vt8, seq):
    _, t, _ = q8.shape
    nb = t // seq
    nk = seq // TK
    return pl.pallas_call(
        _attn_kernel,
        grid=(nb, PAIRS),
        in_specs=[
            pl.BlockSpec((2, seq, LANES), lambda b, p: (p, b, 0)),
            pl.BlockSpec((2, seq, LANES), lambda b, p: (p, b, 0)),
            pl.BlockSpec((nk, 2 * LANES, TK), lambda b, p: (b, p, 0)),
        ],
        out_specs=pl.BlockSpec((seq, LANES), lambda b, p: (b, p)),
        out_shape=jax.ShapeDtypeStruct((t, A_WIDTH), BF16),
        compiler_params=pltpu.CompilerParams(
            dimension_semantics=("arbitrary", "arbitrary"), vmem_limit_bytes=VMEM_LIMIT),
        name="attn",
    )(q8, k8, vt8)


def _merge_kernel(tiles_per_seq, o_ref, pu_ref, puh_ref, cv_ref, cvh_ref, gl_ref, x_ref, gate_ref,
                  wpool_ref, pscale_ref, convw_ref, wbr_ref, wout_ref, g_ref, out_ref):
    i = pl.program_id(0)
    tile_in_seq = i % tiles_per_seq
    keep = (tile_in_seq != 0).astype(F32)
    tm = pu_ref.shape[0]

    ext = jnp.concatenate([puh_ref[...].astype(F32) * keep, pu_ref[...].astype(F32)], axis=0)
    a2 = ext + pltpu.roll(ext, 1, 0)
    a4 = a2 + pltpu.roll(a2, 2, 0)
    a8 = a4 + pltpu.roll(a4, 4, 0)
    a16 = a8 + pltpu.roll(a8, 8, 0)
    lane = lax.broadcasted_iota(jnp.int32, ext.shape, 1)
    row = lax.broadcasted_iota(jnp.int32, ext.shape, 0)
    gd = POOL_WIDTH // len(POOL_WINDOWS)
    win = jnp.where(lane < gd, a2, jnp.where(lane < 2 * gd, a4, jnp.where(lane < 3 * gd, a8, a16)))
    wsz = jnp.where(lane < gd, 2.0, jnp.where(lane < 2 * gd, 4.0, jnp.where(lane < 3 * gd, 8.0, 16.0)))
    frames = (tile_in_seq * tm + row - (HALO - 1)).astype(F32)
    cnt = jnp.maximum(jnp.minimum(frames, wsz), 1.0)
    pm = (win / cnt - ext)[HALO:]
    br_b = jnp.dot(pm.astype(BF16), wpool_ref[...], preferred_element_type=F32) * pscale_ref[...]

    cve = jnp.concatenate([cvh_ref[...].astype(F32) * keep, cv_ref[...].astype(F32)], axis=0)
    u = cve[:, 2 * CONV_WIDTH:] * cve[:, :CONV_WIDTH]
    cw = convw_ref[...]
    y = cw[0:1] * pltpu.roll(u, 2, 0) + cw[1:2] * pltpu.roll(u, 1, 0) + cw[2:3] * u
    br_c = (cve[:, CONV_WIDTH:2 * CONV_WIDTH] * y)[HALO:]

    d = x_ref.shape[1]

    def gate(j):
        gl = gl_ref[:, j * d:(j + 1) * d].astype(F32)
        return 1.0 / (1.0 + jnp.exp(-gl))

    o1 = A_WIDTH + POOL_WIDTH
    merged = gate(0) * jnp.dot(o_ref[...], wbr_ref[0:A_WIDTH, :], preferred_element_type=F32)
    merged += gate(1) * jnp.dot(br_b.astype(BF16), wbr_ref[A_WIDTH:o1, :], preferred_element_type=F32)
    merged += gate(2) * jnp.dot(br_c.astype(BF16), wbr_ref[o1:, :], preferred_element_type=F32)
    yo = jnp.dot(merged.astype(BF16), wout_ref[...], preferred_element_type=F32)
    out_ref[...] = x_ref[...] + gate_ref[0, 0] * _rms(yo, g_ref[...])


def _merge(o, pu, cv, gl, x2, gate, wpool, pscale, convw, wbr, wout, g, seq):
    t, d = x2.shape
    nt = t // TM
    tiles_per_seq = seq // TM
    hb = TM // HALO

    def halo_map(i):
        return (jnp.maximum(i * hb - 1, 0), 0)

    return pl.pallas_call(
        functools.partial(_merge_kernel, tiles_per_seq),
        grid=(nt,),
        in_specs=[
            pl.BlockSpec((TM, A_WIDTH), lambda i: (i, 0)),
            pl.BlockSpec((TM, POOL_WIDTH), lambda i: (i, 0)),
            pl.BlockSpec((HALO, POOL_WIDTH), halo_map),
            pl.BlockSpec((TM, 3 * CONV_WIDTH), lambda i: (i, 0)),
            pl.BlockSpec((HALO, 3 * CONV_WIDTH), halo_map),
            pl.BlockSpec((TM, 3 * d), lambda i: (i, 0)),
            pl.BlockSpec((TM, d), lambda i: (i, 0)),
            pl.BlockSpec((1, 1, 1, d), lambda i: (0, i // tiles_per_seq, 0, 0)),
            _const_spec(wpool.shape),
            _const_spec(pscale.shape),
            _const_spec(convw.shape),
            _const_spec(wbr.shape),
            _const_spec(wout.shape),
            _const_spec(g.shape),
        ],
        out_specs=pl.BlockSpec((TM, d), lambda i: (i, 0)),
        out_shape=jax.ShapeDtypeStruct((t, d), F32),
        compiler_params=pltpu.CompilerParams(
            dimension_semantics=("arbitrary",), vmem_limit_bytes=VMEM_LIMIT),
        name="merge",
    )(o, pu, pu, cv, cv, gl, x2, gate, wpool, pscale, convw, wbr, wout, g)


def _mlp_kernel(x_ref, mod_ref, gpre_ref, w1_ref, w2_ref, gpost_ref, out_ref):
    x = x_ref[...]
    shift = mod_ref[0, 0]
    scale = mod_ref[1, 0]
    gate = mod_ref[2, 0]
    hb = (_rms(x, gpre_ref[...]) * (1.0 + scale) + shift).astype(BF16)
    dff = w1_ref.shape[1]
    acc = jnp.zeros(x.shape, F32)
    for c in range(dff // FF_CHUNK):
        up = jnp.dot(hb, w1_ref[:, c * FF_CHUNK:(c + 1) * FF_CHUNK], preferred_element_type=F32)
        act = jnp.square(jnp.maximum(up, 0.0)).astype(BF16)
        acc += jnp.dot(act, w2_ref[c * FF_CHUNK:(c + 1) * FF_CHUNK, :], preferred_element_type=F32)
    out_ref[...] = x + gate * _rms(acc, gpost_ref[...])


def _mlp(x2, mod, gpre, w1, w2, gpost, seq):
    t, d = x2.shape
    nt = t // TM
    tiles_per_seq = seq // TM
    return pl.pallas_call(
        _mlp_kernel,
        grid=(nt,),
        in_specs=[
            pl.BlockSpec((TM, d), lambda i: (i, 0)),
            pl.BlockSpec((3, 1, 1, d), lambda i: (0, i // tiles_per_seq, 0, 0)),
            _const_spec(gpre.shape),
            _const_spec(w1.shape),
            _const_spec(w2.shape),
            _const_spec(gpost.shape),
        ],
        out_specs=pl.BlockSpec((TM, d), lambda i: (i, 0)),
        out_shape=jax.ShapeDtypeStruct((t, d), F32),
        compiler_params=pltpu.CompilerParams(
            dimension_semantics=("arbitrary",), vmem_limit_bytes=VMEM_LIMIT),
        name="mlp",
    )(x2, mod, gpre, w1, w2, gpost)


def _block_diag(w):
    g, c, dd = w.shape
    eye = jnp.eye(g, dtype=w.dtype)
    return (eye[:, None, :, None] * w[:, :, None, :]).reshape(g * c, g * dd)


def kernel(x, c, w_ada, b_ada, g_mix_pre, g_mix_post, g_ff_pre, g_ff_post, w_in, b_f, w_pool, pool_scale,
           conv_w, w_branch, w_out, w_ff1, w_ff2):
    nb, seq, d = x.shape
    depth = w_ada.shape[0]
    assert seq % TM == 0 and seq % TQ == 0 and seq % TK == 0 and d % LANES == 0
    t = nb * seq
    mod = _ada(c, w_ada, b_ada).reshape(depth, 6, nb, 1, d)
    x2 = x.reshape(t, d)
    vo = 2 * A_WIDTH
    fo = 3 * A_WIDTH
    ro = fo + HEADS
    head_of_lane = np.zeros((LANES,), np.int32)
    lane_used = np.zeros((LANES,), bool)
    for hd in range(HEADS):
        head_of_lane[_aug_base(hd):_aug_base(hd) + AUG] = hd
        lane_used[_aug_base(hd):_aug_base(hd) + AUG] = True
    for l in range(depth):
        wi = w_in[l]
        wm = jnp.concatenate([wi[:, :vo], wi[:, ro:]], axis=1).astype(BF16)
        wvt = wi[:, vo:fo].T.astype(BF16)
        wf = jnp.where(lane_used, wi[:, fo:ro][:, head_of_lane], 0.0).astype(BF16)
        bf = jnp.where(lane_used, b_f[l][head_of_lane], 0.0).reshape(1, LANES)
        row = lambda a: a[l].reshape(1, -1)
        q8, k8, vt8, pu, cv, gl = _inproj(x2, mod[l, 0:2], row(g_mix_pre), wm, wvt, wf, bf, seq)
        o = _attention(q8, k8, vt8, seq)
        x2 = _merge(o, pu, cv, gl, x2, mod[l, 2:3], _block_diag(w_pool[l]).astype(BF16), row(pool_scale),
                    conv_w[l], w_branch[l].astype(BF16), w_out[l].astype(BF16), row(g_mix_post), seq)
        x2 = _mlp(x2, mod[l, 3:6], row(g_ff_pre), w_ff1[l].astype(BF16), w_ff2[l].astype(BF16),
                  row(g_ff_post), seq)
    return x2.reshape(nb, seq, d)
```

```python
import functools

import jax
import jax.numpy as jnp
import numpy as np
from jax import lax
from jax.experimental import pallas as pl
from jax.experimental.pallas import tpu as pltpu

F32 = jnp.float32
BF16 = jnp.bfloat16

LANES = 128
HEAD_DIM = 64
HEADS = 8
PAIRS = HEADS // 2
A_WIDTH = HEADS * HEAD_DIM
POOL_WINDOWS = (2, 4, 8, 16)
POOL_WIDTH = 256
CONV_WIDTH = 256
HALO = 16
RMS_EPS = 1e-6
NEG_INF = -1e30
LOG2E = 1.4426950408889634
AUG = 6
VMEM_LIMIT = 56 * 1024 * 1024

TM = 512
TQ = 512
TK = 512
FF_CHUNK = 1024


def _const_spec(shape):
    n = len(shape)
    return pl.BlockSpec(shape, lambda *_: (0,) * n, pipeline_mode=pl.Buffered(1))


def _rms(x, g):
    ms = jnp.mean(x * x, axis=-1, keepdims=True)
    return x * lax.rsqrt(ms + RMS_EPS) * g


def _nt_dot(a, b):
    return lax.dot_general(a, b, (((1,), (1,)), ((), ())), preferred_element_type=F32)


def _ada_kernel(c_ref, w_ref, b_ref, o_ref):
    c = c_ref[...]
    sc = c * (1.0 / (1.0 + jnp.exp(-c)))
    o_ref[0, 0] = jnp.dot(sc, w_ref[0], precision=lax.Precision.HIGHEST,
                          preferred_element_type=F32) + b_ref[0, 0]


def _ada(c, w_ada, b_ada):
    depth, d, d6 = w_ada.shape
    nb = c.shape[0]
    n = d6 // d
    return pl.pallas_call(
        _ada_kernel,
        grid=(depth, n),
        in_specs=[
            pl.BlockSpec((nb, d), lambda l, j: (0, 0)),
            pl.BlockSpec((1, d, d), lambda l, j: (l, 0, j)),
            pl.BlockSpec((1, 1, 1, d), lambda l, j: (l, j, 0, 0)),
        ],
        out_specs=pl.BlockSpec((1, 1, nb, d), lambda l, j: (l, j, 0, 0)),
        out_shape=jax.ShapeDtypeStruct((depth, n, nb, d), F32),
        compiler_params=pltpu.CompilerParams(
            dimension_semantics=("arbitrary", "arbitrary"), vmem_limit_bytes=VMEM_LIMIT),
        name="ada",
    )(c, w_ada, b_ada.reshape(depth, n, 1, d))


def _aug_base(h):
    return (HEAD_DIM if h % 2 == 0 else 0) + AUG * (h // 2)


def _inproj_kernel(tiles_per_seq, x_ref, mod_ref, g_ref, wqk_ref, wvt_ref, wrest_ref, wf_ref, bf_ref,
                   q_ref, k_ref, vt_ref, pu_ref, cv_ref, gl_ref, carry_ref):
    i = pl.program_id(0)
    x = x_ref[...]
    shift = mod_ref[0, 0]
    scale = mod_ref[1, 0]
    h = _rms(x, g_ref[...]) * (1.0 + scale) + shift
    hb = h.astype(BF16)
    tm = x.shape[0]

    def proj(w_ref, lo, hi):
        return jnp.dot(hb, w_ref[:, lo:hi], preferred_element_type=F32)

    zf = jnp.dot(hb, wf_ref[...], preferred_element_type=F32) + bf_ref[...]
    lf = jnp.minimum(zf, 0.0) - jnp.log(1.0 + jnp.exp(-jnp.abs(zf)))
    row = lax.broadcasted_iota(jnp.int32, lf.shape, 0)
    step = 1
    while step < tm:
        lf = lf + jnp.where(row >= step, pltpu.roll(lf, step, 0), 0.0)
        step *= 2

    @pl.when(i % tiles_per_seq == 0)
    def _():
        carry_ref[...] = jnp.zeros_like(carry_ref)

    fc = lf + carry_ref[0:1, :]
    carry_ref[...] = jnp.broadcast_to(fc[tm - 1:tm, :], carry_ref.shape)

    f2 = fc * LOG2E
    hi = f2.astype(BF16).astype(F32)
    rem = f2 - hi
    mid = rem.astype(BF16).astype(F32)
    lo = rem - mid
    lane = lax.broadcasted_iota(jnp.int32, f2.shape, 1)
    j = (lane % HEAD_DIM) % AUG
    part = jnp.where(j % 3 == 0, hi, jnp.where(j % 3 == 1, mid, lo))
    aug_q = jnp.where(j < 3, 1.0, part)
    aug_k = jnp.where(j < 3, -part, 1.0)

    zq = proj(wqk_ref, 0, A_WIDTH) * (HEAD_DIM ** -0.5 * LOG2E)
    zk = proj(wqk_ref, A_WIDTH, 2 * A_WIDTH)
    for hd in range(HEADS):
        p = hd // 2
        in_head = (lane >= (hd % 2) * HEAD_DIM) & (lane < (hd % 2 + 1) * HEAD_DIM)
        base = _aug_base(hd)
        in_aug = (lane >= base) & (lane < base + AUG)
        q_ref[hd] = jnp.where(in_head, zq[:, p * LANES:(p + 1) * LANES],
                              jnp.where(in_aug, aug_q, 0.0)).astype(BF16)
        k_ref[hd] = jnp.where(in_head, zk[:, p * LANES:(p + 1) * LANES],
                              jnp.where(in_aug, aug_k, 0.0)).astype(BF16)

    vt = _nt_dot(wvt_ref[...], hb)
    ones_row = (lax.broadcasted_iota(jnp.int32, (LANES - HEAD_DIM, tm), 0) == 0).astype(BF16)
    for hd in range(HEADS):
        vt_ref[0, hd * LANES:hd * LANES + HEAD_DIM, :] = vt[hd * HEAD_DIM:(hd + 1) * HEAD_DIM, :].astype(BF16)
        vt_ref[0, hd * LANES + HEAD_DIM:(hd + 1) * LANES, :] = ones_row

    pu_ref[...] = proj(wrest_ref, 0, POOL_WIDTH).astype(BF16)
    o = POOL_WIDTH
    cv_ref[...] = proj(wrest_ref, o, o + 3 * CONV_WIDTH).astype(BF16)
    o += 3 * CONV_WIDTH
    d = x.shape[1]
    for jj in range(3):
        gl_ref[:, jj * d:(jj + 1) * d] = proj(wrest_ref, o + jj * d, o + (jj + 1) * d).astype(BF16)


def _inproj(x2, mod, g, wqk, wvt, wrest, wf, bf, seq):
    t, d = x2.shape
    nt = t // TM
    tiles_per_seq = seq // TM
    out_shape = (
        jax.ShapeDtypeStruct((HEADS, t, LANES), BF16),
        jax.ShapeDtypeStruct((HEADS, t, LANES), BF16),
        jax.ShapeDtypeStruct((nt, HEADS * LANES, TM), BF16),
        jax.ShapeDtypeStruct((t, POOL_WIDTH), BF16),
        jax.ShapeDtypeStruct((t, 3 * CONV_WIDTH), BF16),
        jax.ShapeDtypeStruct((t, 3 * d), BF16),
    )
    out_specs = (
        pl.BlockSpec((HEADS, TM, LANES), lambda i: (0, i, 0)),
        pl.BlockSpec((HEADS, TM, LANES), lambda i: (0, i, 0)),
        pl.BlockSpec((1, HEADS * LANES, TM), lambda i: (i, 0, 0)),
        pl.BlockSpec((TM, POOL_WIDTH), lambda i: (i, 0)),
        pl.BlockSpec((TM, 3 * CONV_WIDTH), lambda i: (i, 0)),
        pl.BlockSpec((TM, 3 * d), lambda i: (i, 0)),
    )
    return pl.pallas_call(
        functools.partial(_inproj_kernel, tiles_per_seq),
        grid=(nt,),
        in_specs=[
            pl.BlockSpec((TM, d), lambda i: (i, 0)),
            pl.BlockSpec((2, 1, 1, d), lambda i: (0, i // tiles_per_seq, 0, 0)),
            _const_spec((1, d)),
            _const_spec(wqk.shape),
            _const_spec(wvt.shape),
            _const_spec(wrest.shape),
            _const_spec((d, LANES)),
            _const_spec((1, LANES)),
        ],
        out_specs=out_specs,
        out_shape=out_shape,
        scratch_shapes=[pltpu.VMEM((8, LANES), F32)],
        compiler_params=pltpu.CompilerParams(
            dimension_semantics=("arbitrary",), vmem_limit_bytes=VMEM_LIMIT),
        name="inproj",
    )(x2, mod, g, wqk, wvt, wrest, wf, bf)


def _attn_kernel(q_ref, k_ref, vt_ref, o_ref):
    seq = q_ref.shape[1]
    nq = seq // TQ
    half = TK // 2
    steps = [(qi, kt) for qi in range(nq) for kt in range(qi + 1)]
    mask0 = (lax.broadcasted_iota(jnp.int32, (half, TQ), 0)
             <= lax.broadcasted_iota(jnp.int32, (half, TQ), 1))
    mask1 = (lax.broadcasted_iota(jnp.int32, (half, half), 0)
             <= lax.broadcasted_iota(jnp.int32, (half, half), 1))

    def logits(step, hh):
        qi, kt = step
        q = q_ref[hh, qi * TQ:(qi + 1) * TQ, :]
        if kt < qi:
            st = _nt_dot(k_ref[hh, kt * TK:(kt + 1) * TK, :], q)
            return (st,), jnp.max(st, axis=0, keepdims=True)
        st0 = jnp.where(mask0, _nt_dot(k_ref[hh, kt * TK:kt * TK + half, :], q), NEG_INF)
        st1 = jnp.where(mask1, _nt_dot(k_ref[hh, kt * TK + half:(kt + 1) * TK, :], q[half:]), NEG_INF)
        c0 = jnp.max(st0, axis=0, keepdims=True)
        c1 = jnp.max(st1, axis=0, keepdims=True)
        return (st0, st1), jnp.concatenate([c0[:, :half], jnp.maximum(c0[:, half:], c1)], axis=1)

    def update(step, hh, sts, cmax, m, acc):
        qi, kt = step
        rows = slice(hh * LANES, (hh + 1) * LANES)
        m_new = cmax if m is None else jnp.maximum(m, cmax)
        if kt < qi:
            pt = jnp.exp2(sts[0] - m_new).astype(BF16)
            pv = jnp.dot(vt_ref[kt, rows, :], pt, preferred_element_type=F32)
        else:
            pt0 = jnp.exp2(sts[0] - m_new).astype(BF16)
            pt1 = jnp.exp2(sts[1] - m_new[:, half:]).astype(BF16)
            pv = jnp.dot(vt_ref[kt, rows, 0:half], pt0, preferred_element_type=F32)
            pv1 = jnp.dot(vt_ref[kt, rows, half:], pt1, preferred_element_type=F32)
            pv = jnp.concatenate([pv[:, :half], pv[:, half:] + pv1], axis=1)
        if m is not None:
            pv = jnp.exp2(m - m_new) * acc + pv
        return m_new, pv

    cur = [logits(steps[0], hh) for hh in range(2)]
    state = [(None, None), (None, None)]
    for s, step in enumerate(steps):
        qi, kt = step
        nxt = [None, None]
        for hh in range(2):
            if s + 1 < len(steps):
                nxt[hh] = logits(steps[s + 1], hh)
            state[hh] = update(step, hh, *cur[hh], *state[hh])
        cur = nxt
        if kt == qi:
            outs = [acc[0:HEAD_DIM] / acc[HEAD_DIM:HEAD_DIM + 1] for _, acc in state]
            o_ref[qi * TQ:(qi + 1) * TQ, :] = jnp.concatenate(outs, axis=0).T.astype(BF16)
            state = [(None, None), (None, None)]


def _attention(q8, k8, vt8, seq):
    _, t, _ = q8.shape
    nb = t // seq
    nk = seq // TK
    return pl.pallas_call(
        _attn_kernel,
        grid=(nb, PAIRS),
        in_specs=[
            pl.BlockSpec((2, seq, LANES), lambda b, p: (p, b, 0)),
            pl.BlockSpec((2, seq, LANES), lambda b, p: (p, b, 0)),
            pl.BlockSpec((nk, 2 * LANES, TK), lambda b, p: (b, p, 0)),
        ],
        out_specs=pl.BlockSpec((seq, LANES), lambda b, p: (b, p)),
        out_shape=jax.ShapeDtypeStruct((t, A_WIDTH), BF16),
        compiler_params=pltpu.CompilerParams(
            dimension_semantics=("arbitrary", "arbitrary"), vmem_limit_bytes=VMEM_LIMIT),
        name="attn",
    )(q8, k8, vt8)


def _merge_kernel(tiles_per_seq, o_ref, pu_ref, puh_ref, cv_ref, cvh_ref, gl_ref, x_ref, gate_ref,
                  wpool_ref, pscale_ref, convw_ref, wbr_ref, wout_ref, g_ref, out_ref):
    i = pl.program_id(0)
    tile_in_seq = i % tiles_per_seq
    keep = (tile_in_seq != 0).astype(F32)
    tm = pu_ref.shape[0]

    ext = jnp.concatenate([puh_ref[...].astype(F32) * keep, pu_ref[...].astype(F32)], axis=0)
    a2 = ext + pltpu.roll(ext, 1, 0)
    a4 = a2 + pltpu.roll(a2, 2, 0)
    a8 = a4 + pltpu.roll(a4, 4, 0)
    a16 = a8 + pltpu.roll(a8, 8, 0)
    lane = lax.broadcasted_iota(jnp.int32, ext.shape, 1)
    row = lax.broadcasted_iota(jnp.int32, ext.shape, 0)
    gd = POOL_WIDTH // len(POOL_WINDOWS)
    win = jnp.where(lane < gd, a2, jnp.where(lane < 2 * gd, a4, jnp.where(lane < 3 * gd, a8, a16)))
    wsz = jnp.where(lane < gd, 2.0, jnp.where(lane < 2 * gd, 4.0, jnp.where(lane < 3 * gd, 8.0, 16.0)))
    frames = (tile_in_seq * tm + row - (HALO - 1)).astype(F32)
    cnt = jnp.maximum(jnp.minimum(frames, wsz), 1.0)
    pm = (win / cnt - ext)[HALO:]
    br_b = jnp.dot(pm.astype(BF16), wpool_ref[...], preferred_element_type=F32) * pscale_ref[...]

    cve = jnp.concatenate([cvh_ref[...].astype(F32) * keep, cv_ref[...].astype(F32)], axis=0)
    u = cve[:, 2 * CONV_WIDTH:] * cve[:, :CONV_WIDTH]
    cw = convw_ref[...]
    y = cw[0:1] * pltpu.roll(u, 2, 0) + cw[1:2] * pltpu.roll(u, 1, 0) + cw[2:3] * u
    br_c = (cve[:, CONV_WIDTH:2 * CONV_WIDTH] * y)[HALO:]

    d = x_ref.shape[1]

    def gated(j, branch, lo, hi):
        b = jnp.dot(branch, wbr_ref[lo:hi, :], preferred_element_type=F32)
        return b + b * jnp.tanh(gl_ref[:, j * d:(j + 1) * d].astype(F32))

    o1 = A_WIDTH + POOL_WIDTH
    merged = gated(0, o_ref[...], 0, A_WIDTH)
    merged += gated(1, br_b.astype(BF16), A_WIDTH, o1)
    merged += gated(2, br_c.astype(BF16), o1, o1 + CONV_WIDTH)
    yo = jnp.dot(merged.astype(BF16), wout_ref[...], preferred_element_type=F32)
    out_ref[...] = x_ref[...] + gate_ref[0, 0] * _rms(yo, g_ref[...])


def _merge(o, pu, cv, gl, x2, gate, wpool, pscale, convw, wbr, wout, g, seq):
    t, d = x2.shape
    nt = t // TM
    tiles_per_seq = seq // TM
    hb = TM // HALO

    def halo_map(i):
        return (jnp.maximum(i * hb - 1, 0), 0)

    return pl.pallas_call(
        functools.partial(_merge_kernel, tiles_per_seq),
        grid=(nt,),
        in_specs=[
            pl.BlockSpec((TM, A_WIDTH), lambda i: (i, 0)),
            pl.BlockSpec((TM, POOL_WIDTH), lambda i: (i, 0)),
            pl.BlockSpec((HALO, POOL_WIDTH), halo_map),
            pl.BlockSpec((TM, 3 * CONV_WIDTH), lambda i: (i, 0)),
            pl.BlockSpec((HALO, 3 * CONV_WIDTH), halo_map),
            pl.BlockSpec((TM, 3 * d), lambda i: (i, 0)),
            pl.BlockSpec((TM, d), lambda i: (i, 0)),
            pl.BlockSpec((1, 1, 1, d), lambda i: (0, i // tiles_per_seq, 0, 0)),
            _const_spec(wpool.shape),
            _const_spec(pscale.shape),
            _const_spec(convw.shape),
            _const_spec(wbr.shape),
            _const_spec(wout.shape),
            _const_spec(g.shape),
        ],
        out_specs=pl.BlockSpec((TM, d), lambda i: (i, 0)),
        out_shape=jax.ShapeDtypeStruct((t, d), F32),
        compiler_params=pltpu.CompilerParams(
            dimension_semantics=("arbitrary",), vmem_limit_bytes=VMEM_LIMIT),
        name="merge",
    )(o, pu, pu, cv, cv, gl, x2, gate, wpool, pscale, convw, wbr, wout, g)


def _mlp_kernel(x_ref, mod_ref, gpre_ref, w1_ref, w2_ref, gpost_ref, out_ref):
    x = x_ref[...]
    shift = mod_ref[0, 0]
    scale = mod_ref[1, 0]
    gate = mod_ref[2, 0]
    hb = (_rms(x, gpre_ref[...]) * (1.0 + scale) + shift).astype(BF16)
    dff = w1_ref.shape[1]
    acc = jnp.zeros(x.shape, F32)
    for c in range(dff // FF_CHUNK):
        up = jnp.dot(hb, w1_ref[:, c * FF_CHUNK:(c + 1) * FF_CHUNK], preferred_element_type=F32)
        act = jnp.square(jnp.maximum(up, 0.0)).astype(BF16)
        acc += jnp.dot(act, w2_ref[c * FF_CHUNK:(c + 1) * FF_CHUNK, :], preferred_element_type=F32)
    out_ref[...] = x + gate * _rms(acc, gpost_ref[...])


def _mlp(x2, mod, gpre, w1, w2, gpost, seq):
    t, d = x2.shape
    nt = t // TM
    tiles_per_seq = seq // TM
    return pl.pallas_call(
        _mlp_kernel,
        grid=(nt,),
        in_specs=[
            pl.BlockSpec((TM, d), lambda i: (i, 0)),
            pl.BlockSpec((3, 1, 1, d), lambda i: (0, i // tiles_per_seq, 0, 0)),
            _const_spec(gpre.shape),
            _const_spec(w1.shape),
            _const_spec(w2.shape),
            _const_spec(gpost.shape),
        ],
        out_specs=pl.BlockSpec((TM, d), lambda i: (i, 0)),
        out_shape=jax.ShapeDtypeStruct((t, d), F32),
        compiler_params=pltpu.CompilerParams(
            dimension_semantics=("arbitrary",), vmem_limit_bytes=VMEM_LIMIT),
        name="mlp",
    )(x2, mod, gpre, w1, w2, gpost)


def _block_diag(w):
    g, c, dd = w.shape
    eye = jnp.eye(g, dtype=w.dtype)
    return (eye[:, None, :, None] * w[:, :, None, :]).reshape(g * c, g * dd)


def kernel(x, c, w_ada, b_ada, g_mix_pre, g_mix_post, g_ff_pre, g_ff_post, w_in, b_f, w_pool, pool_scale,
           conv_w, w_branch, w_out, w_ff1, w_ff2):
    nb, seq, d = x.shape
    depth = w_ada.shape[0]
    assert seq % TM == 0 and seq % TQ == 0 and seq % TK == 0 and d % LANES == 0
    t = nb * seq
    mod = _ada(c, w_ada, b_ada).reshape(depth, 6, nb, 1, d)
    x2 = x.reshape(t, d)
    vo = 2 * A_WIDTH
    fo = 3 * A_WIDTH
    ro = fo + HEADS
    n_plain = POOL_WIDTH + 3 * CONV_WIDTH
    rest_scale = np.concatenate([np.ones((n_plain,), np.float32), np.full((3 * d,), 0.5, np.float32)])
    head_of_lane = np.zeros((LANES,), np.int32)
    lane_used = np.zeros((LANES,), bool)
    for hd in range(HEADS):
        head_of_lane[_aug_base(hd):_aug_base(hd) + AUG] = hd
        lane_used[_aug_base(hd):_aug_base(hd) + AUG] = True
    for l in range(depth):
        wi = w_in[l]
        wqk = wi[:, :vo].astype(BF16)
        wvt = wi[:, vo:fo].T.astype(BF16)
        wrest = (wi[:, ro:] * rest_scale).astype(BF16)
        wf = jnp.where(lane_used, wi[:, fo:ro][:, head_of_lane], 0.0).astype(BF16)
        bf = jnp.where(lane_used, b_f[l][head_of_lane], 0.0).reshape(1, LANES)
        row = lambda a: a[l].reshape(1, -1)
        q8, k8, vt8, pu, cv, gl = _inproj(x2, mod[l, 0:2], row(g_mix_pre), wqk, wvt, wrest, wf, bf, seq)
        o = _attention(q8, k8, vt8, seq)
        x2 = _merge(o, pu, cv, gl, x2, mod[l, 2:3], _block_diag(w_pool[l]).astype(BF16), row(pool_scale),
                    conv_w[l], (w_branch[l] * 0.5).astype(BF16), w_out[l].astype(BF16), row(g_mix_post), seq)
        x2 = _mlp(x2, mod[l, 3:6], row(g_ff_pre), w_ff1[l].astype(BF16), w_ff2[l].astype(BF16),
                  row(g_ff_post), seq)
    return x2.reshape(nb, seq, d)
```

```python
import functools

import jax
import jax.numpy as jnp
import numpy as np
from jax import lax
from jax.experimental import pallas as pl
from jax.experimental.pallas import tpu as pltpu

F32 = jnp.float32
BF16 = jnp.bfloat16

LANES = 128
HEAD_DIM = 64
HEADS = 8
PAIRS = HEADS // 2
A_WIDTH = HEADS * HEAD_DIM
POOL_WINDOWS = (2, 4, 8, 16)
POOL_WIDTH = 256
CONV_WIDTH = 256
HALO = 16
RMS_EPS = 1e-6
NEG_INF = -1e30
LOG2E = 1.4426950408889634
AUG = 6
VMEM_LIMIT = 56 * 1024 * 1024

TM = 512
TQ = 512
TK = 512
FF_CHUNK = 1024
PREP_ROWS = 128


def _const_spec(shape):
    n = len(shape)
    return pl.BlockSpec(shape, lambda *_: (0,) * n, pipeline_mode=pl.Buffered(1))


def _layer_spec(w, l):
    n = w.ndim - 1
    return pl.BlockSpec((1,) + w.shape[1:], lambda *_: (l,) + (0,) * n, pipeline_mode=pl.Buffered(1))


def _rms(x, g):
    ms = jnp.mean(x * x, axis=-1, keepdims=True)
    return x * lax.rsqrt(ms + RMS_EPS) * g


def _nt_dot(a, b):
    return lax.dot_general(a, b, (((1,), (1,)), ((), ())), preferred_element_type=F32)


def _ada_kernel(c_ref, w_ref, b_ref, o_ref):
    c = c_ref[...]
    sc = c * (1.0 / (1.0 + jnp.exp(-c)))
    o_ref[0, 0] = jnp.dot(sc, w_ref[0], precision=lax.Precision.HIGHEST,
                          preferred_element_type=F32) + b_ref[0, 0]


def _ada(c, w_ada, b_ada):
    depth, d, d6 = w_ada.shape
    nb = c.shape[0]
    n = d6 // d
    return pl.pallas_call(
        _ada_kernel,
        grid=(depth, n),
        in_specs=[
            pl.BlockSpec((nb, d), lambda l, j: (0, 0)),
            pl.BlockSpec((1, d, d), lambda l, j: (l, 0, j)),
            pl.BlockSpec((1, 1, 1, d), lambda l, j: (l, j, 0, 0)),
        ],
        out_specs=pl.BlockSpec((1, 1, nb, d), lambda l, j: (l, j, 0, 0)),
        out_shape=jax.ShapeDtypeStruct((depth, n, nb, d), F32),
        compiler_params=pltpu.CompilerParams(
            dimension_semantics=("arbitrary", "arbitrary"), vmem_limit_bytes=VMEM_LIMIT),
        name="ada",
    )(c, w_ada, b_ada.reshape(depth, n, 1, d))


def _aug_base(h):
    return (HEAD_DIM if h % 2 == 0 else 0) + AUG * (h // 2)


def _inproj_kernel(tiles_per_seq, x_ref, mod_ref, g_ref, w_ref, spread_ref, bf_ref,
                   q_ref, k_ref, v_ref, pu_ref, cv_ref, gl_ref, carry_ref, wf_ref, wrest_ref):
    i = pl.program_id(0)
    d = x_ref.shape[1]
    fo = 3 * A_WIDTH
    n_plain = POOL_WIDTH + 3 * CONV_WIDTH

    @pl.when(i == 0)
    def _():
        wf_ref[...] = jnp.dot(w_ref[0, :, fo:fo + LANES], spread_ref[...],
                              preferred_element_type=F32).astype(BF16)
        col = lax.broadcasted_iota(jnp.int32, (1, wrest_ref.shape[1]), 1)
        col_scale = jnp.where(col < n_plain, 1.0, 0.5)
        for r in range(0, d, PREP_ROWS):
            blk = w_ref[0, r:r + PREP_ROWS, fo:].astype(F32)
            wrest_ref[r:r + PREP_ROWS, :] = (blk[:, HEADS:] * col_scale).astype(BF16)

    x = x_ref[...]
    shift = mod_ref[0, 0]
    scale = mod_ref[1, 0]
    h = _rms(x, g_ref[...]) * (1.0 + scale) + shift
    hb = h.astype(BF16)
    tm = x.shape[0]

    zf = jnp.dot(hb, wf_ref[...], preferred_element_type=F32) + bf_ref[...]
    lf = jnp.minimum(zf, 0.0) - jnp.log(1.0 + jnp.exp(-jnp.abs(zf)))
    row = lax.broadcasted_iota(jnp.int32, lf.shape, 0)
    step = 1
    while step < tm:
        lf = lf + jnp.where(row >= step, pltpu.roll(lf, step, 0), 0.0)
        step *= 2

    @pl.when(i % tiles_per_seq == 0)
    def _():
        carry_ref[...] = jnp.zeros_like(carry_ref)

    fc = lf + carry_ref[0:1, :]
    carry_ref[...] = jnp.broadcast_to(fc[tm - 1:tm, :], carry_ref.shape)

    f2 = fc * LOG2E
    hi = f2.astype(BF16).astype(F32)
    rem = f2 - hi
    mid = rem.astype(BF16).astype(F32)
    lo = rem - mid
    lane = lax.broadcasted_iota(jnp.int32, f2.shape, 1)
    j = (lane % HEAD_DIM) % AUG
    part = jnp.where(j % 3 == 0, hi, jnp.where(j % 3 == 1, mid, lo))
    aug_q = jnp.where(j < 3, 1.0, part)
    aug_k = jnp.where(j < 3, -part, 1.0)

    def proj(lo_, hi_):
        return jnp.dot(hb, w_ref[0, :, lo_:hi_], preferred_element_type=F32)

    zq = proj(0, A_WIDTH) * (HEAD_DIM ** -0.5 * LOG2E)
    zk = proj(A_WIDTH, 2 * A_WIDTH)
    zv = proj(2 * A_WIDTH, 3 * A_WIDTH)
    for hd in range(HEADS):
        p = hd // 2
        in_head = (lane >= (hd % 2) * HEAD_DIM) & (lane < (hd % 2 + 1) * HEAD_DIM)
        base = _aug_base(hd)
        in_aug = (lane >= base) & (lane < base + AUG)
        pair = slice(p * LANES, (p + 1) * LANES)
        q_ref[hd] = jnp.where(in_head, zq[:, pair], jnp.where(in_aug, aug_q, 0.0)).astype(BF16)
        k_ref[hd] = jnp.where(in_head, zk[:, pair], jnp.where(in_aug, aug_k, 0.0)).astype(BF16)
        ones_lane = HEAD_DIM if hd % 2 == 0 else 0
        v_ref[hd] = jnp.where(in_head, zv[:, pair], jnp.where(lane == ones_lane, 1.0, 0.0)).astype(BF16)

    def rest(lo_, hi_):
        return jnp.dot(hb, wrest_ref[:, lo_:hi_], preferred_element_type=F32)

    pu_ref[...] = rest(0, POOL_WIDTH).astype(BF16)
    cv_ref[...] = rest(POOL_WIDTH, n_plain).astype(BF16)
    for jj in range(3):
        gl_ref[:, jj * d:(jj + 1) * d] = rest(n_plain + jj * d, n_plain + (jj + 1) * d).astype(BF16)


def _inproj(x2, mod, g, w_in_bf, l, spread, bf, seq):
    t, d = x2.shape
    nt = t // TM
    tiles_per_seq = seq // TM
    n_rest = w_in_bf.shape[2] - 3 * A_WIDTH - HEADS
    out_shape = (
        jax.ShapeDtypeStruct((HEADS, t, LANES), BF16),
        jax.ShapeDtypeStruct((HEADS, t, LANES), BF16),
        jax.ShapeDtypeStruct((HEADS, t, LANES), BF16),
        jax.ShapeDtypeStruct((t, POOL_WIDTH), BF16),
        jax.ShapeDtypeStruct((t, 3 * CONV_WIDTH), BF16),
        jax.ShapeDtypeStruct((t, 3 * d), BF16),
    )
    head_spec = pl.BlockSpec((HEADS, TM, LANES), lambda i: (0, i, 0))
    out_specs = (
        head_spec, head_spec, head_spec,
        pl.BlockSpec((TM, POOL_WIDTH), lambda i: (i, 0)),
        pl.BlockSpec((TM, 3 * CONV_WIDTH), lambda i: (i, 0)),
        pl.BlockSpec((TM, 3 * d), lambda i: (i, 0)),
    )
    return pl.pallas_call(
        functools.partial(_inproj_kernel, tiles_per_seq),
        grid=(nt,),
        in_specs=[
            pl.BlockSpec((TM, d), lambda i: (i, 0)),
            pl.BlockSpec((2, 1, 1, d), lambda i: (0, i // tiles_per_seq, 0, 0)),
            _const_spec((1, d)),
            _layer_spec(w_in_bf, l),
            _const_spec(spread.shape),
            _const_spec((1, LANES)),
        ],
        out_specs=out_specs,
        out_shape=out_shape,
        scratch_shapes=[
            pltpu.VMEM((8, LANES), F32),
            pltpu.VMEM((d, LANES), BF16),
            pltpu.VMEM((d, n_rest), BF16),
        ],
        compiler_params=pltpu.CompilerParams(
            dimension_semantics=("arbitrary",), vmem_limit_bytes=VMEM_LIMIT),
        name="inproj",
    )(x2, mod, g, w_in_bf, spread, bf)


def _attn_kernel(q_ref, k_ref, v_ref, o_ref):
    seq = q_ref.shape[1]
    nq = seq // TQ
    half = TK // 2
    steps = [(qi, kt) for qi in range(nq) for kt in range(qi + 1)]
    mask0 = (lax.broadcasted_iota(jnp.int32, (half, TQ), 0)
             <= lax.broadcasted_iota(jnp.int32, (half, TQ), 1))
    mask1 = (lax.broadcasted_iota(jnp.int32, (half, half), 0)
             <= lax.broadcasted_iota(jnp.int32, (half, half), 1))
    vts = [v_ref[hh].astype(F32).T.astype(BF16) for hh in range(2)]

    def logits(step, hh):
        qi, kt = step
        q = q_ref[hh, qi * TQ:(qi + 1) * TQ, :]
        if kt < qi:
            st = _nt_dot(k_ref[hh, kt * TK:(kt + 1) * TK, :], q)
            return (st,), jnp.max(st, axis=0, keepdims=True)
        st0 = jnp.where(mask0, _nt_dot(k_ref[hh, kt * TK:kt * TK + half, :], q), NEG_INF)
        st1 = jnp.where(mask1, _nt_dot(k_ref[hh, kt * TK + half:(kt + 1) * TK, :], q[half:]), NEG_INF)
        c0 = jnp.max(st0, axis=0, keepdims=True)
        c1 = jnp.max(st1, axis=0, keepdims=True)
        return (st0, st1), jnp.concatenate([c0[:, :half], jnp.maximum(c0[:, half:], c1)], axis=1)

    def update(step, hh, sts, cmax, m, acc):
        qi, kt = step
        vt = vts[hh]
        m_new = cmax if m is None else jnp.maximum(m, cmax)
        if kt < qi:
            pt = jnp.exp2(sts[0] - m_new).astype(BF16)
            pv = jnp.dot(vt[:, kt * TK:(kt + 1) * TK], pt, preferred_element_type=F32)
        else:
            pt0 = jnp.exp2(sts[0] - m_new).astype(BF16)
            pt1 = jnp.exp2(sts[1] - m_new[:, half:]).astype(BF16)
            pv = jnp.dot(vt[:, kt * TK:kt * TK + half], pt0, preferred_element_type=F32)
            pv1 = jnp.dot(vt[:, kt * TK + half:(kt + 1) * TK], pt1, preferred_element_type=F32)
            pv = jnp.concatenate([pv[:, :half], pv[:, half:] + pv1], axis=1)
        if m is not None:
            pv = jnp.exp2(m - m_new) * acc + pv
        return m_new, pv

    cur = [logits(steps[0], hh) for hh in range(2)]
    state = [(None, None), (None, None)]
    for s, step in enumerate(steps):
        qi, kt = step
        nxt = [None, None]
        for hh in range(2):
            if s + 1 < len(steps):
                nxt[hh] = logits(steps[s + 1], hh)
            state[hh] = update(step, hh, *cur[hh], *state[hh])
        cur = nxt
        if kt == qi:
            a0, a1 = state[0][1], state[1][1]
            out = jnp.concatenate([a0[0:HEAD_DIM] / a0[HEAD_DIM:HEAD_DIM + 1],
                                   a1[HEAD_DIM:] / a1[0:1]], axis=0)
            o_ref[qi * TQ:(qi + 1) * TQ, :] = out.T.astype(BF16)
            state = [(None, None), (None, None)]


def _attention(q8, k8, v8, seq):
    _, t, _ = q8.shape
    nb = t // seq
    pair_spec = pl.BlockSpec((2, seq, LANES), lambda b, p: (p, b, 0))
    return pl.pallas_call(
        _attn_kernel,
        grid=(nb, PAIRS),
        in_specs=[pair_spec, pair_spec, pair_spec],
        out_specs=pl.BlockSpec((seq, LANES), lambda b, p: (b, p)),
        out_shape=jax.ShapeDtypeStruct((t, A_WIDTH), BF16),
        compiler_params=pltpu.CompilerParams(
            dimension_semantics=("arbitrary", "arbitrary"), vmem_limit_bytes=VMEM_LIMIT),
        name="attn",
    )(q8, k8, v8)


def _merge_kernel(tiles_per_seq, o_ref, pu_ref, puh_ref, cv_ref, cvh_ref, gl_ref, x_ref, gate_ref,
                  wpool_ref, pscale_ref, convw_ref, wbr_ref, wout_ref, g_ref, out_ref):
    i = pl.program_id(0)
    tile_in_seq = i % tiles_per_seq
    keep = (tile_in_seq != 0).astype(F32)
    tm = pu_ref.shape[0]

    ext = jnp.concatenate([puh_ref[...].astype(F32) * keep, pu_ref[...].astype(F32)], axis=0)
    a2 = ext + pltpu.roll(ext, 1, 0)
    a4 = a2 + pltpu.roll(a2, 2, 0)
    a8 = a4 + pltpu.roll(a4, 4, 0)
    a16 = a8 + pltpu.roll(a8, 8, 0)
    lane = lax.broadcasted_iota(jnp.int32, ext.shape, 1)
    row = lax.broadcasted_iota(jnp.int32, ext.shape, 0)
    gd = POOL_WIDTH // len(POOL_WINDOWS)
    win = jnp.where(lane < gd, a2, jnp.where(lane < 2 * gd, a4, jnp.where(lane < 3 * gd, a8, a16)))
    wsz = jnp.where(lane < gd, 2.0, jnp.where(lane < 2 * gd, 4.0, jnp.where(lane < 3 * gd, 8.0, 16.0)))
    frames = (tile_in_seq * tm + row - (HALO - 1)).astype(F32)
    cnt = jnp.maximum(jnp.minimum(frames, wsz), 1.0)
    pm = (win / cnt - ext)[HALO:]
    br_b = jnp.dot(pm.astype(BF16), wpool_ref[...], preferred_element_type=F32) * pscale_ref[...]

    cve = jnp.concatenate([cvh_ref[...].astype(F32) * keep, cv_ref[...].astype(F32)], axis=0)
    u = cve[:, 2 * CONV_WIDTH:] * cve[:, :CONV_WIDTH]
    cw = convw_ref[...]
    y = cw[0:1] * pltpu.roll(u, 2, 0) + cw[1:2] * pltpu.roll(u, 1, 0) + cw[2:3] * u
    br_c = (cve[:, CONV_WIDTH:2 * CONV_WIDTH] * y)[HALO:]

    d = x_ref.shape[1]

    def gated(j, branch, lo, hi):
        b = jnp.dot(branch, wbr_ref[0, lo:hi, :], preferred_element_type=F32)
        return b + b * jnp.tanh(gl_ref[:, j * d:(j + 1) * d].astype(F32))

    o1 = A_WIDTH + POOL_WIDTH
    merged = gated(0, o_ref[...], 0, A_WIDTH)
    merged += gated(1, br_b.astype(BF16), A_WIDTH, o1)
    merged += gated(2, br_c.astype(BF16), o1, o1 + CONV_WIDTH)
    yo = jnp.dot(merged.astype(BF16), wout_ref[0], preferred_element_type=F32)
    out_ref[...] = x_ref[...] + gate_ref[0, 0] * _rms(yo, g_ref[...])


def _merge(o, pu, cv, gl, x2, gate, wpool, pscale, convw, wbr_bf, wout_bf, l, g, seq):
    t, d = x2.shape
    nt = t // TM
    tiles_per_seq = seq // TM
    hb = TM // HALO

    def halo_map(i):
        return (jnp.maximum(i * hb - 1, 0), 0)

    return pl.pallas_call(
        functools.partial(_merge_kernel, tiles_per_seq),
        grid=(nt,),
        in_specs=[
            pl.BlockSpec((TM, A_WIDTH), lambda i: (i, 0)),
            pl.BlockSpec((TM, POOL_WIDTH), lambda i: (i, 0)),
            pl.BlockSpec((HALO, POOL_WIDTH), halo_map),
            pl.BlockSpec((TM, 3 * CONV_WIDTH), lambda i: (i, 0)),
            pl.BlockSpec((HALO, 3 * CONV_WIDTH), halo_map),
            pl.BlockSpec((TM, 3 * d), lambda i: (i, 0)),
            pl.BlockSpec((TM, d), lambda i: (i, 0)),
            pl.BlockSpec((1, 1, 1, d), lambda i: (0, i // tiles_per_seq, 0, 0)),
            _const_spec(wpool.shape),
            _const_spec(pscale.shape),
            _const_spec(convw.shape),
            _layer_spec(wbr_bf, l),
            _layer_spec(wout_bf, l),
            _const_spec(g.shape),
        ],
        out_specs=pl.BlockSpec((TM, d), lambda i: (i, 0)),
        out_shape=jax.ShapeDtypeStruct((t, d), F32),
        compiler_params=pltpu.CompilerParams(
            dimension_semantics=("arbitrary",), vmem_limit_bytes=VMEM_LIMIT),
        name="merge",
    )(o, pu, pu, cv, cv, gl, x2, gate, wpool, pscale, convw, wbr_bf, wout_bf, g)


def _mlp_kernel(x_ref, mod_ref, gpre_ref, w1_ref, w2_ref, gpost_ref, out_ref):
    x = x_ref[...]
    shift = mod_ref[0, 0]
    scale = mod_ref[1, 0]
    gate = mod_ref[2, 0]
    hb = (_rms(x, gpre_ref[...]) * (1.0 + scale) + shift).astype(BF16)
    dff = w1_ref.shape[2]
    acc = jnp.zeros(x.shape, F32)
    for c in range(dff // FF_CHUNK):
        up = jnp.dot(hb, w1_ref[0, :, c * FF_CHUNK:(c + 1) * FF_CHUNK], preferred_element_type=F32)
        act = jnp.square(jnp.maximum(up, 0.0)).astype(BF16)
        acc += jnp.dot(act, w2_ref[0, c * FF_CHUNK:(c + 1) * FF_CHUNK, :], preferred_element_type=F32)
    out_ref[...] = x + gate * _rms(acc, gpost_ref[...])


def _mlp(x2, mod, gpre, w1_bf, w2_bf, l, gpost, seq):
    t, d = x2.shape
    nt = t // TM
    tiles_per_seq = seq // TM
    return pl.pallas_call(
        _mlp_kernel,
        grid=(nt,),
        in_specs=[
            pl.BlockSpec((TM, d), lambda i: (i, 0)),
            pl.BlockSpec((3, 1, 1, d), lambda i: (0, i // tiles_per_seq, 0, 0)),
            _const_spec(gpre.shape),
            _layer_spec(w1_bf, l),
            _layer_spec(w2_bf, l),
            _const_spec(gpost.shape),
        ],
        out_specs=pl.BlockSpec((TM, d), lambda i: (i, 0)),
        out_shape=jax.ShapeDtypeStruct((t, d), F32),
        compiler_params=pltpu.CompilerParams(
            dimension_semantics=("arbitrary",), vmem_limit_bytes=VMEM_LIMIT),
        name="mlp",
    )(x2, mod, gpre, w1_bf, w2_bf, gpost)


def _block_diag(w):
    g, c, dd = w.shape
    eye = jnp.eye(g, dtype=w.dtype)
    return (eye[:, None, :, None] * w[:, :, None, :]).reshape(g * c, g * dd)


def kernel(x, c, w_ada, b_ada, g_mix_pre, g_mix_post, g_ff_pre, g_ff_post, w_in, b_f, w_pool, pool_scale,
           conv_w, w_branch, w_out, w_ff1, w_ff2):
    nb, seq, d = x.shape
    depth = w_ada.shape[0]
    assert seq % TM == 0 and seq % TQ == 0 and seq % TK == 0 and d % LANES == 0 and d % PREP_ROWS == 0
    t = nb * seq
    mod = _ada(c, w_ada, b_ada).reshape(depth, 6, nb, 1, d)
    x2 = x.reshape(t, d)
    w_in_bf = w_in.astype(BF16)
    w_branch_bf = (w_branch * 0.5).astype(BF16)
    w_out_bf = w_out.astype(BF16)
    w_ff1_bf = w_ff1.astype(BF16)
    w_ff2_bf = w_ff2.astype(BF16)
    spread = np.zeros((LANES, LANES), np.float32)
    head_of_lane = np.zeros((LANES,), np.int32)
    lane_used = np.zeros((LANES,), bool)
    for hd in range(HEADS):
        spread[hd, _aug_base(hd):_aug_base(hd) + AUG] = 1.0
        head_of_lane[_aug_base(hd):_aug_base(hd) + AUG] = hd
        lane_used[_aug_base(hd):_aug_base(hd) + AUG] = True
    spread = jnp.asarray(spread, BF16)
    for l in range(depth):
        bf = jnp.where(lane_used, b_f[l][head_of_lane], 0.0).reshape(1, LANES)
        row = lambda a: a[l].reshape(1, -1)
        q8, k8, v8, pu, cv, gl = _inproj(x2, mod[l, 0:2], row(g_mix_pre), w_in_bf, l, spread, bf, seq)
        o = _attention(q8, k8, v8, seq)
        x2 = _merge(o, pu, cv, gl, x2, mod[l, 2:3], _block_diag(w_pool[l]).astype(BF16), row(pool_scale),
                    conv_w[l], w_branch_bf, w_out_bf, l, row(g_mix_post), seq)
        x2 = _mlp(x2, mod[l, 3:6], row(g_ff_pre), w_ff1_bf, w_ff2_bf, l, row(g_ff_post), seq)
    return x2.reshape(nb, seq, d)
```

```python
import functools

import jax
import jax.numpy as jnp
import numpy as np
from jax import lax
from jax.experimental import pallas as pl
from jax.experimental.pallas import tpu as pltpu

F32 = jnp.float32
BF16 = jnp.bfloat16

LANES = 128
HEAD_DIM = 64
HEADS = 8
PAIRS = HEADS // 2
A_WIDTH = HEADS * HEAD_DIM
POOL_WINDOWS = (2, 4, 8, 16)
POOL_WIDTH = 256
CONV_WIDTH = 256
HALO = 16
RMS_EPS = 1e-6
NEG_INF = -1e30
LOG2E = 1.4426950408889634
AUG = 6
VMEM_LIMIT = 56 * 1024 * 1024

TM = 512
TMM = 1024
TML = 1024
TQ = 512
TK = 512
FF_CHUNK = 1024
SUB = 256
PREP_ROWS = 128


def _const_spec(shape):
    n = len(shape)
    return pl.BlockSpec(shape, lambda *_: (0,) * n, pipeline_mode=pl.Buffered(1))


def _layer_spec(w, l):
    n = w.ndim - 1
    return pl.BlockSpec((1,) + w.shape[1:], lambda *_: (l,) + (0,) * n, pipeline_mode=pl.Buffered(1))


def _rms(x, g):
    ms = jnp.mean(x * x, axis=-1, keepdims=True)
    return x * lax.rsqrt(ms + RMS_EPS) * g


def _nt_dot(a, b):
    return lax.dot_general(a, b, (((1,), (1,)), ((), ())), preferred_element_type=F32)


def _ada_kernel(c_ref, w_ref, b_ref, o_ref):
    c = c_ref[...]
    sc = c * (1.0 / (1.0 + jnp.exp(-c)))
    o_ref[0, 0] = jnp.dot(sc, w_ref[0], precision=lax.Precision.HIGHEST,
                          preferred_element_type=F32) + b_ref[0, 0]


def _ada(c, w_ada, b_ada):
    depth, d, d6 = w_ada.shape
    nb = c.shape[0]
    n = d6 // d
    return pl.pallas_call(
        _ada_kernel,
        grid=(depth, n),
        in_specs=[
            pl.BlockSpec((nb, d), lambda l, j: (0, 0)),
            pl.BlockSpec((1, d, d), lambda l, j: (l, 0, j)),
            pl.BlockSpec((1, 1, 1, d), lambda l, j: (l, j, 0, 0)),
        ],
        out_specs=pl.BlockSpec((1, 1, nb, d), lambda l, j: (l, j, 0, 0)),
        out_shape=jax.ShapeDtypeStruct((depth, n, nb, d), F32),
        compiler_params=pltpu.CompilerParams(
            dimension_semantics=("arbitrary", "arbitrary"), vmem_limit_bytes=VMEM_LIMIT),
        name="ada",
    )(c, w_ada, b_ada.reshape(depth, n, 1, d))


def _aug_base(h):
    return (HEAD_DIM if h % 2 == 0 else 0) + AUG * (h // 2)


def _inproj_kernel(tiles_per_seq, x_ref, mod_ref, g_ref, w_ref, spread_ref, bf_ref,
                   q_ref, k_ref, v_ref, pu_ref, cv_ref, gl_ref, carry_ref, wf_ref, wrest_ref):
    i = pl.program_id(0)
    d = x_ref.shape[1]
    fo = 3 * A_WIDTH
    n_plain = POOL_WIDTH + 3 * CONV_WIDTH

    @pl.when(i == 0)
    def _():
        wf_ref[...] = jnp.dot(w_ref[0, :, fo:fo + LANES], spread_ref[...],
                              preferred_element_type=F32).astype(BF16)
        col = lax.broadcasted_iota(jnp.int32, (1, wrest_ref.shape[1]), 1)
        col_scale = jnp.where(col < n_plain, 1.0, 0.5)
        for r in range(0, d, PREP_ROWS):
            blk = w_ref[0, r:r + PREP_ROWS, fo:].astype(F32)
            wrest_ref[r:r + PREP_ROWS, :] = (blk[:, HEADS:] * col_scale).astype(BF16)

    x = x_ref[...]
    shift = mod_ref[0, 0]
    scale = mod_ref[1, 0]
    h = _rms(x, g_ref[...]) * (1.0 + scale) + shift
    hb = h.astype(BF16)
    tm = x.shape[0]

    zf = jnp.dot(hb, wf_ref[...], preferred_element_type=F32) + bf_ref[...]
    lf = jnp.minimum(zf, 0.0) - jnp.log(1.0 + jnp.exp(-jnp.abs(zf)))
    row = lax.broadcasted_iota(jnp.int32, lf.shape, 0)
    step = 1
    while step < tm:
        lf = lf + jnp.where(row >= step, pltpu.roll(lf, step, 0), 0.0)
        step *= 2

    @pl.when(i % tiles_per_seq == 0)
    def _():
        carry_ref[...] = jnp.zeros_like(carry_ref)

    fc = lf + carry_ref[0:1, :]
    carry_ref[...] = jnp.broadcast_to(fc[tm - 1:tm, :], carry_ref.shape)

    f2 = fc * LOG2E
    hi = f2.astype(BF16).astype(F32)
    rem = f2 - hi
    mid = rem.astype(BF16).astype(F32)
    lo = rem - mid
    lane = lax.broadcasted_iota(jnp.int32, f2.shape, 1)
    j = (lane % HEAD_DIM) % AUG
    part = jnp.where(j % 3 == 0, hi, jnp.where(j % 3 == 1, mid, lo))
    aug_q = jnp.where(j < 3, 1.0, part)
    aug_k = jnp.where(j < 3, -part, 1.0)

    def proj(lo_, hi_):
        return jnp.dot(hb, w_ref[0, :, lo_:hi_], preferred_element_type=F32)

    zq = proj(0, A_WIDTH) * (HEAD_DIM ** -0.5 * LOG2E)
    zk = proj(A_WIDTH, 2 * A_WIDTH)
    zv = proj(2 * A_WIDTH, 3 * A_WIDTH)
    for hd in range(HEADS):
        p = hd // 2
        in_head = (lane >= (hd % 2) * HEAD_DIM) & (lane < (hd % 2 + 1) * HEAD_DIM)
        base = _aug_base(hd)
        in_aug = (lane >= base) & (lane < base + AUG)
        pair = slice(p * LANES, (p + 1) * LANES)
        q_ref[hd] = jnp.where(in_head, zq[:, pair], jnp.where(in_aug, aug_q, 0.0)).astype(BF16)
        k_ref[hd] = jnp.where(in_head, zk[:, pair], jnp.where(in_aug, aug_k, 0.0)).astype(BF16)
        ones_lane = HEAD_DIM if hd % 2 == 0 else 0
        v_ref[hd] = jnp.where(in_head, zv[:, pair], jnp.where(lane == ones_lane, 1.0, 0.0)).astype(BF16)

    def rest(lo_, hi_):
        return jnp.dot(hb, wrest_ref[:, lo_:hi_], preferred_element_type=F32)

    pu_ref[...] = rest(0, POOL_WIDTH).astype(BF16)
    cv_ref[...] = rest(POOL_WIDTH, n_plain).astype(BF16)
    for jj in range(3):
        gl_ref[:, jj * d:(jj + 1) * d] = rest(n_plain + jj * d, n_plain + (jj + 1) * d).astype(BF16)


def _inproj(x2, mod, g, w_in_bf, l, spread, bf, seq):
    t, d = x2.shape
    nt = t // TM
    tiles_per_seq = seq // TM
    n_rest = w_in_bf.shape[2] - 3 * A_WIDTH - HEADS
    out_shape = (
        jax.ShapeDtypeStruct((HEADS, t, LANES), BF16),
        jax.ShapeDtypeStruct((HEADS, t, LANES), BF16),
        jax.ShapeDtypeStruct((HEADS, t, LANES), BF16),
        jax.ShapeDtypeStruct((t, POOL_WIDTH), BF16),
        jax.ShapeDtypeStruct((t, 3 * CONV_WIDTH), BF16),
        jax.ShapeDtypeStruct((t, 3 * d), BF16),
    )
    head_spec = pl.BlockSpec((HEADS, TM, LANES), lambda i: (0, i, 0))
    out_specs = (
        head_spec, head_spec, head_spec,
        pl.BlockSpec((TM, POOL_WIDTH), lambda i: (i, 0)),
        pl.BlockSpec((TM, 3 * CONV_WIDTH), lambda i: (i, 0)),
        pl.BlockSpec((TM, 3 * d), lambda i: (i, 0)),
    )
    return pl.pallas_call(
        functools.partial(_inproj_kernel, tiles_per_seq),
        grid=(nt,),
        in_specs=[
            pl.BlockSpec((TM, d), lambda i: (i, 0)),
            pl.BlockSpec((2, 1, 1, d), lambda i: (0, i // tiles_per_seq, 0, 0)),
            _const_spec((1, d)),
            _layer_spec(w_in_bf, l),
            _const_spec(spread.shape),
            _const_spec((1, LANES)),
        ],
        out_specs=out_specs,
        out_shape=out_shape,
        scratch_shapes=[
            pltpu.VMEM((8, LANES), F32),
            pltpu.VMEM((d, LANES), BF16),
            pltpu.VMEM((d, n_rest), BF16),
        ],
        compiler_params=pltpu.CompilerParams(
            dimension_semantics=("arbitrary",), vmem_limit_bytes=VMEM_LIMIT),
        name="inproj",
    )(x2, mod, g, w_in_bf, spread, bf)


def _attn_kernel(q_ref, k_ref, v_ref, o_ref):
    seq = q_ref.shape[1]
    nq = seq // TQ
    half = TK // 2
    steps = [(qi, kt) for qi in range(nq) for kt in range(qi + 1)]
    mask0 = (lax.broadcasted_iota(jnp.int32, (half, TQ), 0)
             <= lax.broadcasted_iota(jnp.int32, (half, TQ), 1))
    mask1 = (lax.broadcasted_iota(jnp.int32, (half, half), 0)
             <= lax.broadcasted_iota(jnp.int32, (half, half), 1))
    vts = [v_ref[hh].astype(F32).T.astype(BF16) for hh in range(2)]

    def logits(step, hh):
        qi, kt = step
        q = q_ref[hh, qi * TQ:(qi + 1) * TQ, :]
        if kt < qi:
            st = _nt_dot(k_ref[hh, kt * TK:(kt + 1) * TK, :], q)
            return (st,), jnp.max(st, axis=0, keepdims=True)
        st0 = jnp.where(mask0, _nt_dot(k_ref[hh, kt * TK:kt * TK + half, :], q), NEG_INF)
        st1 = jnp.where(mask1, _nt_dot(k_ref[hh, kt * TK + half:(kt + 1) * TK, :], q[half:]), NEG_INF)
        c0 = jnp.max(st0, axis=0, keepdims=True)
        c1 = jnp.max(st1, axis=0, keepdims=True)
        return (st0, st1), jnp.concatenate([c0[:, :half], jnp.maximum(c0[:, half:], c1)], axis=1)

    def update(step, hh, sts, cmax, m, acc):
        qi, kt = step
        vt = vts[hh]
        m_new = cmax if m is None else jnp.maximum(m, cmax)
        if kt < qi:
            pt = jnp.exp2(sts[0] - m_new).astype(BF16)
            pv = jnp.dot(vt[:, kt * TK:(kt + 1) * TK], pt, preferred_element_type=F32)
        else:
            pt0 = jnp.exp2(sts[0] - m_new).astype(BF16)
            pt1 = jnp.exp2(sts[1] - m_new[:, half:]).astype(BF16)
            pv = jnp.dot(vt[:, kt * TK:kt * TK + half], pt0, preferred_element_type=F32)
            pv1 = jnp.dot(vt[:, kt * TK + half:(kt + 1) * TK], pt1, preferred_element_type=F32)
            pv = jnp.concatenate([pv[:, :half], pv[:, half:] + pv1], axis=1)
        if m is not None:
            pv = jnp.exp2(m - m_new) * acc + pv
        return m_new, pv

    cur = [logits(steps[0], hh) for hh in range(2)]
    state = [(None, None), (None, None)]
    for s, step in enumerate(steps):
        qi, kt = step
        nxt = [None, None]
        for hh in range(2):
            if s + 1 < len(steps):
                nxt[hh] = logits(steps[s + 1], hh)
            state[hh] = update(step, hh, *cur[hh], *state[hh])
        cur = nxt
        if kt == qi:
            a0, a1 = state[0][1], state[1][1]
            out = jnp.concatenate([a0[0:HEAD_DIM] / a0[HEAD_DIM:HEAD_DIM + 1],
                                   a1[HEAD_DIM:] / a1[0:1]], axis=0)
            o_ref[qi * TQ:(qi + 1) * TQ, :] = out.T.astype(BF16)
            state = [(None, None), (None, None)]


def _attention(q8, k8, v8, seq):
    _, t, _ = q8.shape
    nb = t // seq
    pair_spec = pl.BlockSpec((2, seq, LANES), lambda b, p: (p, b, 0))
    return pl.pallas_call(
        _attn_kernel,
        grid=(nb, PAIRS),
        in_specs=[pair_spec, pair_spec, pair_spec],
        out_specs=pl.BlockSpec((seq, LANES), lambda b, p: (b, p)),
        out_shape=jax.ShapeDtypeStruct((t, A_WIDTH), BF16),
        compiler_params=pltpu.CompilerParams(
            dimension_semantics=("arbitrary", "arbitrary"), vmem_limit_bytes=VMEM_LIMIT),
        name="attn",
    )(q8, k8, v8)


def _merge_kernel(tiles_per_seq, o_ref, pu_ref, puh_ref, cv_ref, cvh_ref, gl_ref, x_ref, gate_ref,
                  wpool_ref, pscale_ref, convw_ref, wbr_ref, wout_ref, g_ref, out_ref):
    i = pl.program_id(0)
    tile_in_seq = i % tiles_per_seq
    tm, d = x_ref.shape
    n_sub = tm // SUB
    lane = lax.broadcasted_iota(jnp.int32, (SUB + HALO, POOL_WIDTH), 1)
    row = lax.broadcasted_iota(jnp.int32, (SUB + HALO, POOL_WIDTH), 0)
    gd = POOL_WIDTH // len(POOL_WINDOWS)
    wsz = jnp.where(lane < gd, 2.0, jnp.where(lane < 2 * gd, 4.0, jnp.where(lane < 3 * gd, 8.0, 16.0)))
    cw = convw_ref[...]
    o1 = A_WIDTH + POOL_WIDTH

    def with_history(ref, halo_ref, j):
        if j == 0:
            head = jnp.where(tile_in_seq != 0, halo_ref[...].astype(F32), 0.0)
        else:
            head = ref[j * SUB - HALO:j * SUB, :].astype(F32)
        return jnp.concatenate([head, ref[j * SUB:(j + 1) * SUB, :].astype(F32)], axis=0)

    def mix(j):
        rows = slice(j * SUB, (j + 1) * SUB)
        ext = with_history(pu_ref, puh_ref, j)
        a2 = ext + pltpu.roll(ext, 1, 0)
        a4 = a2 + pltpu.roll(a2, 2, 0)
        a8 = a4 + pltpu.roll(a4, 4, 0)
        a16 = a8 + pltpu.roll(a8, 8, 0)
        win = jnp.where(lane < gd, a2, jnp.where(lane < 2 * gd, a4, jnp.where(lane < 3 * gd, a8, a16)))
        frames = (tile_in_seq * tm + j * SUB + row - (HALO - 1)).astype(F32)
        cnt = jnp.maximum(jnp.minimum(frames, wsz), 1.0)
        pm = (win / cnt - ext)[HALO:]
        br_b = jnp.dot(pm.astype(BF16), wpool_ref[...], preferred_element_type=F32) * pscale_ref[...]

        cve = with_history(cv_ref, cvh_ref, j)
        u = cve[:, 2 * CONV_WIDTH:] * cve[:, :CONV_WIDTH]
        y = cw[0:1] * pltpu.roll(u, 2, 0) + cw[1:2] * pltpu.roll(u, 1, 0) + cw[2:3] * u
        br_c = (cve[:, CONV_WIDTH:2 * CONV_WIDTH] * y)[HALO:]

        def gated(n, branch, lo, hi):
            b = jnp.dot(branch, wbr_ref[0, lo:hi, :], preferred_element_type=F32)
            return b + b * jnp.tanh(gl_ref[rows, n * d:(n + 1) * d].astype(F32))

        merged = gated(0, o_ref[rows, :], 0, A_WIDTH)
        merged += gated(1, br_b.astype(BF16), A_WIDTH, o1)
        merged += gated(2, br_c.astype(BF16), o1, o1 + CONV_WIDTH)
        return merged.astype(BF16)

    def project(j, merged):
        rows = slice(j * SUB, (j + 1) * SUB)
        yo = jnp.dot(merged, wout_ref[0], preferred_element_type=F32)
        out_ref[rows, :] = x_ref[rows, :] + gate_ref[0, 0] * _rms(yo, g_ref[...])

    cur = mix(0)
    for j in range(n_sub):
        nxt = mix(j + 1) if j + 1 < n_sub else None
        project(j, cur)
        cur = nxt


def _merge(o, pu, cv, gl, x2, gate, wpool, pscale, convw, wbr_bf, wout_bf, l, g, seq):
    t, d = x2.shape
    nt = t // TMM
    tiles_per_seq = seq // TMM
    hb = TMM // HALO

    def halo_map(i):
        return (jnp.maximum(i * hb - 1, 0), 0)

    return pl.pallas_call(
        functools.partial(_merge_kernel, tiles_per_seq),
        grid=(nt,),
        in_specs=[
            pl.BlockSpec((TMM, A_WIDTH), lambda i: (i, 0)),
            pl.BlockSpec((TMM, POOL_WIDTH), lambda i: (i, 0)),
            pl.BlockSpec((HALO, POOL_WIDTH), halo_map),
            pl.BlockSpec((TMM, 3 * CONV_WIDTH), lambda i: (i, 0)),
            pl.BlockSpec((HALO, 3 * CONV_WIDTH), halo_map),
            pl.BlockSpec((TMM, 3 * d), lambda i: (i, 0)),
            pl.BlockSpec((TMM, d), lambda i: (i, 0)),
            pl.BlockSpec((1, 1, 1, d), lambda i: (0, i // tiles_per_seq, 0, 0)),
            _const_spec(wpool.shape),
            _const_spec(pscale.shape),
            _const_spec(convw.shape),
            _layer_spec(wbr_bf, l),
            _layer_spec(wout_bf, l),
            _const_spec(g.shape),
        ],
        out_specs=pl.BlockSpec((TMM, d), lambda i: (i, 0)),
        out_shape=jax.ShapeDtypeStruct((t, d), F32),
        compiler_params=pltpu.CompilerParams(
            dimension_semantics=("arbitrary",), vmem_limit_bytes=VMEM_LIMIT),
        name="merge",
    )(o, pu, pu, cv, cv, gl, x2, gate, wpool, pscale, convw, wbr_bf, wout_bf, g)


def _mlp_kernel(x_ref, mod_ref, gpre_ref, w1_ref, w2_ref, gpost_ref, out_ref):
    x = x_ref[...]
    shift = mod_ref[0, 0]
    scale = mod_ref[1, 0]
    gate = mod_ref[2, 0]
    hb = (_rms(x, gpre_ref[...]) * (1.0 + scale) + shift).astype(BF16)
    dff = w1_ref.shape[2]
    acc = jnp.zeros(x.shape, F32)
    for c in range(dff // FF_CHUNK):
        up = jnp.dot(hb, w1_ref[0, :, c * FF_CHUNK:(c + 1) * FF_CHUNK], preferred_element_type=F32)
        act = jnp.square(jnp.maximum(up, 0.0)).astype(BF16)
        acc += jnp.dot(act, w2_ref[0, c * FF_CHUNK:(c + 1) * FF_CHUNK, :], preferred_element_type=F32)
    out_ref[...] = x + gate * _rms(acc, gpost_ref[...])


def _mlp(x2, mod, gpre, w1_bf, w2_bf, l, gpost, seq):
    t, d = x2.shape
    nt = t // TML
    tiles_per_seq = seq // TML
    return pl.pallas_call(
        _mlp_kernel,
        grid=(nt,),
        in_specs=[
            pl.BlockSpec((TML, d), lambda i: (i, 0)),
            pl.BlockSpec((3, 1, 1, d), lambda i: (0, i // tiles_per_seq, 0, 0)),
            _const_spec(gpre.shape),
            _layer_spec(w1_bf, l),
            _layer_spec(w2_bf, l),
            _const_spec(gpost.shape),
        ],
        out_specs=pl.BlockSpec((TML, d), lambda i: (i, 0)),
        out_shape=jax.ShapeDtypeStruct((t, d), F32),
        compiler_params=pltpu.CompilerParams(
            dimension_semantics=("arbitrary",), vmem_limit_bytes=VMEM_LIMIT),
        name="mlp",
    )(x2, mod, gpre, w1_bf, w2_bf, gpost)


def _block_diag(w):
    g, c, dd = w.shape
    eye = jnp.eye(g, dtype=w.dtype)
    return (eye[:, None, :, None] * w[:, :, None, :]).reshape(g * c, g * dd)


def kernel(x, c, w_ada, b_ada, g_mix_pre, g_mix_post, g_ff_pre, g_ff_post, w_in, b_f, w_pool, pool_scale,
           conv_w, w_branch, w_out, w_ff1, w_ff2):
    nb, seq, d = x.shape
    depth = w_ada.shape[0]
    assert all(seq % tile == 0 for tile in (TM, TMM, TML, TQ, TK)) and TMM % SUB == 0
    assert d % LANES == 0 and d % PREP_ROWS == 0
    t = nb * seq
    mod = _ada(c, w_ada, b_ada).reshape(depth, 6, nb, 1, d)
    x2 = x.reshape(t, d)
    w_in_bf = w_in.astype(BF16)
    w_branch_bf = (w_branch * 0.5).astype(BF16)
    w_out_bf = w_out.astype(BF16)
    w_ff1_bf = w_ff1.astype(BF16)
    w_ff2_bf = w_ff2.astype(BF16)
    spread = np.zeros((LANES, LANES), np.float32)
    head_of_lane = np.zeros((LANES,), np.int32)
    lane_used = np.zeros((LANES,), bool)
    for hd in range(HEADS):
        spread[hd, _aug_base(hd):_aug_base(hd) + AUG] = 1.0
        head_of_lane[_aug_base(hd):_aug_base(hd) + AUG] = hd
        lane_used[_aug_base(hd):_aug_base(hd) + AUG] = True
    spread = jnp.asarray(spread, BF16)
    for l in range(depth):
        bf = jnp.where(lane_used, b_f[l][head_of_lane], 0.0).reshape(1, LANES)
        row = lambda a: a[l].reshape(1, -1)
        q8, k8, v8, pu, cv, gl = _inproj(x2, mod[l, 0:2], row(g_mix_pre), w_in_bf, l, spread, bf, seq)
        o = _attention(q8, k8, v8, seq)
        x2 = _merge(o, pu, cv, gl, x2, mod[l, 2:3], _block_diag(w_pool[l]).astype(BF16), row(pool_scale),
                    conv_w[l], w_branch_bf, w_out_bf, l, row(g_mix_post), seq)
        x2 = _mlp(x2, mod[l, 3:6], row(g_ff_pre), w_ff1_bf, w_ff2_bf, l, row(g_ff_post), seq)
    return x2.reshape(nb, seq, d)
```

```python
import functools

import jax
import jax.numpy as jnp
import numpy as np
from jax import lax
from jax.experimental import pallas as pl
from jax.experimental.pallas import tpu as pltpu

F32 = jnp.float32
BF16 = jnp.bfloat16

LANES = 128
HEAD_DIM = 64
HEADS = 8
PAIRS = HEADS // 2
A_WIDTH = HEADS * HEAD_DIM
POOL_WINDOWS = (2, 4, 8, 16)
POOL_WIDTH = 256
CONV_WIDTH = 256
HALO = 16
RMS_EPS = 1e-6
NEG_INF = -1e30
LOG2E = 1.4426950408889634
AUG = 6
PV_ROWS = 80
ADA_CHUNKS = 8
VMEM_LIMIT = 56 * 1024 * 1024

TM = 512
TMM = 1024
TML = 1024
TQ = 512
TK = 512
FF_CHUNK = 1024
SUB = 256
PREP_ROWS = 128


def _const_spec(shape):
    n = len(shape)
    return pl.BlockSpec(shape, lambda *_: (0,) * n, pipeline_mode=pl.Buffered(1))


def _rms(x, g):
    ms = jnp.mean(x * x, axis=-1, keepdims=True)
    return x * lax.rsqrt(ms + RMS_EPS) * g


def _nt_dot(a, b):
    return lax.dot_general(a, b, (((1,), (1,)), ((), ())), preferred_element_type=F32)


def _ada_kernel(c_ref, w_ref, b_ref, win_ref, o_ref, win_bf_ref):
    c = c_ref[...]
    sc = c * (1.0 / (1.0 + jnp.exp(-c)))
    o_ref[0] = jnp.dot(sc, w_ref[0], precision=lax.Precision.HIGHEST,
                       preferred_element_type=F32) + b_ref[0]
    win_bf_ref[...] = win_ref[0].astype(BF16)


def _ada(c, w_ada, b_ada, w_in):
    depth, d, d6 = w_ada.shape
    nb = c.shape[0]
    n = ADA_CHUNKS
    cols = d6 // n
    rows = w_in.shape[1] // (depth * n)
    ncol_in = w_in.shape[2]
    return pl.pallas_call(
        _ada_kernel,
        grid=(depth, n),
        in_specs=[
            pl.BlockSpec((nb, d), lambda l, j: (0, 0)),
            pl.BlockSpec((1, d, cols), lambda l, j: (l, 0, j)),
            pl.BlockSpec((1, 1, cols), lambda l, j: (l, 0, j)),
            pl.BlockSpec((1, rows, ncol_in), lambda l, j: (0, l * n + j, 0)),
        ],
        out_specs=(
            pl.BlockSpec((1, nb, cols), lambda l, j: (l, 0, j)),
            pl.BlockSpec((rows, ncol_in), lambda l, j: (l * n + j, 0)),
        ),
        out_shape=(
            jax.ShapeDtypeStruct((depth, nb, d6), F32),
            jax.ShapeDtypeStruct(w_in.shape[1:], BF16),
        ),
        compiler_params=pltpu.CompilerParams(
            dimension_semantics=("arbitrary", "arbitrary"), vmem_limit_bytes=VMEM_LIMIT),
        name="ada",
    )(c, w_ada, b_ada.reshape(depth, 1, d6), w_in)


def _aug_base(h):
    return (HEAD_DIM if h % 2 == 0 else 0) + AUG * (h // 2)


def _inproj_kernel(tiles_per_seq, cast_scales, x_ref, mod_ref, g_ref, w_ref, spread_ref, bf_ref, *refs):
    n_cast = len(cast_scales)
    cast_in = refs[:n_cast]
    q_ref, k_ref, v_ref, pu_ref, cv_ref, gl_ref = refs[n_cast:n_cast + 6]
    cast_out = refs[n_cast + 6:2 * n_cast + 6]
    carry_ref, wf_ref, wrest_ref = refs[2 * n_cast + 6:]
    i = pl.program_id(0)
    d = x_ref.shape[1]

    for src, dst, s in zip(cast_in, cast_out, cast_scales):
        dst[...] = (src[0] if s == 1.0 else src[0] * s).astype(BF16)

    fo = 3 * A_WIDTH
    n_plain = POOL_WIDTH + 3 * CONV_WIDTH

    @pl.when(i == 0)
    def _():
        wf_ref[...] = jnp.dot(w_ref[:, fo:fo + LANES], spread_ref[...],
                              preferred_element_type=F32).astype(BF16)
        col = lax.broadcasted_iota(jnp.int32, (1, wrest_ref.shape[1]), 1)
        col_scale = jnp.where(col < n_plain, 1.0, 0.5)
        for r in range(0, d, PREP_ROWS):
            blk = w_ref[r:r + PREP_ROWS, fo:].astype(F32)
            wrest_ref[r:r + PREP_ROWS, :] = (blk[:, HEADS:] * col_scale).astype(BF16)

    x = x_ref[...]
    shift = mod_ref[0, 0]
    scale = mod_ref[0, 1]
    h = _rms(x, g_ref[...]) * (1.0 + scale) + shift
    hb = h.astype(BF16)
    tm = x.shape[0]

    zf = jnp.dot(hb, wf_ref[...], preferred_element_type=F32) + bf_ref[...]
    lf = jnp.minimum(zf, 0.0) - jnp.log(1.0 + jnp.exp(-jnp.abs(zf)))
    row = lax.broadcasted_iota(jnp.int32, lf.shape, 0)
    step = 1
    while step < tm:
        lf = lf + jnp.where(row >= step, pltpu.roll(lf, step, 0), 0.0)
        step *= 2

    @pl.when(i % tiles_per_seq == 0)
    def _():
        carry_ref[...] = jnp.zeros_like(carry_ref)

    fc = lf + carry_ref[0:1, :]
    carry_ref[...] = jnp.broadcast_to(fc[tm - 1:tm, :], carry_ref.shape)

    f2 = fc * LOG2E
    hi = f2.astype(BF16).astype(F32)
    rem = f2 - hi
    mid = rem.astype(BF16).astype(F32)
    lo = rem - mid
    lane = lax.broadcasted_iota(jnp.int32, f2.shape, 1)
    j = (lane % HEAD_DIM) % AUG
    part = jnp.where(j % 3 == 0, hi, jnp.where(j % 3 == 1, mid, lo))
    aug_q = jnp.where(j < 3, 1.0, part)
    aug_k = jnp.where(j < 3, -part, 1.0)

    def proj(lo_, hi_):
        return jnp.dot(hb, w_ref[:, lo_:hi_], preferred_element_type=F32)

    zq = proj(0, A_WIDTH) * (HEAD_DIM ** -0.5 * LOG2E)
    zk = proj(A_WIDTH, 2 * A_WIDTH)
    zv = proj(2 * A_WIDTH, 3 * A_WIDTH)
    for hd in range(HEADS):
        p = hd // 2
        in_head = (lane >= (hd % 2) * HEAD_DIM) & (lane < (hd % 2 + 1) * HEAD_DIM)
        base = _aug_base(hd)
        in_aug = (lane >= base) & (lane < base + AUG)
        pair = slice(p * LANES, (p + 1) * LANES)
        q_ref[hd] = jnp.where(in_head, zq[:, pair], jnp.where(in_aug, aug_q, 0.0)).astype(BF16)
        k_ref[hd] = jnp.where(in_head, zk[:, pair], jnp.where(in_aug, aug_k, 0.0)).astype(BF16)
        ones_lane = HEAD_DIM if hd % 2 == 0 else HEAD_DIM - 1
        v_ref[hd] = jnp.where(in_head, zv[:, pair], jnp.where(lane == ones_lane, 1.0, 0.0)).astype(BF16)

    def rest(lo_, hi_):
        return jnp.dot(hb, wrest_ref[:, lo_:hi_], preferred_element_type=F32)

    pu_ref[...] = rest(0, POOL_WIDTH).astype(BF16)
    cv_ref[...] = rest(POOL_WIDTH, n_plain).astype(BF16)
    for jj in range(3):
        gl_ref[:, jj * d:(jj + 1) * d] = rest(n_plain + jj * d, n_plain + (jj + 1) * d).astype(BF16)


def _inproj(x2, mod, g, w_in_bf, spread, bf, seq, casts):
    t, d = x2.shape
    nt = t // TM
    tiles_per_seq = seq // TM
    n_rest = w_in_bf.shape[1] - 3 * A_WIDTH - HEADS
    cast_in_specs, cast_out_specs, cast_shapes = [], [], []
    for w, l, _ in casts:
        _, k, n = w.shape
        assert k % nt == 0 and (k // nt) % 16 == 0
        cast_in_specs.append(pl.BlockSpec((1, k // nt, n), functools.partial(lambda l_, i: (l_, i, 0), l)))
        cast_out_specs.append(pl.BlockSpec((k // nt, n), lambda i: (i, 0)))
        cast_shapes.append(jax.ShapeDtypeStruct((k, n), BF16))
    out_shape = (
        jax.ShapeDtypeStruct((HEADS, t, LANES), BF16),
        jax.ShapeDtypeStruct((HEADS, t, LANES), BF16),
        jax.ShapeDtypeStruct((HEADS, t, LANES), BF16),
        jax.ShapeDtypeStruct((t, POOL_WIDTH), BF16),
        jax.ShapeDtypeStruct((t, 3 * CONV_WIDTH), BF16),
        jax.ShapeDtypeStruct((t, 3 * d), BF16),
    )
    head_spec = pl.BlockSpec((HEADS, TM, LANES), lambda i: (0, i, 0))
    out_specs = (
        head_spec, head_spec, head_spec,
        pl.BlockSpec((TM, POOL_WIDTH), lambda i: (i, 0)),
        pl.BlockSpec((TM, 3 * CONV_WIDTH), lambda i: (i, 0)),
        pl.BlockSpec((TM, 3 * d), lambda i: (i, 0)),
    )
    outs = pl.pallas_call(
        functools.partial(_inproj_kernel, tiles_per_seq, tuple(s for _, _, s in casts)),
        grid=(nt,),
        in_specs=[
            pl.BlockSpec((TM, d), lambda i: (i, 0)),
            pl.BlockSpec((1, 2, 1, d), lambda i: (i // tiles_per_seq, 0, 0, 0)),
            _const_spec((1, d)),
            _const_spec(w_in_bf.shape),
            _const_spec(spread.shape),
            _const_spec((1, LANES)),
        ] + cast_in_specs,
        out_specs=out_specs + tuple(cast_out_specs),
        out_shape=out_shape + tuple(cast_shapes),
        scratch_shapes=[
            pltpu.VMEM((8, LANES), F32),
            pltpu.VMEM((d, LANES), BF16),
            pltpu.VMEM((d, n_rest), BF16),
        ],
        compiler_params=pltpu.CompilerParams(
            dimension_semantics=("arbitrary",), vmem_limit_bytes=VMEM_LIMIT),
        name="inproj",
    )(x2, mod, g, w_in_bf, spread, bf, *[w for w, _, _ in casts])
    return outs[:6], outs[6:]


def _attn_kernel(q_ref, k_ref, v_ref, o_ref):
    seq = q_ref.shape[1]
    nq = seq // TQ
    half = TK // 2
    steps = [(qi, kt) for qi in range(nq) for kt in range(qi + 1)]
    mask0 = (lax.broadcasted_iota(jnp.int32, (half, TQ), 0)
             <= lax.broadcasted_iota(jnp.int32, (half, TQ), 1))
    mask1 = (lax.broadcasted_iota(jnp.int32, (half, half), 0)
             <= lax.broadcasted_iota(jnp.int32, (half, half), 1))
    vts = [v_ref[hh].astype(F32).T.astype(BF16)[hh * (LANES - PV_ROWS):hh * (LANES - PV_ROWS) + PV_ROWS]
           for hh in range(2)]

    def logits(step, hh):
        qi, kt = step
        q = q_ref[hh, qi * TQ:(qi + 1) * TQ, :]
        if kt < qi:
            st = _nt_dot(k_ref[hh, kt * TK:(kt + 1) * TK, :], q)
            return (st,), jnp.max(st, axis=0, keepdims=True)
        st0 = jnp.where(mask0, _nt_dot(k_ref[hh, kt * TK:kt * TK + half, :], q), NEG_INF)
        st1 = jnp.where(mask1, _nt_dot(k_ref[hh, kt * TK + half:(kt + 1) * TK, :], q[half:]), NEG_INF)
        c0 = jnp.max(st0, axis=0, keepdims=True)
        c1 = jnp.max(st1, axis=0, keepdims=True)
        return (st0, st1), jnp.concatenate([c0[:, :half], jnp.maximum(c0[:, half:], c1)], axis=1)

    def update(step, hh, sts, cmax, m, acc):
        qi, kt = step
        vt = vts[hh]
        m_new = cmax if m is None else jnp.maximum(m, cmax)
        if kt < qi:
            pt = jnp.exp2(sts[0] - m_new).astype(BF16)
            pv = jnp.dot(vt[:, kt * TK:(kt + 1) * TK], pt, preferred_element_type=F32)
        else:
            pt0 = jnp.exp2(sts[0] - m_new).astype(BF16)
            pt1 = jnp.exp2(sts[1] - m_new[:, half:]).astype(BF16)
            pv = jnp.dot(vt[:, kt * TK:kt * TK + half], pt0, preferred_element_type=F32)
            pv1 = jnp.dot(vt[:, kt * TK + half:(kt + 1) * TK], pt1, preferred_element_type=F32)
            pv = jnp.concatenate([pv[:, :half], pv[:, half:] + pv1], axis=1)
        if m is not None:
            pv = jnp.exp2(m - m_new) * acc + pv
        return m_new, pv

    cur = [logits(steps[0], hh) for hh in range(2)]
    state = [(None, None), (None, None)]
    for s, step in enumerate(steps):
        qi, kt = step
        nxt = [None, None]
        for hh in range(2):
            if s + 1 < len(steps):
                nxt[hh] = logits(steps[s + 1], hh)
            state[hh] = update(step, hh, *cur[hh], *state[hh])
        cur = nxt
        if kt == qi:
            a0, a1 = state[0][1], state[1][1]
            pad = PV_ROWS - HEAD_DIM
            out = jnp.concatenate([a0[0:HEAD_DIM] / a0[HEAD_DIM:HEAD_DIM + 1],
                                   a1[pad:] / a1[pad - 1:pad]], axis=0)
            o_ref[qi * TQ:(qi + 1) * TQ, :] = out.T.astype(BF16)
            state = [(None, None), (None, None)]


def _attention(q8, k8, v8, seq):
    _, t, _ = q8.shape
    nb = t // seq
    pair_spec = pl.BlockSpec((2, seq, LANES), lambda b, p: (p, b, 0))
    return pl.pallas_call(
        _attn_kernel,
        grid=(nb, PAIRS),
        in_specs=[pair_spec, pair_spec, pair_spec],
        out_specs=pl.BlockSpec((seq, LANES), lambda b, p: (b, p)),
        out_shape=jax.ShapeDtypeStruct((t, A_WIDTH), BF16),
        compiler_params=pltpu.CompilerParams(
            dimension_semantics=("arbitrary", "arbitrary"), vmem_limit_bytes=VMEM_LIMIT),
        name="attn",
    )(q8, k8, v8)


def _merge_kernel(tiles_per_seq, o_ref, pu_ref, puh_ref, cv_ref, cvh_ref, gl_ref, x_ref, gate_ref,
                  wpool_ref, pscale_ref, convw_ref, wbr_ref, wout_ref, g_ref, out_ref):
    i = pl.program_id(0)
    tile_in_seq = i % tiles_per_seq
    tm, d = x_ref.shape
    n_sub = tm // SUB
    lane = lax.broadcasted_iota(jnp.int32, (SUB + HALO, POOL_WIDTH), 1)
    row = lax.broadcasted_iota(jnp.int32, (SUB + HALO, POOL_WIDTH), 0)
    gd = POOL_WIDTH // len(POOL_WINDOWS)
    wsz = jnp.where(lane < gd, 2.0, jnp.where(lane < 2 * gd, 4.0, jnp.where(lane < 3 * gd, 8.0, 16.0)))
    cw = convw_ref[...]
    o1 = A_WIDTH + POOL_WIDTH

    def with_history(ref, halo_ref, j):
        if j == 0:
            head = jnp.where(tile_in_seq != 0, halo_ref[...].astype(F32), 0.0)
        else:
            head = ref[j * SUB - HALO:j * SUB, :].astype(F32)
        return jnp.concatenate([head, ref[j * SUB:(j + 1) * SUB, :].astype(F32)], axis=0)

    def mix(j):
        rows = slice(j * SUB, (j + 1) * SUB)
        ext = with_history(pu_ref, puh_ref, j)
        a2 = ext + pltpu.roll(ext, 1, 0)
        a4 = a2 + pltpu.roll(a2, 2, 0)
        a8 = a4 + pltpu.roll(a4, 4, 0)
        a16 = a8 + pltpu.roll(a8, 8, 0)
        win = jnp.where(lane < gd, a2, jnp.where(lane < 2 * gd, a4, jnp.where(lane < 3 * gd, a8, a16)))
        frames = (tile_in_seq * tm + j * SUB + row - (HALO - 1)).astype(F32)
        cnt = jnp.maximum(jnp.minimum(frames, wsz), 1.0)
        pm = (win / cnt - ext)[HALO:]
        br_b = jnp.dot(pm.astype(BF16), wpool_ref[...], preferred_element_type=F32) * pscale_ref[...]

        cve = with_history(cv_ref, cvh_ref, j)
        u = cve[:, 2 * CONV_WIDTH:] * cve[:, :CONV_WIDTH]
        y = cw[0:1] * pltpu.roll(u, 2, 0) + cw[1:2] * pltpu.roll(u, 1, 0) + cw[2:3] * u
        br_c = (cve[:, CONV_WIDTH:2 * CONV_WIDTH] * y)[HALO:]

        def gated(n, branch, lo, hi):
            b = jnp.dot(branch, wbr_ref[lo:hi, :], preferred_element_type=F32)
            return b + b * jnp.tanh(gl_ref[rows, n * d:(n + 1) * d].astype(F32))

        merged = gated(0, o_ref[rows, :], 0, A_WIDTH)
        merged += gated(1, br_b.astype(BF16), A_WIDTH, o1)
        merged += gated(2, br_c.astype(BF16), o1, o1 + CONV_WIDTH)
        return merged.astype(BF16)

    def project(j, merged):
        rows = slice(j * SUB, (j + 1) * SUB)
        yo = jnp.dot(merged, wout_ref[...], preferred_element_type=F32)
        out_ref[rows, :] = x_ref[rows, :] + gate_ref[0, 0] * _rms(yo, g_ref[...])

    cur = mix(0)
    for j in range(n_sub):
        nxt = mix(j + 1) if j + 1 < n_sub else None
        project(j, cur)
        cur = nxt


def _merge(o, pu, cv, gl, x2, mod, wpool, pscale, convw, wbr_bf, wout_bf, g, seq):
    t, d = x2.shape
    nt = t // TMM
    tiles_per_seq = seq // TMM
    hb = TMM // HALO

    def halo_map(i):
        return (jnp.maximum(i * hb - 1, 0), 0)

    return pl.pallas_call(
        functools.partial(_merge_kernel, tiles_per_seq),
        grid=(nt,),
        in_specs=[
            pl.BlockSpec((TMM, A_WIDTH), lambda i: (i, 0)),
            pl.BlockSpec((TMM, POOL_WIDTH), lambda i: (i, 0)),
            pl.BlockSpec((HALO, POOL_WIDTH), halo_map),
            pl.BlockSpec((TMM, 3 * CONV_WIDTH), lambda i: (i, 0)),
            pl.BlockSpec((HALO, 3 * CONV_WIDTH), halo_map),
            pl.BlockSpec((TMM, 3 * d), lambda i: (i, 0)),
            pl.BlockSpec((TMM, d), lambda i: (i, 0)),
            pl.BlockSpec((1, 1, 1, d), lambda i: (i // tiles_per_seq, 2, 0, 0)),
            _const_spec(wpool.shape),
            _const_spec(pscale.shape),
            _const_spec(convw.shape),
            _const_spec(wbr_bf.shape),
            _const_spec(wout_bf.shape),
            _const_spec(g.shape),
        ],
        out_specs=pl.BlockSpec((TMM, d), lambda i: (i, 0)),
        out_shape=jax.ShapeDtypeStruct((t, d), F32),
        compiler_params=pltpu.CompilerParams(
            dimension_semantics=("arbitrary",), vmem_limit_bytes=VMEM_LIMIT),
        name="merge",
    )(o, pu, pu, cv, cv, gl, x2, mod, wpool, pscale, convw, wbr_bf, wout_bf, g)


def _mlp_kernel(x_ref, mod_ref, gpre_ref, w1_ref, w2_ref, gpost_ref, out_ref):
    x = x_ref[...]
    shift = mod_ref[0, 0]
    scale = mod_ref[0, 1]
    gate = mod_ref[0, 2]
    hb = (_rms(x, gpre_ref[...]) * (1.0 + scale) + shift).astype(BF16)
    dff = w1_ref.shape[1]
    acc = jnp.zeros(x.shape, F32)
    for c in range(dff // FF_CHUNK):
        up = jnp.dot(hb, w1_ref[:, c * FF_CHUNK:(c + 1) * FF_CHUNK], preferred_element_type=F32)
        act = jnp.square(jnp.maximum(up, 0.0)).astype(BF16)
        acc += jnp.dot(act, w2_ref[c * FF_CHUNK:(c + 1) * FF_CHUNK, :], preferred_element_type=F32)
    out_ref[...] = x + gate * _rms(acc, gpost_ref[...])


def _mlp(x2, mod, gpre, w1_bf, w2_bf, gpost, seq):
    t, d = x2.shape
    nt = t // TML
    tiles_per_seq = seq // TML
    return pl.pallas_call(
        _mlp_kernel,
        grid=(nt,),
        in_specs=[
            pl.BlockSpec((TML, d), lambda i: (i, 0)),
            pl.BlockSpec((1, 3, 1, d), lambda i: (i // tiles_per_seq, 1, 0, 0)),
            _const_spec(gpre.shape),
            _const_spec(w1_bf.shape),
            _const_spec(w2_bf.shape),
            _const_spec(gpost.shape),
        ],
        out_specs=pl.BlockSpec((TML, d), lambda i: (i, 0)),
        out_shape=jax.ShapeDtypeStruct((t, d), F32),
        compiler_params=pltpu.CompilerParams(
            dimension_semantics=("arbitrary",), vmem_limit_bytes=VMEM_LIMIT),
        name="mlp",
    )(x2, mod, gpre, w1_bf, w2_bf, gpost)


def _block_diag(w):
    g, c, dd = w.shape
    eye = jnp.eye(g, dtype=w.dtype)
    return (eye[:, None, :, None] * w[:, :, None, :]).reshape(g * c, g * dd)


def kernel(x, c, w_ada, b_ada, g_mix_pre, g_mix_post, g_ff_pre, g_ff_post, w_in, b_f, w_pool, pool_scale,
           conv_w, w_branch, w_out, w_ff1, w_ff2):
    nb, seq, d = x.shape
    depth = w_ada.shape[0]
    assert all(seq % tile == 0 for tile in (TM, TMM, TML, TQ, TK)) and TMM % SUB == 0
    assert d % LANES == 0 and d % PREP_ROWS == 0
    t = nb * seq
    assert w_in.shape[1] % (depth * ADA_CHUNKS * 16) == 0 and (6 * d) % (ADA_CHUNKS * LANES) == 0
    mod, w_in_bf = _ada(c, w_ada, b_ada, w_in)
    mod = mod.reshape(depth, nb, 6, 1, d)
    x2 = x.reshape(t, d)
    stacked = {"in": w_in, "branch": w_branch, "out": w_out, "ff1": w_ff1, "ff2": w_ff2}
    cast_keys = [("in", l) for l in range(1, depth)]
    cast_keys += [(name, l) for l in range(depth) for name in ("branch", "out", "ff1", "ff2")]
    casts = [(stacked[name], l, 0.5 if name == "branch" else 1.0) for name, l in cast_keys]
    bf16_w = {("in", 0): w_in_bf}
    spread = np.zeros((LANES, LANES), np.float32)
    head_of_lane = np.zeros((LANES,), np.int32)
    lane_used = np.zeros((LANES,), bool)
    for hd in range(HEADS):
        spread[hd, _aug_base(hd):_aug_base(hd) + AUG] = 1.0
        head_of_lane[_aug_base(hd):_aug_base(hd) + AUG] = hd
        lane_used[_aug_base(hd):_aug_base(hd) + AUG] = True
    spread = jnp.asarray(spread, BF16)
    for l in range(depth):
        bf = jnp.where(lane_used, b_f[l][head_of_lane], 0.0).reshape(1, LANES)
        row = lambda a: a[l].reshape(1, -1)
        (q8, k8, v8, pu, cv, gl), cast = _inproj(x2, mod[l], row(g_mix_pre), bf16_w[("in", l)], spread, bf, seq,
                                                 casts if l == 0 else [])
        if l == 0:
            bf16_w.update(zip(cast_keys, cast))
        o = _attention(q8, k8, v8, seq)
        x2 = _merge(o, pu, cv, gl, x2, mod[l], _block_diag(w_pool[l]).astype(BF16), row(pool_scale),
                    conv_w[l], bf16_w[("branch", l)], bf16_w[("out", l)], row(g_mix_post), seq)
        x2 = _mlp(x2, mod[l], row(g_ff_pre), bf16_w[("ff1", l)], bf16_w[("ff2", l)], row(g_ff_post), seq)
    return x2.reshape(nb, seq, d)
```

```python
import functools

import jax
import jax.numpy as jnp
import numpy as np
from jax import lax
from jax.experimental import pallas as pl
from jax.experimental.pallas import tpu as pltpu

F32 = jnp.float32
BF16 = jnp.bfloat16

LANES = 128
HEAD_DIM = 64
HEADS = 8
PAIRS = HEADS // 2
A_WIDTH = HEADS * HEAD_DIM
POOL_WINDOWS = (2, 4, 8, 16)
POOL_WIDTH = 256
CONV_WIDTH = 256
HALO = 16
RMS_EPS = 1e-6
NEG_INF = -1e30
LOG2E = 1.4426950408889634
AUG = 6
PV_ROWS = 128
ADA_CHUNKS = 8
VMEM_LIMIT = 56 * 1024 * 1024

TM = 512
TMM = 1024
TML = 1024
TQ = 512
TK = 512
FF_CHUNK = 1024
SUB = 256
PREP_ROWS = 128


def _const_spec(shape):
    n = len(shape)
    return pl.BlockSpec(shape, lambda *_: (0,) * n, pipeline_mode=pl.Buffered(1))


def _rms(x, g):
    ms = jnp.mean(x * x, axis=-1, keepdims=True)
    return x * lax.rsqrt(ms + RMS_EPS) * g


def _row_chunks(k, steps):
    rows = -(-(-(-k // steps)) // 16) * 16
    return rows, -(-k // rows)


def _nt_dot(a, b):
    return lax.dot_general(a, b, (((1,), (1,)), ((), ())), preferred_element_type=F32)


def _ada_kernel(c_ref, w_ref, b_ref, win_ref, o_ref, win_bf_ref):
    c = c_ref[...]
    sc = c * (1.0 / (1.0 + jnp.exp(-c)))
    o_ref[0] = jnp.dot(sc, w_ref[0], precision=lax.Precision.HIGHEST,
                       preferred_element_type=F32) + b_ref[0]
    win_bf_ref[...] = win_ref[0].astype(BF16)


def _ada(c, w_ada, b_ada, w_in_t):
    depth, d, d6 = w_ada.shape
    nb = c.shape[0]
    n = ADA_CHUNKS
    cols = d6 // n
    rows, nblk = _row_chunks(w_in_t.shape[1], depth * n)
    assert nblk == depth * n
    ncol_in = w_in_t.shape[2]
    return pl.pallas_call(
        _ada_kernel,
        grid=(depth, n),
        in_specs=[
            pl.BlockSpec((nb, d), lambda l, j: (0, 0)),
            pl.BlockSpec((1, d, cols), lambda l, j: (l, 0, j)),
            pl.BlockSpec((1, 1, cols), lambda l, j: (l, 0, j)),
            pl.BlockSpec((1, rows, ncol_in), lambda l, j: (0, l * n + j, 0)),
        ],
        out_specs=(
            pl.BlockSpec((1, nb, cols), lambda l, j: (l, 0, j)),
            pl.BlockSpec((rows, ncol_in), lambda l, j: (l * n + j, 0)),
        ),
        out_shape=(
            jax.ShapeDtypeStruct((depth, nb, d6), F32),
            jax.ShapeDtypeStruct(w_in_t.shape[1:], BF16),
        ),
        compiler_params=pltpu.CompilerParams(
            dimension_semantics=("arbitrary", "arbitrary"), vmem_limit_bytes=VMEM_LIMIT),
        name="ada",
    )(c, w_ada, b_ada.reshape(depth, 1, d6), w_in_t)


def _aug_base(h):
    return (HEAD_DIM if h % 2 == 0 else 0) + AUG * (h // 2)


def _inproj_kernel(tiles_per_seq, cast_scales, x_ref, mod_ref, g_ref, w_ref, spread_ref, bf_ref, *refs):
    n_cast = len(cast_scales)
    cast_in = refs[:n_cast]
    q_ref, k_ref, v_ref, pu_ref, cv_ref, gl_ref = refs[n_cast:n_cast + 6]
    cast_out = refs[n_cast + 6:2 * n_cast + 6]
    carry_ref, wf_ref, wrest_ref = refs[2 * n_cast + 6:]
    i = pl.program_id(0)
    d = x_ref.shape[1]

    for src, dst, s in zip(cast_in, cast_out, cast_scales):
        dst[...] = (src[0] if s == 1.0 else src[0] * s).astype(BF16)

    fo = 3 * A_WIDTH
    n_plain = POOL_WIDTH + 3 * CONV_WIDTH
    n_rest = wrest_ref.shape[0]

    @pl.when(i == 0)
    def _():
        wf_ref[...] = jnp.dot(spread_ref[...], w_ref[fo:fo + LANES, :],
                              preferred_element_type=F32).astype(BF16)
        for r in range(0, n_rest, PREP_ROWS):
            end = min(fo + r + PREP_ROWS + 2 * HEADS, w_ref.shape[0])
            blk = w_ref[fo + r:end, :].astype(F32)[HEADS:HEADS + PREP_ROWS]
            wrest_ref[r:r + PREP_ROWS, :] = (blk if r < n_plain else blk * 0.5).astype(BF16)

    x = x_ref[...]
    shift = mod_ref[0, 0]
    scale = mod_ref[0, 1]
    h = _rms(x, g_ref[...]) * (1.0 + scale) + shift
    hb = h.astype(BF16)
    tm = x.shape[0]

    zf = _nt_dot(hb, wf_ref[...]) + bf_ref[...]
    lf = jnp.minimum(zf, 0.0) - jnp.log(1.0 + jnp.exp(-jnp.abs(zf)))
    row = lax.broadcasted_iota(jnp.int32, lf.shape, 0)
    step = 1
    while step < tm:
        lf = lf + jnp.where(row >= step, pltpu.roll(lf, step, 0), 0.0)
        step *= 2

    @pl.when(i % tiles_per_seq == 0)
    def _():
        carry_ref[...] = jnp.zeros_like(carry_ref)

    fc = lf + carry_ref[0:1, :]
    carry_ref[...] = jnp.broadcast_to(fc[tm - 1:tm, :], carry_ref.shape)

    f2 = fc * LOG2E
    hi = f2.astype(BF16).astype(F32)
    rem = f2 - hi
    mid = rem.astype(BF16).astype(F32)
    lo = rem - mid
    lane = lax.broadcasted_iota(jnp.int32, f2.shape, 1)
    j = (lane % HEAD_DIM) % AUG
    part = jnp.where(j % 3 == 0, hi, jnp.where(j % 3 == 1, mid, lo))
    aug_q = jnp.where(j < 3, 1.0, part)
    aug_k = jnp.where(j < 3, -part, 1.0)

    def proj(lo_, hi_):
        return _nt_dot(hb, w_ref[lo_:hi_, :])

    zq = proj(0, A_WIDTH) * (HEAD_DIM ** -0.5 * LOG2E)
    zk = proj(A_WIDTH, 2 * A_WIDTH)
    zv = proj(2 * A_WIDTH, 3 * A_WIDTH)
    for hd in range(HEADS):
        p = hd // 2
        in_head = (lane >= (hd % 2) * HEAD_DIM) & (lane < (hd % 2 + 1) * HEAD_DIM)
        base = _aug_base(hd)
        in_aug = (lane >= base) & (lane < base + AUG)
        pair = slice(p * LANES, (p + 1) * LANES)
        q_ref[hd] = jnp.where(in_head, zq[:, pair], jnp.where(in_aug, aug_q, 0.0)).astype(BF16)
        k_ref[hd] = jnp.where(in_head, zk[:, pair], jnp.where(in_aug, aug_k, 0.0)).astype(BF16)
        ones_lane = HEAD_DIM if hd % 2 == 0 else HEAD_DIM - 1
        v_ref[hd] = jnp.where(in_head, zv[:, pair], jnp.where(lane == ones_lane, 1.0, 0.0)).astype(BF16)

    def rest(lo_, hi_):
        return _nt_dot(hb, wrest_ref[lo_:hi_, :])

    pu_ref[...] = rest(0, POOL_WIDTH).astype(BF16)
    cv_ref[...] = rest(POOL_WIDTH, n_plain).astype(BF16)
    for jj in range(3):
        gl_ref[:, jj * d:(jj + 1) * d] = rest(n_plain + jj * d, n_plain + (jj + 1) * d).astype(BF16)


def _inproj(x2, mod, g, w_in_bf, spread, bf, seq, casts):
    t, d = x2.shape
    nt = t // TM
    tiles_per_seq = seq // TM
    n_rest = w_in_bf.shape[0] - 3 * A_WIDTH - HEADS
    cast_in_specs, cast_out_specs, cast_shapes = [], [], []
    for w, l, _ in casts:
        _, k, n = w.shape
        rows, nblk = _row_chunks(k, nt)
        cast_in_specs.append(pl.BlockSpec(
            (1, rows, n), functools.partial(lambda l_, nb_, i: (l_, jnp.minimum(i, nb_ - 1), 0), l, nblk)))
        cast_out_specs.append(pl.BlockSpec(
            (rows, n), functools.partial(lambda nb_, i: (jnp.minimum(i, nb_ - 1), 0), nblk)))
        cast_shapes.append(jax.ShapeDtypeStruct((k, n), BF16))
    out_shape = (
        jax.ShapeDtypeStruct((HEADS, t, LANES), BF16),
        jax.ShapeDtypeStruct((HEADS, t, LANES), BF16),
        jax.ShapeDtypeStruct((HEADS, t, LANES), BF16),
        jax.ShapeDtypeStruct((t, POOL_WIDTH), BF16),
        jax.ShapeDtypeStruct((t, 3 * CONV_WIDTH), BF16),
        jax.ShapeDtypeStruct((t, 3 * d), BF16),
    )
    head_spec = pl.BlockSpec((HEADS, TM, LANES), lambda i: (0, i, 0))
    out_specs = (
        head_spec, head_spec, head_spec,
        pl.BlockSpec((TM, POOL_WIDTH), lambda i: (i, 0)),
        pl.BlockSpec((TM, 3 * CONV_WIDTH), lambda i: (i, 0)),
        pl.BlockSpec((TM, 3 * d), lambda i: (i, 0)),
    )
    outs = pl.pallas_call(
        functools.partial(_inproj_kernel, tiles_per_seq, tuple(s for _, _, s in casts)),
        grid=(nt,),
        in_specs=[
            pl.BlockSpec((TM, d), lambda i: (i, 0)),
            pl.BlockSpec((1, 2, 1, d), lambda i: (i // tiles_per_seq, 0, 0, 0)),
            _const_spec((1, d)),
            _const_spec(w_in_bf.shape),
            _const_spec(spread.shape),
            _const_spec((1, LANES)),
        ] + cast_in_specs,
        out_specs=out_specs + tuple(cast_out_specs),
        out_shape=out_shape + tuple(cast_shapes),
        scratch_shapes=[
            pltpu.VMEM((8, LANES), F32),
            pltpu.VMEM((LANES, d), BF16),
            pltpu.VMEM((n_rest, d), BF16),
        ],
        compiler_params=pltpu.CompilerParams(
            dimension_semantics=("arbitrary",), vmem_limit_bytes=VMEM_LIMIT),
        name="inproj",
    )(x2, mod, g, w_in_bf, spread, bf, *[w for w, _, _ in casts])
    return outs[:6], outs[6:]


def _attn_kernel(q_ref, k_ref, v_ref, o_ref):
    seq = q_ref.shape[1]
    nq = seq // TQ
    half = TK // 2
    steps = [(qi, kt) for qi in range(nq) for kt in range(qi + 1)]
    mask0 = (lax.broadcasted_iota(jnp.int32, (half, TQ), 0)
             <= lax.broadcasted_iota(jnp.int32, (half, TQ), 1))
    mask1 = (lax.broadcasted_iota(jnp.int32, (half, half), 0)
             <= lax.broadcasted_iota(jnp.int32, (half, half), 1))
    vts = [v_ref[hh].astype(F32).T.astype(BF16)[hh * (LANES - PV_ROWS):hh * (LANES - PV_ROWS) + PV_ROWS]
           for hh in range(2)]

    def logits(step, hh):
        qi, kt = step
        q = q_ref[hh, qi * TQ:(qi + 1) * TQ, :]
        if kt < qi:
            st = _nt_dot(k_ref[hh, kt * TK:(kt + 1) * TK, :], q)
            return (st,), jnp.max(st, axis=0, keepdims=True)
        st0 = jnp.where(mask0, _nt_dot(k_ref[hh, kt * TK:kt * TK + half, :], q), NEG_INF)
        st1 = jnp.where(mask1, _nt_dot(k_ref[hh, kt * TK + half:(kt + 1) * TK, :], q[half:]), NEG_INF)
        c0 = jnp.max(st0, axis=0, keepdims=True)
        c1 = jnp.max(st1, axis=0, keepdims=True)
        return (st0, st1), jnp.concatenate([c0[:, :half], jnp.maximum(c0[:, half:], c1)], axis=1)

    def update(step, hh, sts, cmax, m, acc):
        qi, kt = step
        vt = vts[hh]
        m_new = cmax if m is None else jnp.maximum(m, cmax)
        if kt < qi:
            pt = jnp.exp2(sts[0] - m_new).astype(BF16)
            pv = jnp.dot(vt[:, kt * TK:(kt + 1) * TK], pt, preferred_element_type=F32)
        else:
            pt0 = jnp.exp2(sts[0] - m_new).astype(BF16)
            pt1 = jnp.exp2(sts[1] - m_new[:, half:]).astype(BF16)
            pv = jnp.dot(vt[:, kt * TK:kt * TK + half], pt0, preferred_element_type=F32)
            pv1 = jnp.dot(vt[:, kt * TK + half:(kt + 1) * TK], pt1, preferred_element_type=F32)
            pv = jnp.concatenate([pv[:, :half], pv[:, half:] + pv1], axis=1)
        if m is not None:
            pv = jnp.exp2(m - m_new) * acc + pv
        return m_new, pv

    cur = [logits(steps[0], hh) for hh in range(2)]
    state = [(None, None), (None, None)]
    for s, step in enumerate(steps):
        qi, kt = step
        nxt = [None, None]
        for hh in range(2):
            if s + 1 < len(steps):
                nxt[hh] = logits(steps[s + 1], hh)
            state[hh] = update(step, hh, *cur[hh], *state[hh])
        cur = nxt
        if kt == qi:
            a0, a1 = state[0][1], state[1][1]
            pad = PV_ROWS - HEAD_DIM
            out = jnp.concatenate([a0[0:HEAD_DIM] / a0[HEAD_DIM:HEAD_DIM + 1],
                                   a1[pad:] / a1[pad - 1:pad]], axis=0)
            o_ref[qi * TQ:(qi + 1) * TQ, :] = out.T.astype(BF16)
            state = [(None, None), (None, None)]


def _attention(q8, k8, v8, seq):
    _, t, _ = q8.shape
    nb = t // seq
    pair_spec = pl.BlockSpec((2, seq, LANES), lambda b, p: (p, b, 0))
    return pl.pallas_call(
        _attn_kernel,
        grid=(nb, PAIRS),
        in_specs=[pair_spec, pair_spec, pair_spec],
        out_specs=pl.BlockSpec((seq, LANES), lambda b, p: (b, p)),
        out_shape=jax.ShapeDtypeStruct((t, A_WIDTH), BF16),
        compiler_params=pltpu.CompilerParams(
            dimension_semantics=("arbitrary", "arbitrary"), vmem_limit_bytes=VMEM_LIMIT),
        name="attn",
    )(q8, k8, v8)


def _merge_kernel(tiles_per_seq, o_ref, pu_ref, puh_ref, cv_ref, cvh_ref, gl_ref, x_ref, gate_ref,
                  wpool_ref, pscale_ref, convw_ref, wbr_ref, wout_ref, g_ref, out_ref):
    i = pl.program_id(0)
    tile_in_seq = i % tiles_per_seq
    tm, d = x_ref.shape
    n_sub = tm // SUB
    lane = lax.broadcasted_iota(jnp.int32, (SUB + HALO, POOL_WIDTH), 1)
    row = lax.broadcasted_iota(jnp.int32, (SUB + HALO, POOL_WIDTH), 0)
    gd = POOL_WIDTH // len(POOL_WINDOWS)
    wsz = jnp.where(lane < gd, 2.0, jnp.where(lane < 2 * gd, 4.0, jnp.where(lane < 3 * gd, 8.0, 16.0)))
    cw = convw_ref[...]
    o1 = A_WIDTH + POOL_WIDTH

    def with_history(ref, halo_ref, j):
        if j == 0:
            head = jnp.where(tile_in_seq != 0, halo_ref[...].astype(F32), 0.0)
        else:
            head = ref[j * SUB - HALO:j * SUB, :].astype(F32)
        return jnp.concatenate([head, ref[j * SUB:(j + 1) * SUB, :].astype(F32)], axis=0)

    def mix(j):
        rows = slice(j * SUB, (j + 1) * SUB)
        ext = with_history(pu_ref, puh_ref, j)
        a2 = ext + pltpu.roll(ext, 1, 0)
        a4 = a2 + pltpu.roll(a2, 2, 0)
        a8 = a4 + pltpu.roll(a4, 4, 0)
        a16 = a8 + pltpu.roll(a8, 8, 0)
        win = jnp.where(lane < gd, a2, jnp.where(lane < 2 * gd, a4, jnp.where(lane < 3 * gd, a8, a16)))
        frames = (tile_in_seq * tm + j * SUB + row - (HALO - 1)).astype(F32)
        cnt = jnp.maximum(jnp.minimum(frames, wsz), 1.0)
        pm = (win / cnt - ext)[HALO:]
        br_b = jnp.dot(pm.astype(BF16), wpool_ref[...], preferred_element_type=F32) * pscale_ref[...]

        cve = with_history(cv_ref, cvh_ref, j)
        u = cve[:, 2 * CONV_WIDTH:] * cve[:, :CONV_WIDTH]
        y = cw[0:1] * pltpu.roll(u, 2, 0) + cw[1:2] * pltpu.roll(u, 1, 0) + cw[2:3] * u
        br_c = (cve[:, CONV_WIDTH:2 * CONV_WIDTH] * y)[HALO:]

        def gated(n, branch, lo, hi):
            b = jnp.dot(branch, wbr_ref[lo:hi, :], preferred_element_type=F32)
            return b + b * jnp.tanh(gl_ref[rows, n * d:(n + 1) * d].astype(F32))

        merged = gated(0, o_ref[rows, :], 0, A_WIDTH)
        merged += gated(1, br_b.astype(BF16), A_WIDTH, o1)
        merged += gated(2, br_c.astype(BF16), o1, o1 + CONV_WIDTH)
        return merged.astype(BF16)

    def project(j, merged):
        rows = slice(j * SUB, (j + 1) * SUB)
        yo = jnp.dot(merged, wout_ref[...], preferred_element_type=F32)
        out_ref[rows, :] = x_ref[rows, :] + gate_ref[0, 0] * _rms(yo, g_ref[...])

    cur = mix(0)
    for j in range(n_sub):
        nxt = mix(j + 1) if j + 1 < n_sub else None
        project(j, cur)
        cur = nxt


def _merge(o, pu, cv, gl, x2, mod, wpool, pscale, convw, wbr_bf, wout_bf, g, seq):
    t, d = x2.shape
    nt = t // TMM
    tiles_per_seq = seq // TMM
    hb = TMM // HALO

    def halo_map(i):
        return (jnp.maximum(i * hb - 1, 0), 0)

    return pl.pallas_call(
        functools.partial(_merge_kernel, tiles_per_seq),
        grid=(nt,),
        in_specs=[
            pl.BlockSpec((TMM, A_WIDTH), lambda i: (i, 0)),
            pl.BlockSpec((TMM, POOL_WIDTH), lambda i: (i, 0)),
            pl.BlockSpec((HALO, POOL_WIDTH), halo_map),
            pl.BlockSpec((TMM, 3 * CONV_WIDTH), lambda i: (i, 0)),
            pl.BlockSpec((HALO, 3 * CONV_WIDTH), halo_map),
            pl.BlockSpec((TMM, 3 * d), lambda i: (i, 0)),
            pl.BlockSpec((TMM, d), lambda i: (i, 0)),
            pl.BlockSpec((1, 1, 1, d), lambda i: (i // tiles_per_seq, 2, 0, 0)),
            _const_spec(wpool.shape),
            _const_spec(pscale.shape),
            _const_spec(convw.shape),
            _const_spec(wbr_bf.shape),
            _const_spec(wout_bf.shape),
            _const_spec(g.shape),
        ],
        out_specs=pl.BlockSpec((TMM, d), lambda i: (i, 0)),
        out_shape=jax.ShapeDtypeStruct((t, d), F32),
        compiler_params=pltpu.CompilerParams(
            dimension_semantics=("arbitrary",), vmem_limit_bytes=VMEM_LIMIT),
        name="merge",
    )(o, pu, pu, cv, cv, gl, x2, mod, wpool, pscale, convw, wbr_bf, wout_bf, g)


def _mlp_kernel(x_ref, mod_ref, gpre_ref, w1_ref, w2_ref, gpost_ref, out_ref):
    x = x_ref[...]
    shift = mod_ref[0, 0]
    scale = mod_ref[0, 1]
    gate = mod_ref[0, 2]
    hb = (_rms(x, gpre_ref[...]) * (1.0 + scale) + shift).astype(BF16)
    dff = w1_ref.shape[1]
    acc = jnp.zeros(x.shape, F32)
    for c in range(dff // FF_CHUNK):
        up = jnp.dot(hb, w1_ref[:, c * FF_CHUNK:(c + 1) * FF_CHUNK], preferred_element_type=F32)
        act = jnp.square(jnp.maximum(up, 0.0)).astype(BF16)
        acc += jnp.dot(act, w2_ref[c * FF_CHUNK:(c + 1) * FF_CHUNK, :], preferred_element_type=F32)
    out_ref[...] = x + gate * _rms(acc, gpost_ref[...])


def _mlp(x2, mod, gpre, w1_bf, w2_bf, gpost, seq):
    t, d = x2.shape
    nt = t // TML
    tiles_per_seq = seq // TML
    return pl.pallas_call(
        _mlp_kernel,
        grid=(nt,),
        in_specs=[
            pl.BlockSpec((TML, d), lambda i: (i, 0)),
            pl.BlockSpec((1, 3, 1, d), lambda i: (i // tiles_per_seq, 1, 0, 0)),
            _const_spec(gpre.shape),
            _const_spec(w1_bf.shape),
            _const_spec(w2_bf.shape),
            _const_spec(gpost.shape),
        ],
        out_specs=pl.BlockSpec((TML, d), lambda i: (i, 0)),
        out_shape=jax.ShapeDtypeStruct((t, d), F32),
        compiler_params=pltpu.CompilerParams(
            dimension_semantics=("arbitrary",), vmem_limit_bytes=VMEM_LIMIT),
        name="mlp",
    )(x2, mod, gpre, w1_bf, w2_bf, gpost)


def _block_diag(w):
    g, c, dd = w.shape
    eye = jnp.eye(g, dtype=w.dtype)
    return (eye[:, None, :, None] * w[:, :, None, :]).reshape(g * c, g * dd)


def kernel(x, c, w_ada, b_ada, g_mix_pre, g_mix_post, g_ff_pre, g_ff_post, w_in, b_f, w_pool, pool_scale,
           conv_w, w_branch, w_out, w_ff1, w_ff2):
    nb, seq, d = x.shape
    depth = w_ada.shape[0]
    assert all(seq % tile == 0 for tile in (TM, TMM, TML, TQ, TK)) and TMM % SUB == 0
    assert d % LANES == 0 and d % PREP_ROWS == 0
    t = nb * seq
    assert (6 * d) % (ADA_CHUNKS * LANES) == 0
    w_in_t = jnp.swapaxes(w_in, 1, 2)
    mod, w_in_bf = _ada(c, w_ada, b_ada, w_in_t)
    mod = mod.reshape(depth, nb, 6, 1, d)
    x2 = x.reshape(t, d)
    stacked = {"in": w_in_t, "branch": w_branch, "out": w_out, "ff1": w_ff1, "ff2": w_ff2}
    cast_keys = [("in", l) for l in range(1, depth)]
    cast_keys += [(name, l) for l in range(depth) for name in ("branch", "out", "ff1", "ff2")]
    casts = [(stacked[name], l, 0.5 if name == "branch" else 1.0) for name, l in cast_keys]
    bf16_w = {("in", 0): w_in_bf}
    spread = np.zeros((LANES, LANES), np.float32)
    head_of_lane = np.zeros((LANES,), np.int32)
    lane_used = np.zeros((LANES,), bool)
    for hd in range(HEADS):
        spread[_aug_base(hd):_aug_base(hd) + AUG, hd] = 1.0
        head_of_lane[_aug_base(hd):_aug_base(hd) + AUG] = hd
        lane_used[_aug_base(hd):_aug_base(hd) + AUG] = True
    spread = jnp.asarray(spread, BF16)
    for l in range(depth):
        bf = jnp.where(lane_used, b_f[l][head_of_lane], 0.0).reshape(1, LANES)
        row = lambda a: a[l].reshape(1, -1)
        (q8, k8, v8, pu, cv, gl), cast = _inproj(x2, mod[l], row(g_mix_pre), bf16_w[("in", l)], spread, bf, seq,
                                                 casts if l == 0 else [])
        if l == 0:
            bf16_w.update(zip(cast_keys, cast))
        o = _attention(q8, k8, v8, seq)
        x2 = _merge(o, pu, cv, gl, x2, mod[l], _block_diag(w_pool[l]).astype(BF16), row(pool_scale),
                    conv_w[l], bf16_w[("branch", l)], bf16_w[("out", l)], row(g_mix_post), seq)
        x2 = _mlp(x2, mod[l], row(g_ff_pre), bf16_w[("ff1", l)], bf16_w[("ff2", l)], row(g_ff_post), seq)
    return x2.reshape(nb, seq, d)
```

```python
import functools

import jax
import jax.numpy as jnp
import numpy as np
from jax import lax
from jax.experimental import pallas as pl
from jax.experimental.pallas import tpu as pltpu

F32 = jnp.float32
BF16 = jnp.bfloat16

LANES = 128
HEAD_DIM = 64
HEADS = 8
PAIRS = HEADS // 2
A_WIDTH = HEADS * HEAD_DIM
POOL_WINDOWS = (2, 4, 8, 16)
POOL_WIDTH = 256
CONV_WIDTH = 256
HALO = 16
RMS_EPS = 1e-6
NEG_INF = -1e30
LOG2E = 1.4426950408889634
AUG = 6
PV_ROWS = 128
ADA_CHUNKS = 8
VMEM_LIMIT = 56 * 1024 * 1024

TM = 512
TMM = 1024
TML = 1024
TQ = 512
TK = 512
FF_CHUNK = 1024
SUB = 256
PREP_ROWS = 128


def _const_spec(shape):
    n = len(shape)
    return pl.BlockSpec(shape, lambda *_: (0,) * n, pipeline_mode=pl.Buffered(1))


def _rms(x, g):
    ms = jnp.mean(x * x, axis=-1, keepdims=True)
    return x * lax.rsqrt(ms + RMS_EPS) * g


def _row_chunks(k, steps):
    rows = -(-(-(-k // steps)) // 16) * 16
    return rows, -(-k // rows)


def _nt_dot(a, b):
    return lax.dot_general(a, b, (((1,), (1,)), ((), ())), preferred_element_type=F32)


def _ada_kernel(c_ref, w_ref, b_ref, win_ref, o_ref, win_bf_ref):
    c = c_ref[...]
    sc = c * (1.0 / (1.0 + jnp.exp(-c)))
    o_ref[0] = jnp.dot(sc, w_ref[0], precision=lax.Precision.HIGHEST,
                       preferred_element_type=F32) + b_ref[0]
    win_bf_ref[...] = win_ref[0].astype(BF16)


def _ada(c, w_ada, b_ada, w_in_t):
    depth, d, d6 = w_ada.shape
    nb = c.shape[0]
    n = ADA_CHUNKS
    cols = d6 // n
    rows, nblk = _row_chunks(w_in_t.shape[1], depth * n)
    assert nblk == depth * n
    ncol_in = w_in_t.shape[2]
    return pl.pallas_call(
        _ada_kernel,
        grid=(depth, n),
        in_specs=[
            pl.BlockSpec((nb, d), lambda l, j: (0, 0)),
            pl.BlockSpec((1, d, cols), lambda l, j: (l, 0, j)),
            pl.BlockSpec((1, 1, cols), lambda l, j: (l, 0, j)),
            pl.BlockSpec((1, rows, ncol_in), lambda l, j: (0, l * n + j, 0)),
        ],
        out_specs=(
            pl.BlockSpec((1, nb, cols), lambda l, j: (l, 0, j)),
            pl.BlockSpec((rows, ncol_in), lambda l, j: (l * n + j, 0)),
        ),
        out_shape=(
            jax.ShapeDtypeStruct((depth, nb, d6), F32),
            jax.ShapeDtypeStruct(w_in_t.shape[1:], BF16),
        ),
        compiler_params=pltpu.CompilerParams(
            dimension_semantics=("arbitrary", "arbitrary"), vmem_limit_bytes=VMEM_LIMIT),
        name="ada",
    )(c, w_ada, b_ada.reshape(depth, 1, d6), w_in_t)


def _aug_base(h):
    return (HEAD_DIM if h % 2 == 0 else 0) + AUG * (h // 2)


def _inproj_kernel(tiles_per_seq, cast_scales, x_ref, mod_ref, xn_ref, modn_ref, g_ref, w_ref, spread_ref,
                   bf_ref, *refs):
    n_cast = len(cast_scales)
    cast_in = refs[:n_cast]
    q_ref, k_ref, v_ref, pu_ref, cv_ref, gl_ref = refs[n_cast:n_cast + 6]
    cast_out = refs[n_cast + 6:2 * n_cast + 6]
    carry_ref, wf_ref, wrest_ref, hb_ref = refs[2 * n_cast + 6:]
    i = pl.program_id(0)
    d = x_ref.shape[1]

    for src, dst, s in zip(cast_in, cast_out, cast_scales):
        dst[...] = (src[0] if s == 1.0 else src[0] * s).astype(BF16)

    fo = 3 * A_WIDTH
    n_plain = POOL_WIDTH + 3 * CONV_WIDTH
    n_rest = wrest_ref.shape[0]

    @pl.when(i == 0)
    def _():
        wf_ref[...] = jnp.dot(spread_ref[...], w_ref[fo:fo + LANES, :],
                              preferred_element_type=F32).astype(BF16)
        for r in range(0, n_rest, PREP_ROWS):
            end = min(fo + r + PREP_ROWS + 2 * HEADS, w_ref.shape[0])
            blk = w_ref[fo + r:end, :].astype(F32)[HEADS:HEADS + PREP_ROWS]
            wrest_ref[r:r + PREP_ROWS, :] = (blk if r < n_plain else blk * 0.5).astype(BF16)

    def normed(xr, modr):
        return (_rms(xr[...], g_ref[...]) * (1.0 + modr[0, 1]) + modr[0, 0]).astype(BF16)

    @pl.when(i == 0)
    def _():
        hb_ref[...] = normed(x_ref, mod_ref)

    hb = hb_ref[...]
    tm = hb.shape[0]

    def proj(lo_, hi_):
        return _nt_dot(hb, w_ref[lo_:hi_, :])

    def rest(lo_, hi_):
        return _nt_dot(hb, wrest_ref[lo_:hi_, :])

    zf = _nt_dot(hb, wf_ref[...]) + bf_ref[...]
    pu_ref[...] = rest(0, POOL_WIDTH).astype(BF16)
    cv_ref[...] = rest(POOL_WIDTH, n_plain).astype(BF16)
    for jj in range(3):
        gl_ref[:, jj * d:(jj + 1) * d] = rest(n_plain + jj * d, n_plain + (jj + 1) * d).astype(BF16)

    lf = jnp.minimum(zf, 0.0) - jnp.log(1.0 + jnp.exp(-jnp.abs(zf)))
    row = lax.broadcasted_iota(jnp.int32, lf.shape, 0)
    step = 1
    while step < tm:
        lf = lf + jnp.where(row >= step, pltpu.roll(lf, step, 0), 0.0)
        step *= 2

    @pl.when(i % tiles_per_seq == 0)
    def _():
        carry_ref[...] = jnp.zeros_like(carry_ref)

    fc = lf + carry_ref[0:1, :]
    carry_ref[...] = jnp.broadcast_to(fc[tm - 1:tm, :], carry_ref.shape)

    f2 = fc * LOG2E
    hi = f2.astype(BF16).astype(F32)
    rem = f2 - hi
    mid = rem.astype(BF16).astype(F32)
    lo = rem - mid
    lane = lax.broadcasted_iota(jnp.int32, f2.shape, 1)
    j = (lane % HEAD_DIM) % AUG
    part = jnp.where(j % 3 == 0, hi, jnp.where(j % 3 == 1, mid, lo))
    aug_q = jnp.where(j < 3, 1.0, part)
    aug_k = jnp.where(j < 3, -part, 1.0)

    zq = proj(0, A_WIDTH) * (HEAD_DIM ** -0.5 * LOG2E)
    zk = proj(A_WIDTH, 2 * A_WIDTH)
    zv = proj(2 * A_WIDTH, 3 * A_WIDTH)
    for hd in range(HEADS):
        p = hd // 2
        in_head = (lane >= (hd % 2) * HEAD_DIM) & (lane < (hd % 2 + 1) * HEAD_DIM)
        base = _aug_base(hd)
        in_aug = (lane >= base) & (lane < base + AUG)
        pair = slice(p * LANES, (p + 1) * LANES)
        q_ref[hd] = jnp.where(in_head, zq[:, pair], jnp.where(in_aug, aug_q, 0.0)).astype(BF16)
        k_ref[hd] = jnp.where(in_head, zk[:, pair], jnp.where(in_aug, aug_k, 0.0)).astype(BF16)
        ones_lane = HEAD_DIM if hd % 2 == 0 else HEAD_DIM - 1
        v_ref[hd] = jnp.where(in_head, zv[:, pair], jnp.where(lane == ones_lane, 1.0, 0.0)).astype(BF16)

    hb_ref[...] = normed(xn_ref, modn_ref)


def _inproj(x2, mod, g, w_in_bf, spread, bf, seq, casts):
    t, d = x2.shape
    nt = t // TM
    tiles_per_seq = seq // TM
    n_rest = w_in_bf.shape[0] - 3 * A_WIDTH - HEADS
    cast_in_specs, cast_out_specs, cast_shapes = [], [], []
    for w, l, _ in casts:
        _, k, n = w.shape
        rows, nblk = _row_chunks(k, nt)
        cast_in_specs.append(pl.BlockSpec(
            (1, rows, n), functools.partial(lambda l_, nb_, i: (l_, jnp.minimum(i, nb_ - 1), 0), l, nblk)))
        cast_out_specs.append(pl.BlockSpec(
            (rows, n), functools.partial(lambda nb_, i: (jnp.minimum(i, nb_ - 1), 0), nblk)))
        cast_shapes.append(jax.ShapeDtypeStruct((k, n), BF16))
    out_shape = (
        jax.ShapeDtypeStruct((HEADS, t, LANES), BF16),
        jax.ShapeDtypeStruct((HEADS, t, LANES), BF16),
        jax.ShapeDtypeStruct((HEADS, t, LANES), BF16),
        jax.ShapeDtypeStruct((t, POOL_WIDTH), BF16),
        jax.ShapeDtypeStruct((t, 3 * CONV_WIDTH), BF16),
        jax.ShapeDtypeStruct((t, 3 * d), BF16),
    )
    head_spec = pl.BlockSpec((HEADS, TM, LANES), lambda i: (0, i, 0))
    out_specs = (
        head_spec, head_spec, head_spec,
        pl.BlockSpec((TM, POOL_WIDTH), lambda i: (i, 0)),
        pl.BlockSpec((TM, 3 * CONV_WIDTH), lambda i: (i, 0)),
        pl.BlockSpec((TM, 3 * d), lambda i: (i, 0)),
    )
    outs = pl.pallas_call(
        functools.partial(_inproj_kernel, tiles_per_seq, tuple(s for _, _, s in casts)),
        grid=(nt,),
        in_specs=[
            pl.BlockSpec((TM, d), lambda i: (i, 0)),
            pl.BlockSpec((1, 2, 1, d), lambda i: (i // tiles_per_seq, 0, 0, 0)),
            pl.BlockSpec((TM, d), lambda i: (jnp.minimum(i + 1, nt - 1), 0)),
            pl.BlockSpec((1, 2, 1, d), lambda i: (jnp.minimum(i + 1, nt - 1) // tiles_per_seq, 0, 0, 0)),
            _const_spec((1, d)),
            _const_spec(w_in_bf.shape),
            _const_spec(spread.shape),
            _const_spec((1, LANES)),
        ] + cast_in_specs,
        out_specs=out_specs + tuple(cast_out_specs),
        out_shape=out_shape + tuple(cast_shapes),
        scratch_shapes=[
            pltpu.VMEM((8, LANES), F32),
            pltpu.VMEM((LANES, d), BF16),
            pltpu.VMEM((n_rest, d), BF16),
            pltpu.VMEM((TM, d), BF16),
        ],
        compiler_params=pltpu.CompilerParams(
            dimension_semantics=("arbitrary",), vmem_limit_bytes=VMEM_LIMIT),
        name="inproj",
    )(x2, mod, x2, mod, g, w_in_bf, spread, bf, *[w for w, _, _ in casts])
    return outs[:6], outs[6:]


def _attn_kernel(q_ref, k_ref, v_ref, o_ref):
    seq = q_ref.shape[1]
    nq = seq // TQ
    half = TK // 2
    steps = [(qi, kt) for qi in range(nq) for kt in range(qi + 1)]
    mask0 = (lax.broadcasted_iota(jnp.int32, (half, TQ), 0)
             <= lax.broadcasted_iota(jnp.int32, (half, TQ), 1))
    mask1 = (lax.broadcasted_iota(jnp.int32, (half, half), 0)
             <= lax.broadcasted_iota(jnp.int32, (half, half), 1))
    vts = [v_ref[hh].astype(F32).T.astype(BF16)[hh * (LANES - PV_ROWS):hh * (LANES - PV_ROWS) + PV_ROWS]
           for hh in range(2)]

    def logits(step, hh):
        qi, kt = step
        q = q_ref[hh, qi * TQ:(qi + 1) * TQ, :]
        if kt < qi:
            st = _nt_dot(k_ref[hh, kt * TK:(kt + 1) * TK, :], q)
            return (st,), jnp.max(st, axis=0, keepdims=True)
        st0 = jnp.where(mask0, _nt_dot(k_ref[hh, kt * TK:kt * TK + half, :], q), NEG_INF)
        st1 = jnp.where(mask1, _nt_dot(k_ref[hh, kt * TK + half:(kt + 1) * TK, :], q[half:]), NEG_INF)
        c0 = jnp.max(st0, axis=0, keepdims=True)
        c1 = jnp.max(st1, axis=0, keepdims=True)
        return (st0, st1), jnp.concatenate([c0[:, :half], jnp.maximum(c0[:, half:], c1)], axis=1)

    def update(step, hh, sts, cmax, m, acc):
        qi, kt = step
        vt = vts[hh]
        m_new = cmax if m is None else jnp.maximum(m, cmax)
        if kt < qi:
            pt = jnp.exp2(sts[0] - m_new).astype(BF16)
            pv = jnp.dot(vt[:, kt * TK:(kt + 1) * TK], pt, preferred_element_type=F32)
        else:
            pt0 = jnp.exp2(sts[0] - m_new).astype(BF16)
            pt1 = jnp.exp2(sts[1] - m_new[:, half:]).astype(BF16)
            pv = jnp.dot(vt[:, kt * TK:kt * TK + half], pt0, preferred_element_type=F32)
            pv1 = jnp.dot(vt[:, kt * TK + half:(kt + 1) * TK], pt1, preferred_element_type=F32)
            pv = jnp.concatenate([pv[:, :half], pv[:, half:] + pv1], axis=1)
        if m is not None:
            pv = jnp.exp2(m - m_new) * acc + pv
        return m_new, pv

    cur = [logits(steps[0], hh) for hh in range(2)]
    state = [(None, None), (None, None)]
    for s, step in enumerate(steps):
        qi, kt = step
        nxt = [None, None]
        for hh in range(2):
            if s + 1 < len(steps):
                nxt[hh] = logits(steps[s + 1], hh)
            state[hh] = update(step, hh, *cur[hh], *state[hh])
        cur = nxt
        if kt == qi:
            a0, a1 = state[0][1], state[1][1]
            pad = PV_ROWS - HEAD_DIM
            out = jnp.concatenate([a0[0:HEAD_DIM] / a0[HEAD_DIM:HEAD_DIM + 1],
                                   a1[pad:] / a1[pad - 1:pad]], axis=0)
            o_ref[qi * TQ:(qi + 1) * TQ, :] = out.T.astype(BF16)
            state = [(None, None), (None, None)]


def _attention(q8, k8, v8, seq):
    _, t, _ = q8.shape
    nb = t // seq
    pair_spec = pl.BlockSpec((2, seq, LANES), lambda b, p: (p, b, 0))
    return pl.pallas_call(
        _attn_kernel,
        grid=(nb, PAIRS),
        in_specs=[pair_spec, pair_spec, pair_spec],
        out_specs=pl.BlockSpec((seq, LANES), lambda b, p: (b, p)),
        out_shape=jax.ShapeDtypeStruct((t, A_WIDTH), BF16),
        compiler_params=pltpu.CompilerParams(
            dimension_semantics=("arbitrary", "arbitrary"), vmem_limit_bytes=VMEM_LIMIT),
        name="attn",
    )(q8, k8, v8)


def _merge_kernel(tiles_per_seq, o_ref, pu_ref, puh_ref, cv_ref, cvh_ref, gl_ref, x_ref, gate_ref,
                  wpool_ref, pscale_ref, convw_ref, wbr_ref, wout_ref, g_ref, out_ref):
    i = pl.program_id(0)
    tile_in_seq = i % tiles_per_seq
    tm, d = x_ref.shape
    n_sub = tm // SUB
    lane = lax.broadcasted_iota(jnp.int32, (SUB + HALO, POOL_WIDTH), 1)
    row = lax.broadcasted_iota(jnp.int32, (SUB + HALO, POOL_WIDTH), 0)
    gd = POOL_WIDTH // len(POOL_WINDOWS)
    wsz = jnp.where(lane < gd, 2.0, jnp.where(lane < 2 * gd, 4.0, jnp.where(lane < 3 * gd, 8.0, 16.0)))
    cw = convw_ref[...]
    o1 = A_WIDTH + POOL_WIDTH

    def with_history(ref, halo_ref, j):
        if j == 0:
            head = jnp.where(tile_in_seq != 0, halo_ref[...].astype(F32), 0.0)
        else:
            head = ref[j * SUB - HALO:j * SUB, :].astype(F32)
        return jnp.concatenate([head, ref[j * SUB:(j + 1) * SUB, :].astype(F32)], axis=0)

    def mix(j):
        rows = slice(j * SUB, (j + 1) * SUB)
        ext = with_history(pu_ref, puh_ref, j)
        a2 = ext + pltpu.roll(ext, 1, 0)
        a4 = a2 + pltpu.roll(a2, 2, 0)
        a8 = a4 + pltpu.roll(a4, 4, 0)
        a16 = a8 + pltpu.roll(a8, 8, 0)
        win = jnp.where(lane < gd, a2, jnp.where(lane < 2 * gd, a4, jnp.where(lane < 3 * gd, a8, a16)))
        frames = (tile_in_seq * tm + j * SUB + row - (HALO - 1)).astype(F32)
        cnt = jnp.maximum(jnp.minimum(frames, wsz), 1.0)
        pm = (win / cnt - ext)[HALO:]
        br_b = jnp.dot(pm.astype(BF16), wpool_ref[...], preferred_element_type=F32) * pscale_ref[...]

        cve = with_history(cv_ref, cvh_ref, j)
        u = cve[:, 2 * CONV_WIDTH:] * cve[:, :CONV_WIDTH]
        y = cw[0:1] * pltpu.roll(u, 2, 0) + cw[1:2] * pltpu.roll(u, 1, 0) + cw[2:3] * u
        br_c = (cve[:, CONV_WIDTH:2 * CONV_WIDTH] * y)[HALO:]

        def gated(n, branch, lo, hi):
            b = jnp.dot(branch, wbr_ref[lo:hi, :], preferred_element_type=F32)
            return b + b * jnp.tanh(gl_ref[rows, n * d:(n + 1) * d].astype(F32))

        merged = gated(0, o_ref[rows, :], 0, A_WIDTH)
        merged += gated(1, br_b.astype(BF16), A_WIDTH, o1)
        merged += gated(2, br_c.astype(BF16), o1, o1 + CONV_WIDTH)
        return merged.astype(BF16)

    def project(j, merged):
        rows = slice(j * SUB, (j + 1) * SUB)
        yo = jnp.dot(merged, wout_ref[...], preferred_element_type=F32)
        out_ref[rows, :] = x_ref[rows, :] + gate_ref[0, 0] * _rms(yo, g_ref[...])

    cur = mix(0)
    for j in range(n_sub):
        nxt = mix(j + 1) if j + 1 < n_sub else None
        project(j, cur)
        cur = nxt


def _merge(o, pu, cv, gl, x2, mod, wpool, pscale, convw, wbr_bf, wout_bf, g, seq):
    t, d = x2.shape
    nt = t // TMM
    tiles_per_seq = seq // TMM
    hb = TMM // HALO

    def halo_map(i):
        return (jnp.maximum(i * hb - 1, 0), 0)

    return pl.pallas_call(
        functools.partial(_merge_kernel, tiles_per_seq),
        grid=(nt,),
        in_specs=[
            pl.BlockSpec((TMM, A_WIDTH), lambda i: (i, 0)),
            pl.BlockSpec((TMM, POOL_WIDTH), lambda i: (i, 0)),
            pl.BlockSpec((HALO, POOL_WIDTH), halo_map),
            pl.BlockSpec((TMM, 3 * CONV_WIDTH), lambda i: (i, 0)),
            pl.BlockSpec((HALO, 3 * CONV_WIDTH), halo_map),
            pl.BlockSpec((TMM, 3 * d), lambda i: (i, 0)),
            pl.BlockSpec((TMM, d), lambda i: (i, 0)),
            pl.BlockSpec((1, 1, 1, d), lambda i: (i // tiles_per_seq, 2, 0, 0)),
            _const_spec(wpool.shape),
            _const_spec(pscale.shape),
            _const_spec(convw.shape),
            _const_spec(wbr_bf.shape),
            _const_spec(wout_bf.shape),
            _const_spec(g.shape),
        ],
        out_specs=pl.BlockSpec((TMM, d), lambda i: (i, 0)),
        out_shape=jax.ShapeDtypeStruct((t, d), F32),
        compiler_params=pltpu.CompilerParams(
            dimension_semantics=("arbitrary",), vmem_limit_bytes=VMEM_LIMIT),
        name="merge",
    )(o, pu, pu, cv, cv, gl, x2, mod, wpool, pscale, convw, wbr_bf, wout_bf, g)


def _mlp_kernel(x_ref, mod_ref, gpre_ref, w1_ref, w2_ref, gpost_ref, out_ref):
    x = x_ref[...]
    shift, scale, gate = mod_ref[0, 0], mod_ref[0, 1], mod_ref[0, 2]
    hb = (_rms(x, gpre_ref[...]) * (1.0 + scale) + shift).astype(BF16)
    dff = w1_ref.shape[1]
    acc = jnp.zeros(x.shape, F32)
    for c in range(dff // FF_CHUNK):
        up = jnp.dot(hb, w1_ref[:, c * FF_CHUNK:(c + 1) * FF_CHUNK], preferred_element_type=F32)
        act = jnp.square(jnp.maximum(up, 0.0)).astype(BF16)
        acc += jnp.dot(act, w2_ref[c * FF_CHUNK:(c + 1) * FF_CHUNK, :], preferred_element_type=F32)
    out_ref[...] = x + gate * _rms(acc, gpost_ref[...])


def _mlp(x2, mod, gpre, w1_bf, w2_bf, gpost, seq):
    t, d = x2.shape
    nt = t // TML
    tiles_per_seq = seq // TML
    return pl.pallas_call(
        _mlp_kernel,
        grid=(nt,),
        in_specs=[
            pl.BlockSpec((TML, d), lambda i: (i, 0)),
            pl.BlockSpec((1, 3, 1, d), lambda i: (i // tiles_per_seq, 1, 0, 0)),
            _const_spec(gpre.shape),
            _const_spec(w1_bf.shape),
            _const_spec(w2_bf.shape),
            _const_spec(gpost.shape),
        ],
        out_specs=pl.BlockSpec((TML, d), lambda i: (i, 0)),
        out_shape=jax.ShapeDtypeStruct((t, d), F32),
        compiler_params=pltpu.CompilerParams(
            dimension_semantics=("arbitrary",), vmem_limit_bytes=VMEM_LIMIT),
        name="mlp",
    )(x2, mod, gpre, w1_bf, w2_bf, gpost)


def _block_diag(w):
    g, c, dd = w.shape
    eye = jnp.eye(g, dtype=w.dtype)
    return (eye[:, None, :, None] * w[:, :, None, :]).reshape(g * c, g * dd)


def kernel(x, c, w_ada, b_ada, g_mix_pre, g_mix_post, g_ff_pre, g_ff_post, w_in, b_f, w_pool, pool_scale,
           conv_w, w_branch, w_out, w_ff1, w_ff2):
    nb, seq, d = x.shape
    depth = w_ada.shape[0]
    assert all(seq % tile == 0 for tile in (TM, TMM, TML, TQ, TK)) and TMM % SUB == 0
    assert d % LANES == 0 and d % PREP_ROWS == 0
    t = nb * seq
    assert (6 * d) % (ADA_CHUNKS * LANES) == 0
    w_in_t = jnp.swapaxes(w_in, 1, 2)
    mod, w_in_bf = _ada(c, w_ada, b_ada, w_in_t)
    mod = mod.reshape(depth, nb, 6, 1, d)
    x2 = x.reshape(t, d)
    stacked = {"in": w_in_t, "branch": w_branch, "out": w_out, "ff1": w_ff1, "ff2": w_ff2}
    cast_keys = [("in", l) for l in range(1, depth)]
    cast_keys += [(name, l) for l in range(depth) for name in ("branch", "out", "ff1", "ff2")]
    casts = [(stacked[name], l, 0.5 if name == "branch" else 1.0) for name, l in cast_keys]
    bf16_w = {("in", 0): w_in_bf}
    spread = np.zeros((LANES, LANES), np.float32)
    head_of_lane = np.zeros((LANES,), np.int32)
    lane_used = np.zeros((LANES,), bool)
    for hd in range(HEADS):
        spread[_aug_base(hd):_aug_base(hd) + AUG, hd] = 1.0
        head_of_lane[_aug_base(hd):_aug_base(hd) + AUG] = hd
        lane_used[_aug_base(hd):_aug_base(hd) + AUG] = True
    spread = jnp.asarray(spread, BF16)
    bf_all = jnp.where(lane_used, b_f[:, head_of_lane], 0.0)
    for l in range(depth):
        bf = bf_all[l:l + 1]
        row = lambda a: a[l].reshape(1, -1)
        (q8, k8, v8, pu, cv, gl), cast = _inproj(x2, mod[l], row(g_mix_pre), bf16_w[("in", l)], spread, bf, seq,
                                                 casts if l == 0 else [])
        if l == 0:
            bf16_w.update(zip(cast_keys, cast))
        o = _attention(q8, k8, v8, seq)
        x2 = _merge(o, pu, cv, gl, x2, mod[l], _block_diag(w_pool[l]).astype(BF16), row(pool_scale),
                    conv_w[l], bf16_w[("branch", l)], bf16_w[("out", l)], row(g_mix_post), seq)
        x2 = _mlp(x2, mod[l], row(g_ff_pre), bf16_w[("ff1", l)], bf16_w[("ff2", l)], row(g_ff_post), seq)
    return x2.reshape(nb, seq, d)
```

```python
import functools

import jax
import jax.numpy as jnp
import numpy as np
from jax import lax
from jax.experimental import pallas as pl
from jax.experimental.pallas import tpu as pltpu

F32 = jnp.float32
BF16 = jnp.bfloat16

LANES = 128
HEAD_DIM = 64
HEADS = 8
PAIRS = HEADS // 2
A_WIDTH = HEADS * HEAD_DIM
POOL_WINDOWS = (2, 4, 8, 16)
POOL_WIDTH = 256
CONV_WIDTH = 256
HALO = 16
RMS_EPS = 1e-6
NEG_INF = -1e30
LOG2E = 1.4426950408889634
AUG = 6
PV_ROWS = 128
ADA_CHUNKS = 8
VMEM_LIMIT = 56 * 1024 * 1024

TM = 512
TMT = 512
TQ = 512
TK = 512
FF_CHUNK = 1024
SUB = 256
PREP_ROWS = 128


def _const_spec(shape):
    n = len(shape)
    return pl.BlockSpec(shape, lambda *_: (0,) * n, pipeline_mode=pl.Buffered(1))


def _rms(x, g):
    ms = jnp.mean(x * x, axis=-1, keepdims=True)
    return x * lax.rsqrt(ms + RMS_EPS) * g


def _row_chunks(k, steps):
    rows = -(-(-(-k // steps)) // 16) * 16
    return rows, -(-k // rows)


def _nt_dot(a, b):
    return lax.dot_general(a, b, (((1,), (1,)), ((), ())), preferred_element_type=F32)


def _ada_kernel(c_ref, w_ref, b_ref, win_ref, o_ref, win_bf_ref):
    c = c_ref[...]
    sc = c * (1.0 / (1.0 + jnp.exp(-c)))
    o_ref[0] = jnp.dot(sc, w_ref[0], precision=lax.Precision.HIGHEST,
                       preferred_element_type=F32) + b_ref[0]
    win_bf_ref[...] = win_ref[0].astype(BF16)


def _ada(c, w_ada, b_ada, w_in_t):
    depth, d, d6 = w_ada.shape
    nb = c.shape[0]
    n = ADA_CHUNKS
    cols = d6 // n
    rows, nblk = _row_chunks(w_in_t.shape[1], depth * n)
    assert nblk == depth * n
    ncol_in = w_in_t.shape[2]
    return pl.pallas_call(
        _ada_kernel,
        grid=(depth, n),
        in_specs=[
            pl.BlockSpec((nb, d), lambda l, j: (0, 0)),
            pl.BlockSpec((1, d, cols), lambda l, j: (l, 0, j)),
            pl.BlockSpec((1, 1, cols), lambda l, j: (l, 0, j)),
            pl.BlockSpec((1, rows, ncol_in), lambda l, j: (0, l * n + j, 0)),
        ],
        out_specs=(
            pl.BlockSpec((1, nb, cols), lambda l, j: (l, 0, j)),
            pl.BlockSpec((rows, ncol_in), lambda l, j: (l * n + j, 0)),
        ),
        out_shape=(
            jax.ShapeDtypeStruct((depth, nb, d6), F32),
            jax.ShapeDtypeStruct(w_in_t.shape[1:], BF16),
        ),
        compiler_params=pltpu.CompilerParams(
            dimension_semantics=("arbitrary", "arbitrary"), vmem_limit_bytes=VMEM_LIMIT),
        name="ada",
    )(c, w_ada, b_ada.reshape(depth, 1, d6), w_in_t)


def _aug_base(h):
    return (HEAD_DIM if h % 2 == 0 else 0) + AUG * (h // 2)


def _inproj_kernel(tiles_per_seq, cast_scales, x_ref, mod_ref, g_ref, w_ref, spread_ref, bf_ref, *refs):
    n_cast = len(cast_scales)
    cast_in = refs[:n_cast]
    q_ref, k_ref, v_ref, pu_ref, cv_ref, gl_ref = refs[n_cast:n_cast + 6]
    cast_out = refs[n_cast + 6:2 * n_cast + 6]
    carry_ref, wf_ref, wrest_ref = refs[2 * n_cast + 6:]
    i = pl.program_id(0)
    d = x_ref.shape[1]

    for src, dst, s in zip(cast_in, cast_out, cast_scales):
        dst[...] = (src[0] if s == 1.0 else src[0] * s).astype(BF16)

    fo = 3 * A_WIDTH
    n_plain = POOL_WIDTH + 3 * CONV_WIDTH
    n_rest = wrest_ref.shape[0]

    @pl.when(i == 0)
    def _():
        wf_ref[...] = jnp.dot(spread_ref[...], w_ref[fo:fo + LANES, :],
                              preferred_element_type=F32).astype(BF16)
        for r in range(0, n_rest, PREP_ROWS):
            end = min(fo + r + PREP_ROWS + 2 * HEADS, w_ref.shape[0])
            blk = w_ref[fo + r:end, :].astype(F32)[HEADS:HEADS + PREP_ROWS]
            wrest_ref[r:r + PREP_ROWS, :] = (blk if r < n_plain else blk * 0.5).astype(BF16)

    shift, scale = mod_ref[0, 0], mod_ref[0, 1]
    hb = (_rms(x_ref[...], g_ref[...]) * (1.0 + scale) + shift).astype(BF16)
    tm = hb.shape[0]

    def proj(lo_, hi_):
        return _nt_dot(hb, w_ref[lo_:hi_, :])

    def rest(lo_, hi_):
        return _nt_dot(hb, wrest_ref[lo_:hi_, :])

    zf = _nt_dot(hb, wf_ref[...]) + bf_ref[...]
    pu_ref[...] = rest(0, POOL_WIDTH).astype(BF16)
    cv_ref[...] = rest(POOL_WIDTH, n_plain).astype(BF16)
    for jj in range(3):
        gl_ref[:, jj * d:(jj + 1) * d] = rest(n_plain + jj * d, n_plain + (jj + 1) * d).astype(BF16)

    lf = jnp.minimum(zf, 0.0) - jnp.log(1.0 + jnp.exp(-jnp.abs(zf)))
    row = lax.broadcasted_iota(jnp.int32, lf.shape, 0)
    step = 1
    while step < tm:
        lf = lf + jnp.where(row >= step, pltpu.roll(lf, step, 0), 0.0)
        step *= 2

    @pl.when(i % tiles_per_seq == 0)
    def _():
        carry_ref[...] = jnp.zeros_like(carry_ref)

    fc = lf + carry_ref[0:1, :]
    carry_ref[...] = jnp.broadcast_to(fc[tm - 1:tm, :], carry_ref.shape)

    f2 = fc * LOG2E
    hi = f2.astype(BF16).astype(F32)
    rem = f2 - hi
    mid = rem.astype(BF16).astype(F32)
    lo = rem - mid
    lane = lax.broadcasted_iota(jnp.int32, f2.shape, 1)
    j = (lane % HEAD_DIM) % AUG
    part = jnp.where(j % 3 == 0, hi, jnp.where(j % 3 == 1, mid, lo))
    aug_q = jnp.where(j < 3, 1.0, part)
    aug_k = jnp.where(j < 3, -part, 1.0)

    zq = proj(0, A_WIDTH) * (HEAD_DIM ** -0.5 * LOG2E)
    zk = proj(A_WIDTH, 2 * A_WIDTH)
    zv = proj(2 * A_WIDTH, 3 * A_WIDTH)
    for hd in range(HEADS):
        p = hd // 2
        in_head = (lane >= (hd % 2) * HEAD_DIM) & (lane < (hd % 2 + 1) * HEAD_DIM)
        base = _aug_base(hd)
        in_aug = (lane >= base) & (lane < base + AUG)
        pair = slice(p * LANES, (p + 1) * LANES)
        q_ref[hd] = jnp.where(in_head, zq[:, pair], jnp.where(in_aug, aug_q, 0.0)).astype(BF16)
        k_ref[hd] = jnp.where(in_head, zk[:, pair], jnp.where(in_aug, aug_k, 0.0)).astype(BF16)
        ones_lane = HEAD_DIM if hd % 2 == 0 else HEAD_DIM - 1
        v_ref[hd] = jnp.where(in_head, zv[:, pair], jnp.where(lane == ones_lane, 1.0, 0.0)).astype(BF16)


def _inproj(x2, mod, g, w_in_bf, spread, bf, seq, casts):
    t, d = x2.shape
    nt = t // TM
    tiles_per_seq = seq // TM
    n_rest = w_in_bf.shape[0] - 3 * A_WIDTH - HEADS
    cast_in_specs, cast_out_specs, cast_shapes = [], [], []
    for w, l, _ in casts:
        _, k, n = w.shape
        rows, nblk = _row_chunks(k, nt)
        cast_in_specs.append(pl.BlockSpec(
            (1, rows, n), functools.partial(lambda l_, nb_, i: (l_, jnp.minimum(i, nb_ - 1), 0), l, nblk)))
        cast_out_specs.append(pl.BlockSpec(
            (rows, n), functools.partial(lambda nb_, i: (jnp.minimum(i, nb_ - 1), 0), nblk)))
        cast_shapes.append(jax.ShapeDtypeStruct((k, n), BF16))
    out_shape = (
        jax.ShapeDtypeStruct((HEADS, t, LANES), BF16),
        jax.ShapeDtypeStruct((HEADS, t, LANES), BF16),
        jax.ShapeDtypeStruct((HEADS, t, LANES), BF16),
        jax.ShapeDtypeStruct((t, POOL_WIDTH), BF16),
        jax.ShapeDtypeStruct((t, 3 * CONV_WIDTH), BF16),
        jax.ShapeDtypeStruct((t, 3 * d), BF16),
    )
    head_spec = pl.BlockSpec((HEADS, TM, LANES), lambda i: (0, i, 0))
    out_specs = (
        head_spec, head_spec, head_spec,
        pl.BlockSpec((TM, POOL_WIDTH), lambda i: (i, 0)),
        pl.BlockSpec((TM, 3 * CONV_WIDTH), lambda i: (i, 0)),
        pl.BlockSpec((TM, 3 * d), lambda i: (i, 0)),
    )
    outs = pl.pallas_call(
        functools.partial(_inproj_kernel, tiles_per_seq, tuple(s for _, _, s in casts)),
        grid=(nt,),
        in_specs=[
            pl.BlockSpec((TM, d), lambda i: (i, 0)),
            pl.BlockSpec((1, 2, 1, d), lambda i: (i // tiles_per_seq, 0, 0, 0)),
            _const_spec((1, d)),
            _const_spec(w_in_bf.shape),
            _const_spec(spread.shape),
            _const_spec((1, LANES)),
        ] + cast_in_specs,
        out_specs=out_specs + tuple(cast_out_specs),
        out_shape=out_shape + tuple(cast_shapes),
        scratch_shapes=[
            pltpu.VMEM((8, LANES), F32),
            pltpu.VMEM((LANES, d), BF16),
            pltpu.VMEM((n_rest, d), BF16),
        ],
        compiler_params=pltpu.CompilerParams(
            dimension_semantics=("arbitrary",), vmem_limit_bytes=VMEM_LIMIT),
        name="inproj",
    )(x2, mod, g, w_in_bf, spread, bf, *[w for w, _, _ in casts])
    return outs[:6], outs[6:]


def _attn_kernel(q_ref, k_ref, v_ref, o_ref):
    seq = q_ref.shape[1]
    nq = seq // TQ
    half = TK // 2
    steps = [(qi, kt) for qi in range(nq) for kt in range(qi + 1)]
    mask0 = (lax.broadcasted_iota(jnp.int32, (half, TQ), 0)
             <= lax.broadcasted_iota(jnp.int32, (half, TQ), 1))
    mask1 = (lax.broadcasted_iota(jnp.int32, (half, half), 0)
             <= lax.broadcasted_iota(jnp.int32, (half, half), 1))
    vts = [v_ref[hh].astype(F32).T.astype(BF16)[hh * (LANES - PV_ROWS):hh * (LANES - PV_ROWS) + PV_ROWS]
           for hh in range(2)]

    def logits(step, hh):
        qi, kt = step
        q = q_ref[hh, qi * TQ:(qi + 1) * TQ, :]
        if kt < qi:
            st = _nt_dot(k_ref[hh, kt * TK:(kt + 1) * TK, :], q)
            return (st,), jnp.max(st, axis=0, keepdims=True)
        st0 = jnp.where(mask0, _nt_dot(k_ref[hh, kt * TK:kt * TK + half, :], q), NEG_INF)
        st1 = jnp.where(mask1, _nt_dot(k_ref[hh, kt * TK + half:(kt + 1) * TK, :], q[half:]), NEG_INF)
        c0 = jnp.max(st0, axis=0, keepdims=True)
        c1 = jnp.max(st1, axis=0, keepdims=True)
        return (st0, st1), jnp.concatenate([c0[:, :half], jnp.maximum(c0[:, half:], c1)], axis=1)

    def update(step, hh, sts, cmax, m, acc):
        qi, kt = step
        vt = vts[hh]
        m_new = cmax if m is None else jnp.maximum(m, cmax)
        if kt < qi:
            pt = jnp.exp2(sts[0] - m_new).astype(BF16)
            pv = jnp.dot(vt[:, kt * TK:(kt + 1) * TK], pt, preferred_element_type=F32)
        else:
            pt0 = jnp.exp2(sts[0] - m_new).astype(BF16)
            pt1 = jnp.exp2(sts[1] - m_new[:, half:]).astype(BF16)
            pv = jnp.dot(vt[:, kt * TK:kt * TK + half], pt0, preferred_element_type=F32)
            pv1 = jnp.dot(vt[:, kt * TK + half:(kt + 1) * TK], pt1, preferred_element_type=F32)
            pv = jnp.concatenate([pv[:, :half], pv[:, half:] + pv1], axis=1)
        if m is not None:
            pv = jnp.exp2(m - m_new) * acc + pv
        return m_new, pv

    cur = [logits(steps[0], hh) for hh in range(2)]
    state = [(None, None), (None, None)]
    for s, step in enumerate(steps):
        qi, kt = step
        nxt = [None, None]
        for hh in range(2):
            if s + 1 < len(steps):
                nxt[hh] = logits(steps[s + 1], hh)
            state[hh] = update(step, hh, *cur[hh], *state[hh])
        cur = nxt
        if kt == qi:
            a0, a1 = state[0][1], state[1][1]
            pad = PV_ROWS - HEAD_DIM
            out = jnp.concatenate([a0[0:HEAD_DIM] / a0[HEAD_DIM:HEAD_DIM + 1],
                                   a1[pad:] / a1[pad - 1:pad]], axis=0)
            o_ref[qi * TQ:(qi + 1) * TQ, :] = out.T.astype(BF16)
            state = [(None, None), (None, None)]


def _attention(q8, k8, v8, seq):
    _, t, _ = q8.shape
    nb = t // seq
    pair_spec = pl.BlockSpec((2, seq, LANES), lambda b, p: (p, b, 0))
    return pl.pallas_call(
        _attn_kernel,
        grid=(nb, PAIRS),
        in_specs=[pair_spec, pair_spec, pair_spec],
        out_specs=pl.BlockSpec((seq, LANES), lambda b, p: (b, p)),
        out_shape=jax.ShapeDtypeStruct((t, A_WIDTH), BF16),
        compiler_params=pltpu.CompilerParams(
            dimension_semantics=("arbitrary", "arbitrary"), vmem_limit_bytes=VMEM_LIMIT),
        name="attn",
    )(q8, k8, v8)


def _tail_kernel(tiles_per_seq, o_ref, pu_ref, puh_ref, cv_ref, cvh_ref, gl_ref, x_ref, mod_ref,
                 wpool_ref, pscale_ref, convw_ref, wbr_ref, wout_ref, g_ref,
                 gpre_ref, w1_ref, w2_ref, gpost_ref, out_ref, x1_ref):
    i = pl.program_id(0)
    gate_m, shift_f, scale_f, gate_f = mod_ref[0, 2], mod_ref[0, 3], mod_ref[0, 4], mod_ref[0, 5]
    tile_in_seq = i % tiles_per_seq
    tm, d = x_ref.shape
    n_sub = tm // SUB
    lane = lax.broadcasted_iota(jnp.int32, (SUB + HALO, POOL_WIDTH), 1)
    row = lax.broadcasted_iota(jnp.int32, (SUB + HALO, POOL_WIDTH), 0)
    gd = POOL_WIDTH // len(POOL_WINDOWS)
    wsz = jnp.where(lane < gd, 2.0, jnp.where(lane < 2 * gd, 4.0, jnp.where(lane < 3 * gd, 8.0, 16.0)))
    cw = convw_ref[...]
    o1 = A_WIDTH + POOL_WIDTH

    def with_history(ref, halo_ref, j):
        if j == 0:
            head = jnp.where(tile_in_seq != 0, halo_ref[...].astype(F32), 0.0)
        else:
            head = ref[j * SUB - HALO:j * SUB, :].astype(F32)
        return jnp.concatenate([head, ref[j * SUB:(j + 1) * SUB, :].astype(F32)], axis=0)

    def mix(j):
        rows = slice(j * SUB, (j + 1) * SUB)
        ext = with_history(pu_ref, puh_ref, j)
        a2 = ext + pltpu.roll(ext, 1, 0)
        a4 = a2 + pltpu.roll(a2, 2, 0)
        a8 = a4 + pltpu.roll(a4, 4, 0)
        a16 = a8 + pltpu.roll(a8, 8, 0)
        win = jnp.where(lane < gd, a2, jnp.where(lane < 2 * gd, a4, jnp.where(lane < 3 * gd, a8, a16)))
        frames = (tile_in_seq * tm + j * SUB + row - (HALO - 1)).astype(F32)
        cnt = jnp.maximum(jnp.minimum(frames, wsz), 1.0)
        pm = (win / cnt - ext)[HALO:]
        br_b = jnp.dot(pm.astype(BF16), wpool_ref[...], preferred_element_type=F32) * pscale_ref[...]

        cve = with_history(cv_ref, cvh_ref, j)
        u = cve[:, 2 * CONV_WIDTH:] * cve[:, :CONV_WIDTH]
        y = cw[0:1] * pltpu.roll(u, 2, 0) + cw[1:2] * pltpu.roll(u, 1, 0) + cw[2:3] * u
        br_c = (cve[:, CONV_WIDTH:2 * CONV_WIDTH] * y)[HALO:]

        def gated(n, branch, lo, hi):
            b = jnp.dot(branch, wbr_ref[lo:hi, :], preferred_element_type=F32)
            return b + b * jnp.tanh(gl_ref[rows, n * d:(n + 1) * d].astype(F32))

        merged = gated(0, o_ref[rows, :], 0, A_WIDTH)
        merged += gated(1, br_b.astype(BF16), A_WIDTH, o1)
        merged += gated(2, br_c.astype(BF16), o1, o1 + CONV_WIDTH)
        return merged.astype(BF16)

    def project(j, merged):
        rows = slice(j * SUB, (j + 1) * SUB)
        yo = jnp.dot(merged, wout_ref[...], preferred_element_type=F32)
        x1_ref[rows, :] = x_ref[rows, :] + gate_m * _rms(yo, g_ref[...])

    cur = mix(0)
    for j in range(n_sub):
        nxt = mix(j + 1) if j + 1 < n_sub else None
        project(j, cur)
        cur = nxt

    x1 = x1_ref[...]
    hb = (_rms(x1, gpre_ref[...]) * (1.0 + scale_f) + shift_f).astype(BF16)
    dff = w1_ref.shape[1]
    acc = jnp.zeros(x1.shape, F32)
    for c in range(dff // FF_CHUNK):
        up = jnp.dot(hb, w1_ref[:, c * FF_CHUNK:(c + 1) * FF_CHUNK], preferred_element_type=F32)
        act = jnp.square(jnp.maximum(up, 0.0)).astype(BF16)
        acc += jnp.dot(act, w2_ref[c * FF_CHUNK:(c + 1) * FF_CHUNK, :], preferred_element_type=F32)
    out_ref[...] = x1 + gate_f * _rms(acc, gpost_ref[...])


def _tail(o, pu, cv, gl, x2, mod, wpool, pscale, convw, wbr_bf, wout_bf, g_mix_post, g_ff_pre, w1_bf, w2_bf,
          g_ff_post, seq):
    t, d = x2.shape
    nt = t // TMT
    tiles_per_seq = seq // TMT
    hb = TMT // HALO

    def halo_map(i):
        return (jnp.maximum(i * hb - 1, 0), 0)

    consts = (wpool, pscale, convw, wbr_bf, wout_bf, g_mix_post, g_ff_pre, w1_bf, w2_bf, g_ff_post)
    return pl.pallas_call(
        functools.partial(_tail_kernel, tiles_per_seq),
        grid=(nt,),
        in_specs=[
            pl.BlockSpec((TMT, A_WIDTH), lambda i: (i, 0)),
            pl.BlockSpec((TMT, POOL_WIDTH), lambda i: (i, 0)),
            pl.BlockSpec((HALO, POOL_WIDTH), halo_map),
            pl.BlockSpec((TMT, 3 * CONV_WIDTH), lambda i: (i, 0)),
            pl.BlockSpec((HALO, 3 * CONV_WIDTH), halo_map),
            pl.BlockSpec((TMT, 3 * d), lambda i: (i, 0)),
            pl.BlockSpec((TMT, d), lambda i: (i, 0)),
            pl.BlockSpec((1,) + mod.shape[1:], lambda i: (i // tiles_per_seq, 0, 0, 0)),
        ] + [_const_spec(a.shape) for a in consts],
        out_specs=pl.BlockSpec((TMT, d), lambda i: (i, 0)),
        out_shape=jax.ShapeDtypeStruct((t, d), F32),
        scratch_shapes=[pltpu.VMEM((TMT, d), F32)],
        compiler_params=pltpu.CompilerParams(
            dimension_semantics=("arbitrary",), vmem_limit_bytes=VMEM_LIMIT),
        name="tail",
    )(o, pu, pu, cv, cv, gl, x2, mod, *consts)


def _block_diag(w):
    g, c, dd = w.shape
    eye = jnp.eye(g, dtype=w.dtype)
    return (eye[:, None, :, None] * w[:, :, None, :]).reshape(g * c, g * dd)


def kernel(x, c, w_ada, b_ada, g_mix_pre, g_mix_post, g_ff_pre, g_ff_post, w_in, b_f, w_pool, pool_scale,
           conv_w, w_branch, w_out, w_ff1, w_ff2):
    nb, seq, d = x.shape
    depth = w_ada.shape[0]
    assert all(seq % tile == 0 for tile in (TM, TMT, TQ, TK)) and TMT % SUB == 0
    assert d % LANES == 0 and d % PREP_ROWS == 0
    t = nb * seq
    assert (6 * d) % (ADA_CHUNKS * LANES) == 0
    w_in_t = jnp.swapaxes(w_in, 1, 2)
    mod, w_in_bf = _ada(c, w_ada, b_ada, w_in_t)
    mod = mod.reshape(depth, nb, 6, 1, d)
    x2 = x.reshape(t, d)
    stacked = {"in": w_in_t, "branch": w_branch, "out": w_out, "ff1": w_ff1, "ff2": w_ff2}
    cast_keys = [("in", l) for l in range(1, depth)]
    cast_keys += [(name, l) for l in range(depth) for name in ("branch", "out", "ff1", "ff2")]
    casts = [(stacked[name], l, 0.5 if name == "branch" else 1.0) for name, l in cast_keys]
    bf16_w = {("in", 0): w_in_bf}
    spread = np.zeros((LANES, LANES), np.float32)
    head_of_lane = np.zeros((LANES,), np.int32)
    lane_used = np.zeros((LANES,), bool)
    for hd in range(HEADS):
        spread[_aug_base(hd):_aug_base(hd) + AUG, hd] = 1.0
        head_of_lane[_aug_base(hd):_aug_base(hd) + AUG] = hd
        lane_used[_aug_base(hd):_aug_base(hd) + AUG] = True
    spread = jnp.asarray(spread, BF16)
    bf_all = jnp.where(lane_used, b_f[:, head_of_lane], 0.0)
    for l in range(depth):
        bf = bf_all[l:l + 1]
        row = lambda a: a[l].reshape(1, -1)
        (q8, k8, v8, pu, cv, gl), cast = _inproj(x2, mod[l], row(g_mix_pre), bf16_w[("in", l)], spread, bf, seq,
                                                 casts if l == 0 else [])
        if l == 0:
            bf16_w.update(zip(cast_keys, cast))
        o = _attention(q8, k8, v8, seq)
        x2 = _tail(o, pu, cv, gl, x2, mod[l], _block_diag(w_pool[l]).astype(BF16), row(pool_scale),
                   conv_w[l], bf16_w[("branch", l)], bf16_w[("out", l)], row(g_mix_post), row(g_ff_pre),
                   bf16_w[("ff1", l)], bf16_w[("ff2", l)], row(g_ff_post), seq)
    return x2.reshape(nb, seq, d)
```

```python
import functools

import jax
import jax.numpy as jnp
import numpy as np
from jax import lax
from jax.experimental import pallas as pl
from jax.experimental.pallas import tpu as pltpu

F32 = jnp.float32
BF16 = jnp.bfloat16

LANES = 128
HEAD_DIM = 64
HEADS = 8
PAIRS = HEADS // 2
A_WIDTH = HEADS * HEAD_DIM
POOL_WINDOWS = (2, 4, 8, 16)
POOL_WIDTH = 256
CONV_WIDTH = 256
HALO = 16
RMS_EPS = 1e-6
NEG_INF = -1e30
LOG2E = 1.4426950408889634
AUG = 6
PV_ROWS = 128
ADA_CHUNKS = 8
VMEM_LIMIT = 56 * 1024 * 1024

TM = 512
TQ = 512
TK = 512
FF_CHUNK = 1024
SUB = 256
PREP_ROWS = 128


def _const_spec(shape):
    n = len(shape)
    return pl.BlockSpec(shape, lambda *_: (0,) * n, pipeline_mode=pl.Buffered(1))


def _rms(x, g):
    ms = jnp.mean(x * x, axis=-1, keepdims=True)
    return x * lax.rsqrt(ms + RMS_EPS) * g


def _row_chunks(k, steps):
    rows = -(-(-(-k // steps)) // 16) * 16
    return rows, -(-k // rows)


def _nt_dot(a, b):
    return lax.dot_general(a, b, (((1,), (1,)), ((), ())), preferred_element_type=F32)


def _ada_kernel(c_ref, w_ref, b_ref, win_ref, o_ref, win_bf_ref):
    c = c_ref[...]
    sc = c * (1.0 / (1.0 + jnp.exp(-c)))
    w = w_ref[0]
    sc_hi, w_hi = sc.astype(BF16), w.astype(BF16)
    sc_lo = (sc - sc_hi.astype(F32)).astype(BF16)
    w_lo = (w - w_hi.astype(F32)).astype(BF16)
    dot = functools.partial(jnp.dot, preferred_element_type=F32)
    o_ref[0] = dot(sc_hi, w_hi) + dot(sc_hi, w_lo) + dot(sc_lo, w_hi) + b_ref[0]
    win_bf_ref[...] = win_ref[0].astype(BF16)


def _ada(c, w_ada, b_ada, w_in_t):
    depth, d, d6 = w_ada.shape
    nb = c.shape[0]
    n = ADA_CHUNKS
    cols = d6 // n
    rows, nblk = _row_chunks(w_in_t.shape[1], depth * n)
    assert nblk == depth * n
    ncol_in = w_in_t.shape[2]
    return pl.pallas_call(
        _ada_kernel,
        grid=(depth, n),
        in_specs=[
            pl.BlockSpec((nb, d), lambda l, j: (0, 0)),
            pl.BlockSpec((1, d, cols), lambda l, j: (l, 0, j)),
            pl.BlockSpec((1, 1, cols), lambda l, j: (l, 0, j)),
            pl.BlockSpec((1, rows, ncol_in), lambda l, j: (0, l * n + j, 0)),
        ],
        out_specs=(
            pl.BlockSpec((1, nb, cols), lambda l, j: (l, 0, j)),
            pl.BlockSpec((rows, ncol_in), lambda l, j: (l * n + j, 0)),
        ),
        out_shape=(
            jax.ShapeDtypeStruct((depth, nb, d6), F32),
            jax.ShapeDtypeStruct(w_in_t.shape[1:], BF16),
        ),
        compiler_params=pltpu.CompilerParams(
            dimension_semantics=("arbitrary", "arbitrary"), vmem_limit_bytes=VMEM_LIMIT),
        name="ada",
    )(c, w_ada, b_ada.reshape(depth, 1, d6), w_in_t)


def _aug_base(h):
    return (HEAD_DIM if h % 2 == 0 else 0) + AUG * (h // 2)


def _inproj_kernel(tiles_per_seq, cast_scales, x_ref, mod_ref, g_ref, w_ref, spread_ref, bf_ref, *refs):
    n_cast = len(cast_scales)
    cast_in = refs[:n_cast]
    q_ref, k_ref, v_ref, pu_ref, cv_ref, gl_ref = refs[n_cast:n_cast + 6]
    cast_out = refs[n_cast + 6:2 * n_cast + 6]
    carry_ref, wf_ref, wrest_ref = refs[2 * n_cast + 6:]
    i = pl.program_id(0)
    d = x_ref.shape[1]

    for src, dst, s in zip(cast_in, cast_out, cast_scales):
        dst[...] = (src[0] if s == 1.0 else src[0] * s).astype(BF16)

    fo = 3 * A_WIDTH
    n_plain = POOL_WIDTH + 3 * CONV_WIDTH
    n_rest = wrest_ref.shape[0]

    @pl.when(i == 0)
    def _():
        wf_ref[...] = jnp.dot(spread_ref[...], w_ref[fo:fo + LANES, :],
                              preferred_element_type=F32).astype(BF16)
        for r in range(0, n_rest, PREP_ROWS):
            end = min(fo + r + PREP_ROWS + 2 * HEADS, w_ref.shape[0])
            blk = w_ref[fo + r:end, :].astype(F32)[HEADS:HEADS + PREP_ROWS]
            wrest_ref[r:r + PREP_ROWS, :] = (blk if r < n_plain else blk * 0.5).astype(BF16)

    shift, scale = mod_ref[0, 0], mod_ref[0, 1]
    hb = (_rms(x_ref[...], g_ref[...]) * (1.0 + scale) + shift).astype(BF16)
    tm = hb.shape[0]

    def proj(lo_, hi_):
        return _nt_dot(hb, w_ref[lo_:hi_, :])

    def rest(lo_, hi_):
        return _nt_dot(hb, wrest_ref[lo_:hi_, :])

    zf = _nt_dot(hb, wf_ref[...]) + bf_ref[...]
    pu_ref[...] = rest(0, POOL_WIDTH).astype(BF16)
    cv_ref[...] = rest(POOL_WIDTH, n_plain).astype(BF16)
    for jj in range(3):
        gl_ref[:, jj * d:(jj + 1) * d] = rest(n_plain + jj * d, n_plain + (jj + 1) * d).astype(BF16)

    lf = jnp.minimum(zf, 0.0) - jnp.log(1.0 + jnp.exp(-jnp.abs(zf)))
    row = lax.broadcasted_iota(jnp.int32, lf.shape, 0)
    step = 1
    while step < tm:
        lf = lf + jnp.where(row >= step, pltpu.roll(lf, step, 0), 0.0)
        step *= 2

    @pl.when(i % tiles_per_seq == 0)
    def _():
        carry_ref[...] = jnp.zeros_like(carry_ref)

    fc = lf + carry_ref[0:1, :]
    carry_ref[...] = jnp.broadcast_to(fc[tm - 1:tm, :], carry_ref.shape)

    f2 = fc * LOG2E
    hi = f2.astype(BF16).astype(F32)
    rem = f2 - hi
    mid = rem.astype(BF16).astype(F32)
    lo = rem - mid
    lane = lax.broadcasted_iota(jnp.int32, f2.shape, 1)
    j = (lane % HEAD_DIM) % AUG
    part = jnp.where(j % 3 == 0, hi, jnp.where(j % 3 == 1, mid, lo))
    aug_q = jnp.where(j < 3, 1.0, part)
    aug_k = jnp.where(j < 3, -part, 1.0)

    zq = proj(0, A_WIDTH) * (HEAD_DIM ** -0.5 * LOG2E)
    zk = proj(A_WIDTH, 2 * A_WIDTH)
    zv = proj(2 * A_WIDTH, 3 * A_WIDTH)
    for hd in range(HEADS):
        p = hd // 2
        in_head = (lane >= (hd % 2) * HEAD_DIM) & (lane < (hd % 2 + 1) * HEAD_DIM)
        base = _aug_base(hd)
        in_aug = (lane >= base) & (lane < base + AUG)
        pair = slice(p * LANES, (p + 1) * LANES)
        q_ref[hd] = jnp.where(in_head, zq[:, pair], jnp.where(in_aug, aug_q, 0.0)).astype(BF16)
        k_ref[hd] = jnp.where(in_head, zk[:, pair], jnp.where(in_aug, aug_k, 0.0)).astype(BF16)
        ones_lane = HEAD_DIM if hd % 2 == 0 else HEAD_DIM - 1
        v_ref[hd] = jnp.where(in_head, zv[:, pair], jnp.where(lane == ones_lane, 1.0, 0.0)).astype(BF16)


def _inproj(x2, mod, g, w_in_bf, spread, bf, seq, casts):
    t, d = x2.shape
    nt = t // TM
    tiles_per_seq = seq // TM
    n_rest = w_in_bf.shape[0] - 3 * A_WIDTH - HEADS
    cast_in_specs, cast_out_specs, cast_shapes = [], [], []
    for w, l, _ in casts:
        _, k, n = w.shape
        rows, nblk = _row_chunks(k, nt)
        cast_in_specs.append(pl.BlockSpec(
            (1, rows, n), functools.partial(lambda l_, nb_, i: (l_, jnp.minimum(i, nb_ - 1), 0), l, nblk)))
        cast_out_specs.append(pl.BlockSpec(
            (rows, n), functools.partial(lambda nb_, i: (jnp.minimum(i, nb_ - 1), 0), nblk)))
        cast_shapes.append(jax.ShapeDtypeStruct((k, n), BF16))
    out_shape = (
        jax.ShapeDtypeStruct((HEADS, t, LANES), BF16),
        jax.ShapeDtypeStruct((HEADS, t, LANES), BF16),
        jax.ShapeDtypeStruct((HEADS, t, LANES), BF16),
        jax.ShapeDtypeStruct((t, POOL_WIDTH), BF16),
        jax.ShapeDtypeStruct((t, 3 * CONV_WIDTH), BF16),
        jax.ShapeDtypeStruct((t, 3 * d), BF16),
    )
    head_spec = pl.BlockSpec((HEADS, TM, LANES), lambda i: (0, i, 0))
    out_specs = (
        head_spec, head_spec, head_spec,
        pl.BlockSpec((TM, POOL_WIDTH), lambda i: (i, 0)),
        pl.BlockSpec((TM, 3 * CONV_WIDTH), lambda i: (i, 0)),
        pl.BlockSpec((TM, 3 * d), lambda i: (i, 0)),
    )
    outs = pl.pallas_call(
        functools.partial(_inproj_kernel, tiles_per_seq, tuple(s for _, _, s in casts)),
        grid=(nt,),
        in_specs=[
            pl.BlockSpec((TM, d), lambda i: (i, 0)),
            pl.BlockSpec((1, 2, 1, d), lambda i: (i // tiles_per_seq, 0, 0, 0)),
            _const_spec((1, d)),
            _const_spec(w_in_bf.shape),
            _const_spec(spread.shape),
            _const_spec((1, LANES)),
        ] + cast_in_specs,
        out_specs=out_specs + tuple(cast_out_specs),
        out_shape=out_shape + tuple(cast_shapes),
        scratch_shapes=[
            pltpu.VMEM((8, LANES), F32),
            pltpu.VMEM((LANES, d), BF16),
            pltpu.VMEM((n_rest, d), BF16),
        ],
        compiler_params=pltpu.CompilerParams(
            dimension_semantics=("arbitrary",), vmem_limit_bytes=VMEM_LIMIT),
        name="inproj",
    )(x2, mod, g, w_in_bf, spread, bf, *[w for w, _, _ in casts])
    return outs[:6], outs[6:]


def _attn_kernel(q_ref, k_ref, v_ref, o_ref):
    seq = q_ref.shape[1]
    nq = seq // TQ
    half = TK // 2
    steps = [(qi, kt) for qi in range(nq) for kt in range(qi + 1)]
    mask0 = (lax.broadcasted_iota(jnp.int32, (half, TQ), 0)
             <= lax.broadcasted_iota(jnp.int32, (half, TQ), 1))
    mask1 = (lax.broadcasted_iota(jnp.int32, (half, half), 0)
             <= lax.broadcasted_iota(jnp.int32, (half, half), 1))
    vts = [v_ref[hh].astype(F32).T.astype(BF16)[hh * (LANES - PV_ROWS):hh * (LANES - PV_ROWS) + PV_ROWS]
           for hh in range(2)]

    def logits(step, hh):
        qi, kt = step
        q = q_ref[hh, qi * TQ:(qi + 1) * TQ, :]
        if kt < qi:
            st = _nt_dot(k_ref[hh, kt * TK:(kt + 1) * TK, :], q)
            return (st,), jnp.max(st, axis=0, keepdims=True)
        st0 = jnp.where(mask0, _nt_dot(k_ref[hh, kt * TK:kt * TK + half, :], q), NEG_INF)
        st1 = jnp.where(mask1, _nt_dot(k_ref[hh, kt * TK + half:(kt + 1) * TK, :], q[half:]), NEG_INF)
        c0 = jnp.max(st0, axis=0, keepdims=True)
        c1 = jnp.max(st1, axis=0, keepdims=True)
        return (st0, st1), jnp.concatenate([c0[:, :half], jnp.maximum(c0[:, half:], c1)], axis=1)

    def update(step, hh, sts, cmax, m, acc):
        qi, kt = step
        vt = vts[hh]
        m_new = cmax if m is None else jnp.maximum(m, cmax)
        if kt < qi:
            pt = jnp.exp2(sts[0] - m_new).astype(BF16)
            pv = jnp.dot(vt[:, kt * TK:(kt + 1) * TK], pt, preferred_element_type=F32)
        else:
            pt0 = jnp.exp2(sts[0] - m_new).astype(BF16)
            pt1 = jnp.exp2(sts[1] - m_new[:, half:]).astype(BF16)
            pv = jnp.dot(vt[:, kt * TK:kt * TK + half], pt0, preferred_element_type=F32)
            pv1 = jnp.dot(vt[:, kt * TK + half:(kt + 1) * TK], pt1, preferred_element_type=F32)
            pv = jnp.concatenate([pv[:, :half], pv[:, half:] + pv1], axis=1)
        if m is not None:
            pv = jnp.exp2(m - m_new) * acc + pv
        return m_new, pv

    cur = [logits(steps[0], hh) for hh in range(2)]
    state = [(None, None), (None, None)]
    for s, step in enumerate(steps):
        qi, kt = step
        nxt = [None, None]
        for hh in range(2):
            if s + 1 < len(steps):
                nxt[hh] = logits(steps[s + 1], hh)
            state[hh] = update(step, hh, *cur[hh], *state[hh])
        cur = nxt
        if kt == qi:
            a0, a1 = state[0][1], state[1][1]
            pad = PV_ROWS - HEAD_DIM
            out = jnp.concatenate([a0[0:HEAD_DIM] / a0[HEAD_DIM:HEAD_DIM + 1],
                                   a1[pad:] / a1[pad - 1:pad]], axis=0)
            o_ref[qi * TQ:(qi + 1) * TQ, :] = out.T.astype(BF16)
            state = [(None, None), (None, None)]


def _attention(q8, k8, v8, seq):
    _, t, _ = q8.shape
    nb = t // seq
    pair_spec = pl.BlockSpec((2, seq, LANES), lambda b, p: (p, b, 0))
    return pl.pallas_call(
        _attn_kernel,
        grid=(nb, PAIRS),
        in_specs=[pair_spec, pair_spec, pair_spec],
        out_specs=pl.BlockSpec((seq, LANES), lambda b, p: (b, p)),
        out_shape=jax.ShapeDtypeStruct((t, A_WIDTH), BF16),
        compiler_params=pltpu.CompilerParams(
            dimension_semantics=("arbitrary", "arbitrary"), vmem_limit_bytes=VMEM_LIMIT),
        name="attn",
    )(q8, k8, v8)


def _tail_kernel(subs_per_seq, n_subs, *refs):
    first, cur, nxt = refs[0:5], refs[5:12], refs[12:19]
    mod_ref, modn_ref = refs[19:21]
    (wpool_ref, pscale_ref, convw_ref, wbr_ref, wout_ref, g_ref, gpre_ref, w1_ref, w2_ref, gpost_ref,
     out_ref, x1_ref) = refs[21:]
    i = pl.program_id(0)
    shift_f, scale_f, gate_f = mod_ref[0, 3], mod_ref[0, 4], mod_ref[0, 5]
    d = out_ref.shape[1]
    lane = lax.broadcasted_iota(jnp.int32, (SUB + HALO, POOL_WIDTH), 1)
    row = lax.broadcasted_iota(jnp.int32, (SUB + HALO, POOL_WIDTH), 0)
    gd = POOL_WIDTH // len(POOL_WINDOWS)
    wsz = jnp.where(lane < gd, 2.0, jnp.where(lane < 2 * gd, 4.0, jnp.where(lane < 3 * gd, 8.0, 16.0)))
    cw = convw_ref[...]
    o1 = A_WIDTH + POOL_WIDTH

    def with_history(ref, halo_ref, sub_in_seq):
        if halo_ref is None:
            head = jnp.zeros((HALO, ref.shape[1]), F32)
        else:
            head = jnp.where(sub_in_seq != 0, halo_ref[...].astype(F32), 0.0)
        return jnp.concatenate([head, ref[...].astype(F32)], axis=0)

    def mixing_tail(o_ref, pu_ref, puh_ref, cv_ref, cvh_ref, gl_ref, x_ref, sub_in_seq, gate_m):
        ext = with_history(pu_ref, puh_ref, sub_in_seq)
        a2 = ext + pltpu.roll(ext, 1, 0)
        a4 = a2 + pltpu.roll(a2, 2, 0)
        a8 = a4 + pltpu.roll(a4, 4, 0)
        a16 = a8 + pltpu.roll(a8, 8, 0)
        win = jnp.where(lane < gd, a2, jnp.where(lane < 2 * gd, a4, jnp.where(lane < 3 * gd, a8, a16)))
        frames = (sub_in_seq * SUB + row - (HALO - 1)).astype(F32)
        cnt = jnp.maximum(jnp.minimum(frames, wsz), 1.0)
        pm = (win / cnt - ext)[HALO:]
        br_b = jnp.dot(pm.astype(BF16), wpool_ref[...], preferred_element_type=F32) * pscale_ref[...]

        cve = with_history(cv_ref, cvh_ref, sub_in_seq)
        u = cve[:, 2 * CONV_WIDTH:] * cve[:, :CONV_WIDTH]
        y = cw[0:1] * pltpu.roll(u, 2, 0) + cw[1:2] * pltpu.roll(u, 1, 0) + cw[2:3] * u
        br_c = (cve[:, CONV_WIDTH:2 * CONV_WIDTH] * y)[HALO:]

        def gated(n, branch, lo, hi):
            b = jnp.dot(branch, wbr_ref[lo:hi, :], preferred_element_type=F32)
            return b + b * jnp.tanh(gl_ref[:, n * d:(n + 1) * d].astype(F32))

        merged = gated(0, o_ref[...], 0, A_WIDTH)
        yield None
        merged += gated(1, br_b.astype(BF16), A_WIDTH, o1)
        merged += gated(2, br_c.astype(BF16), o1, o1 + CONV_WIDTH)
        yield None
        yo = jnp.dot(merged.astype(BF16), wout_ref[...], preferred_element_type=F32)
        yield None
        yield x_ref[...] + gate_m * _rms(yo, g_ref[...])

    def mlp(x1):
        hb = (_rms(x1, gpre_ref[...]) * (1.0 + scale_f) + shift_f).astype(BF16)
        dff = w1_ref.shape[1]
        acc = jnp.zeros(x1.shape, F32)
        for c in range(dff // FF_CHUNK):
            up = jnp.dot(hb, w1_ref[:, c * FF_CHUNK:(c + 1) * FF_CHUNK], preferred_element_type=F32)
            act = jnp.square(jnp.maximum(up, 0.0)).astype(BF16)
            acc += jnp.dot(act, w2_ref[c * FF_CHUNK:(c + 1) * FF_CHUNK, :], preferred_element_type=F32)
            yield None
        yield x1 + gate_f * _rms(acc, gpost_ref[...])

    def interleave(*gens):
        last = [None] * len(gens)
        live = list(range(len(gens)))
        while live:
            for n in list(live):
                try:
                    last[n] = next(gens[n])
                except StopIteration:
                    live.remove(n)
        return last

    sub0 = 2 * i
    next0 = jnp.minimum(sub0 + 2, n_subs - 1)

    @pl.when(i == 0)
    def _():
        o_f, pu_f, cv_f, gl_f, x_f = first
        (x1_ref[...],) = interleave(mixing_tail(o_f, pu_f, None, cv_f, None, gl_f, x_f, 0, mod_ref[0, 2]))

    x1 = x1_ref[...]
    out, x1 = interleave(mlp(x1), mixing_tail(*cur, (sub0 + 1) % subs_per_seq, mod_ref[0, 2]))
    out_ref[0:SUB, :] = out
    out, x1 = interleave(mlp(x1), mixing_tail(*nxt, next0 % subs_per_seq, modn_ref[0, 2]))
    out_ref[SUB:2 * SUB, :] = out
    x1_ref[...] = x1


def _tail(o, pu, cv, gl, x2, mod, wpool, pscale, convw, wbr_bf, wout_bf, g_mix_post, g_ff_pre, w1_bf, w2_bf,
          g_ff_post, seq):
    t, d = x2.shape
    nt = t // (2 * SUB)
    n_subs = 2 * nt
    subs_per_seq = seq // SUB
    tiles_per_seq = subs_per_seq // 2
    halos_per_sub = SUB // HALO
    widths = (A_WIDTH, POOL_WIDTH, 3 * CONV_WIDTH, 3 * d, d)

    def sub_specs(sub_of_step, with_halos):
        specs = []
        for n, w in enumerate(widths):
            specs.append(pl.BlockSpec((SUB, w), lambda i: (sub_of_step(i), 0)))
            if with_halos and n in (1, 2):
                specs.append(pl.BlockSpec((HALO, w), lambda i: (sub_of_step(i) * halos_per_sub - 1, 0)))
        return specs

    def next_tile(i):
        return jnp.minimum(i + 1, nt - 1)

    def mod_spec(tile_of_step):
        return pl.BlockSpec((1,) + mod.shape[1:], lambda i: (tile_of_step(i) // tiles_per_seq, 0, 0, 0))

    first = [pl.BlockSpec((SUB, w), lambda i: (0, 0), pipeline_mode=pl.Buffered(1)) for w in widths]
    cur = sub_specs(lambda i: 2 * i + 1, True)
    nxt = sub_specs(lambda i: jnp.minimum(2 * i + 2, n_subs - 1), True)
    consts = (wpool, pscale, convw, wbr_bf, wout_bf, g_mix_post, g_ff_pre, w1_bf, w2_bf, g_ff_post)
    acts = (o, pu, cv, gl, x2)
    acts_h = (o, pu, pu, cv, cv, gl, x2)
    return pl.pallas_call(
        functools.partial(_tail_kernel, subs_per_seq, n_subs),
        grid=(nt,),
        in_specs=first + cur + nxt + [mod_spec(lambda i: i), mod_spec(next_tile)]
        + [_const_spec(a.shape) for a in consts],
        out_specs=pl.BlockSpec((2 * SUB, d), lambda i: (i, 0)),
        out_shape=jax.ShapeDtypeStruct((t, d), F32),
        scratch_shapes=[pltpu.VMEM((SUB, d), F32)],
        compiler_params=pltpu.CompilerParams(
            dimension_semantics=("arbitrary",), vmem_limit_bytes=VMEM_LIMIT),
        name="tail",
    )(*acts, *acts_h, *acts_h, mod, mod, *consts)


def _block_diag(w):
    g, c, dd = w.shape
    eye = jnp.eye(g, dtype=w.dtype)
    return (eye[:, None, :, None] * w[:, :, None, :]).reshape(g * c, g * dd)


def kernel(x, c, w_ada, b_ada, g_mix_pre, g_mix_post, g_ff_pre, g_ff_post, w_in, b_f, w_pool, pool_scale,
           conv_w, w_branch, w_out, w_ff1, w_ff2):
    nb, seq, d = x.shape
    depth = w_ada.shape[0]
    assert all(seq % tile == 0 for tile in (TM, 2 * SUB, TQ, TK))
    assert d % LANES == 0 and d % PREP_ROWS == 0
    t = nb * seq
    assert (6 * d) % (ADA_CHUNKS * LANES) == 0
    w_in_t = jnp.swapaxes(w_in, 1, 2)
    mod, w_in_bf = _ada(c, w_ada, b_ada, w_in_t)
    mod = mod.reshape(depth, nb, 6, 1, d)
    x2 = x.reshape(t, d)
    stacked = {"in": w_in_t, "branch": w_branch, "out": w_out, "ff1": w_ff1, "ff2": w_ff2}
    cast_keys = [("in", l) for l in range(1, depth)]
    cast_keys += [(name, l) for l in range(depth) for name in ("branch", "out", "ff1", "ff2")]
    casts = [(stacked[name], l, 0.5 if name == "branch" else 1.0) for name, l in cast_keys]
    bf16_w = {("in", 0): w_in_bf}
    spread = np.zeros((LANES, LANES), np.float32)
    head_of_lane = np.zeros((LANES,), np.int32)
    lane_used = np.zeros((LANES,), bool)
    for hd in range(HEADS):
        spread[_aug_base(hd):_aug_base(hd) + AUG, hd] = 1.0
        head_of_lane[_aug_base(hd):_aug_base(hd) + AUG] = hd
        lane_used[_aug_base(hd):_aug_base(hd) + AUG] = True
    spread = jnp.asarray(spread, BF16)
    bf_all = jnp.where(lane_used, b_f[:, head_of_lane], 0.0)
    for l in range(depth):
        bf = bf_all[l:l + 1]
        row = lambda a: a[l].reshape(1, -1)
        (q8, k8, v8, pu, cv, gl), cast = _inproj(x2, mod[l], row(g_mix_pre), bf16_w[("in", l)], spread, bf, seq,
                                                 casts if l == 0 else [])
        if l == 0:
            bf16_w.update(zip(cast_keys, cast))
        o = _attention(q8, k8, v8, seq)
        x2 = _tail(o, pu, cv, gl, x2, mod[l], _block_diag(w_pool[l]).astype(BF16), row(pool_scale),
                   conv_w[l], bf16_w[("branch", l)], bf16_w[("out", l)], row(g_mix_post), row(g_ff_pre),
                   bf16_w[("ff1", l)], bf16_w[("ff2", l)], row(g_ff_post), seq)
    return x2.reshape(nb, seq, d)
```

```python
import functools

import jax
import jax.numpy as jnp
import numpy as np
from jax import lax
from jax.experimental import pallas as pl
from jax.experimental.pallas import tpu as pltpu

F32 = jnp.float32
BF16 = jnp.bfloat16

LANES = 128
HEAD_DIM = 64
HEADS = 8
PAIRS = HEADS // 2
A_WIDTH = HEADS * HEAD_DIM
POOL_WINDOWS = (2, 4, 8, 16)
POOL_WIDTH = 256
CONV_WIDTH = 256
HALO = 16
RMS_EPS = 1e-6
NEG_INF = -1e30
LOG2E = 1.4426950408889634
AUG = 6
PV_ROWS = 128
ADA_CHUNKS = 8
VMEM_LIMIT = 56 * 1024 * 1024

TM = 512
TMT = 512
TQ = 512
TK = 512
FF_CHUNK = 1024
SUB = 256
PREP_ROWS = 128


def _const_spec(shape):
    n = len(shape)
    return pl.BlockSpec(shape, lambda *_: (0,) * n, pipeline_mode=pl.Buffered(1))


def _rms(x, g):
    ms = jnp.mean(x * x, axis=-1, keepdims=True)
    return x * lax.rsqrt(ms + RMS_EPS) * g


def _row_chunks(k, steps):
    rows = -(-(-(-k // steps)) // 16) * 16
    return rows, -(-k // rows)


def _nt_dot(a, b):
    return lax.dot_general(a, b, (((1,), (1,)), ((), ())), preferred_element_type=F32)


def _ada_kernel(c_ref, w_ref, b_ref, win_ref, o_ref, win_bf_ref):
    c = c_ref[...]
    sc = c * (1.0 / (1.0 + jnp.exp(-c)))
    w = w_ref[0]
    sc_hi, w_hi = sc.astype(BF16), w.astype(BF16)
    sc_lo = (sc - sc_hi.astype(F32)).astype(BF16)
    w_lo = (w - w_hi.astype(F32)).astype(BF16)
    dot = functools.partial(jnp.dot, preferred_element_type=F32)
    o_ref[0] = dot(sc_hi, w_hi) + dot(sc_hi, w_lo) + dot(sc_lo, w_hi) + b_ref[0]
    win_bf_ref[...] = win_ref[0].astype(BF16)


def _ada(c, w_ada, b_ada, w_in_t):
    depth, d, d6 = w_ada.shape
    nb = c.shape[0]
    n = ADA_CHUNKS
    cols = d6 // n
    rows, nblk = _row_chunks(w_in_t.shape[1], depth * n)
    assert nblk == depth * n
    ncol_in = w_in_t.shape[2]
    return pl.pallas_call(
        _ada_kernel,
        grid=(depth, n),
        in_specs=[
            pl.BlockSpec((nb, d), lambda l, j: (0, 0)),
            pl.BlockSpec((1, d, cols), lambda l, j: (l, 0, j)),
            pl.BlockSpec((1, 1, cols), lambda l, j: (l, 0, j)),
            pl.BlockSpec((1, rows, ncol_in), lambda l, j: (0, l * n + j, 0)),
        ],
        out_specs=(
            pl.BlockSpec((1, nb, cols), lambda l, j: (l, 0, j)),
            pl.BlockSpec((rows, ncol_in), lambda l, j: (l * n + j, 0)),
        ),
        out_shape=(
            jax.ShapeDtypeStruct((depth, nb, d6), F32),
            jax.ShapeDtypeStruct(w_in_t.shape[1:], BF16),
        ),
        compiler_params=pltpu.CompilerParams(
            dimension_semantics=("arbitrary", "arbitrary"), vmem_limit_bytes=VMEM_LIMIT),
        name="ada",
    )(c, w_ada, b_ada.reshape(depth, 1, d6), w_in_t)


def _aug_base(h):
    return (HEAD_DIM if h % 2 == 0 else 0) + AUG * (h // 2)


def _inproj_kernel(tiles_per_seq, cast_scales, x_ref, mod_ref, g_ref, w_ref, spread_ref, bf_ref, *refs):
    n_cast = len(cast_scales)
    cast_in = refs[:n_cast]
    q_ref, k_ref, v_ref, pu_ref, cv_ref, wgl_ref = refs[n_cast:n_cast + 6]
    cast_out = refs[n_cast + 6:2 * n_cast + 6]
    carry_ref, wf_ref, wrest_ref = refs[2 * n_cast + 6:]
    i = pl.program_id(0)
    d = x_ref.shape[1]

    for src, dst, s in zip(cast_in, cast_out, cast_scales):
        dst[...] = (src[0] if s == 1.0 else src[0] * s).astype(BF16)

    fo = 3 * A_WIDTH
    n_plain = POOL_WIDTH + 3 * CONV_WIDTH
    n_rest = n_plain + wgl_ref.shape[0]

    @pl.when(i == 0)
    def _():
        wf_ref[...] = jnp.dot(spread_ref[...], w_ref[fo:fo + LANES, :],
                              preferred_element_type=F32).astype(BF16)
        for r in range(0, n_rest, PREP_ROWS):
            end = min(fo + r + PREP_ROWS + 2 * HEADS, w_ref.shape[0])
            blk = w_ref[fo + r:end, :].astype(F32)[HEADS:HEADS + PREP_ROWS]
            if r < n_plain:
                wrest_ref[r:r + PREP_ROWS, :] = blk.astype(BF16)
            else:
                wgl_ref[r - n_plain:r - n_plain + PREP_ROWS, :] = (blk * 0.5).astype(BF16)

    shift, scale = mod_ref[0, 0], mod_ref[0, 1]
    hb = (_rms(x_ref[...], g_ref[...]) * (1.0 + scale) + shift).astype(BF16)
    tm = hb.shape[0]

    def proj(lo_, hi_):
        return _nt_dot(hb, w_ref[lo_:hi_, :])

    def rest(lo_, hi_):
        return _nt_dot(hb, wrest_ref[lo_:hi_, :])

    zf = _nt_dot(hb, wf_ref[...]) + bf_ref[...]
    pu_ref[...] = rest(0, POOL_WIDTH).astype(BF16)
    cv_ref[...] = rest(POOL_WIDTH, n_plain).astype(BF16)
    zq = proj(0, A_WIDTH) * (HEAD_DIM ** -0.5 * LOG2E)
    zk = proj(A_WIDTH, 2 * A_WIDTH)
    zv = proj(2 * A_WIDTH, 3 * A_WIDTH)

    lf = jnp.minimum(zf, 0.0) - jnp.log(1.0 + jnp.exp(-jnp.abs(zf)))
    row = lax.broadcasted_iota(jnp.int32, lf.shape, 0)
    step = 1
    while step < tm:
        lf = lf + jnp.where(row >= step, pltpu.roll(lf, step, 0), 0.0)
        step *= 2

    @pl.when(i % tiles_per_seq == 0)
    def _():
        carry_ref[...] = jnp.zeros_like(carry_ref)

    fc = lf + carry_ref[0:1, :]
    carry_ref[...] = jnp.broadcast_to(fc[tm - 1:tm, :], carry_ref.shape)

    f2 = fc * LOG2E
    hi = f2.astype(BF16).astype(F32)
    rem = f2 - hi
    mid = rem.astype(BF16).astype(F32)
    lo = rem - mid
    lane = lax.broadcasted_iota(jnp.int32, f2.shape, 1)
    j = (lane % HEAD_DIM) % AUG
    part = jnp.where(j % 3 == 0, hi, jnp.where(j % 3 == 1, mid, lo))
    aug_q = jnp.where(j < 3, 1.0, part)
    aug_k = jnp.where(j < 3, -part, 1.0)

    for hd in range(HEADS):
        p = hd // 2
        in_head = (lane >= (hd % 2) * HEAD_DIM) & (lane < (hd % 2 + 1) * HEAD_DIM)
        base = _aug_base(hd)
        in_aug = (lane >= base) & (lane < base + AUG)
        pair = slice(p * LANES, (p + 1) * LANES)
        q_ref[hd] = jnp.where(in_head, zq[:, pair], jnp.where(in_aug, aug_q, 0.0)).astype(BF16)
        k_ref[hd] = jnp.where(in_head, zk[:, pair], jnp.where(in_aug, aug_k, 0.0)).astype(BF16)
        ones_lane = HEAD_DIM if hd % 2 == 0 else HEAD_DIM - 1
        v_ref[hd] = jnp.where(in_head, zv[:, pair], jnp.where(lane == ones_lane, 1.0, 0.0)).astype(BF16)


def _inproj(x2, mod, g, w_in_bf, spread, bf, seq, casts):
    t, d = x2.shape
    nt = t // TM
    tiles_per_seq = seq // TM
    n_plain = POOL_WIDTH + 3 * CONV_WIDTH
    n_gate = w_in_bf.shape[0] - 3 * A_WIDTH - HEADS - n_plain
    cast_in_specs, cast_out_specs, cast_shapes = [], [], []
    for w, l, _ in casts:
        _, k, n = w.shape
        rows, nblk = _row_chunks(k, nt)
        cast_in_specs.append(pl.BlockSpec(
            (1, rows, n), functools.partial(lambda l_, nb_, i: (l_, jnp.minimum(i, nb_ - 1), 0), l, nblk)))
        cast_out_specs.append(pl.BlockSpec(
            (rows, n), functools.partial(lambda nb_, i: (jnp.minimum(i, nb_ - 1), 0), nblk)))
        cast_shapes.append(jax.ShapeDtypeStruct((k, n), BF16))
    out_shape = (
        jax.ShapeDtypeStruct((HEADS, t, LANES), BF16),
        jax.ShapeDtypeStruct((HEADS, t, LANES), BF16),
        jax.ShapeDtypeStruct((HEADS, t, LANES), BF16),
        jax.ShapeDtypeStruct((t, POOL_WIDTH), BF16),
        jax.ShapeDtypeStruct((t, 3 * CONV_WIDTH), BF16),
        jax.ShapeDtypeStruct((n_gate, d), BF16),
    )
    head_spec = pl.BlockSpec((HEADS, TM, LANES), lambda i: (0, i, 0))
    out_specs = (
        head_spec, head_spec, head_spec,
        pl.BlockSpec((TM, POOL_WIDTH), lambda i: (i, 0)),
        pl.BlockSpec((TM, 3 * CONV_WIDTH), lambda i: (i, 0)),
        pl.BlockSpec((n_gate, d), lambda i: (0, 0)),
    )
    outs = pl.pallas_call(
        functools.partial(_inproj_kernel, tiles_per_seq, tuple(s for _, _, s in casts)),
        grid=(nt,),
        in_specs=[
            pl.BlockSpec((TM, d), lambda i: (i, 0)),
            pl.BlockSpec((1, 2, 1, d), lambda i: (i // tiles_per_seq, 0, 0, 0)),
            _const_spec((1, d)),
            _const_spec(w_in_bf.shape),
            _const_spec(spread.shape),
            _const_spec((1, LANES)),
        ] + cast_in_specs,
        out_specs=out_specs + tuple(cast_out_specs),
        out_shape=out_shape + tuple(cast_shapes),
        scratch_shapes=[
            pltpu.VMEM((8, LANES), F32),
            pltpu.VMEM((LANES, d), BF16),
            pltpu.VMEM((n_plain, d), BF16),
        ],
        compiler_params=pltpu.CompilerParams(
            dimension_semantics=("arbitrary",), vmem_limit_bytes=VMEM_LIMIT),
        name="inproj",
    )(x2, mod, g, w_in_bf, spread, bf, *[w for w, _, _ in casts])
    return outs[:6], outs[6:]


def _attn_kernel(q_ref, k_ref, v_ref, o_ref):
    seq = q_ref.shape[1]
    nq = seq // TQ
    half = TK // 2
    steps = [(qi, kt) for qi in range(nq) for kt in range(qi + 1)]
    mask0 = (lax.broadcasted_iota(jnp.int32, (half, TQ), 0)
             <= lax.broadcasted_iota(jnp.int32, (half, TQ), 1))
    mask1 = (lax.broadcasted_iota(jnp.int32, (half, half), 0)
             <= lax.broadcasted_iota(jnp.int32, (half, half), 1))
    vts = [v_ref[hh].astype(F32).T.astype(BF16)[hh * (LANES - PV_ROWS):hh * (LANES - PV_ROWS) + PV_ROWS]
           for hh in range(2)]

    def logits(step, hh):
        qi, kt = step
        q = q_ref[hh, qi * TQ:(qi + 1) * TQ, :]
        if kt < qi:
            st = _nt_dot(k_ref[hh, kt * TK:(kt + 1) * TK, :], q)
            return (st,), jnp.max(st, axis=0, keepdims=True)
        st0 = jnp.where(mask0, _nt_dot(k_ref[hh, kt * TK:kt * TK + half, :], q), NEG_INF)
        st1 = jnp.where(mask1, _nt_dot(k_ref[hh, kt * TK + half:(kt + 1) * TK, :], q[half:]), NEG_INF)
        c0 = jnp.max(st0, axis=0, keepdims=True)
        c1 = jnp.max(st1, axis=0, keepdims=True)
        return (st0, st1), jnp.concatenate([c0[:, :half], jnp.maximum(c0[:, half:], c1)], axis=1)

    def update(step, hh, sts, cmax, m, acc):
        qi, kt = step
        vt = vts[hh]
        m_new = cmax if m is None else jnp.maximum(m, cmax)
        if kt < qi:
            pt = jnp.exp2(sts[0] - m_new).astype(BF16)
            pv = jnp.dot(vt[:, kt * TK:(kt + 1) * TK], pt, preferred_element_type=F32)
        else:
            pt0 = jnp.exp2(sts[0] - m_new).astype(BF16)
            pt1 = jnp.exp2(sts[1] - m_new[:, half:]).astype(BF16)
            pv = jnp.dot(vt[:, kt * TK:kt * TK + half], pt0, preferred_element_type=F32)
            pv1 = jnp.dot(vt[:, kt * TK + half:(kt + 1) * TK], pt1, preferred_element_type=F32)
            pv = jnp.concatenate([pv[:, :half], pv[:, half:] + pv1], axis=1)
        if m is not None:
            pv = jnp.exp2(m - m_new) * acc + pv
        return m_new, pv

    cur = [logits(steps[0], hh) for hh in range(2)]
    state = [(None, None), (None, None)]
    for s, step in enumerate(steps):
        qi, kt = step
        nxt = [None, None]
        for hh in range(2):
            if s + 1 < len(steps):
                nxt[hh] = logits(steps[s + 1], hh)
            state[hh] = update(step, hh, *cur[hh], *state[hh])
        cur = nxt
        if kt == qi:
            a0, a1 = state[0][1], state[1][1]
            pad = PV_ROWS - HEAD_DIM
            out = jnp.concatenate([a0[0:HEAD_DIM] / a0[HEAD_DIM:HEAD_DIM + 1],
                                   a1[pad:] / a1[pad - 1:pad]], axis=0)
            o_ref[qi * TQ:(qi + 1) * TQ, :] = out.T.astype(BF16)
            state = [(None, None), (None, None)]


def _attention(q8, k8, v8, seq):
    _, t, _ = q8.shape
    nb = t // seq
    pair_spec = pl.BlockSpec((2, seq, LANES), lambda b, p: (p, b, 0))
    return pl.pallas_call(
        _attn_kernel,
        grid=(nb, PAIRS),
        in_specs=[pair_spec, pair_spec, pair_spec],
        out_specs=pl.BlockSpec((seq, LANES), lambda b, p: (b, p)),
        out_shape=jax.ShapeDtypeStruct((t, A_WIDTH), BF16),
        compiler_params=pltpu.CompilerParams(
            dimension_semantics=("arbitrary", "arbitrary"), vmem_limit_bytes=VMEM_LIMIT),
        name="attn",
    )(q8, k8, v8)


def _tail_kernel(tiles_per_seq, o_ref, pu_ref, puh_ref, cv_ref, cvh_ref, x_ref, mod_ref,
                 gmix_ref, wgl_ref, wpool_ref, pscale_ref, convw_ref, wbr_ref, wout_ref, g_ref,
                 gpre_ref, w1_ref, w2_ref, gpost_ref, out_ref, x1_ref):
    i = pl.program_id(0)
    shift_m, scale_m, gate_m = mod_ref[0, 0], mod_ref[0, 1], mod_ref[0, 2]
    shift_f, scale_f, gate_f = mod_ref[0, 3], mod_ref[0, 4], mod_ref[0, 5]
    tile_in_seq = i % tiles_per_seq
    tm, d = x_ref.shape
    n_sub = tm // SUB
    lane = lax.broadcasted_iota(jnp.int32, (SUB + HALO, POOL_WIDTH), 1)
    row = lax.broadcasted_iota(jnp.int32, (SUB + HALO, POOL_WIDTH), 0)
    gd = POOL_WIDTH // len(POOL_WINDOWS)
    wsz = jnp.where(lane < gd, 2.0, jnp.where(lane < 2 * gd, 4.0, jnp.where(lane < 3 * gd, 8.0, 16.0)))
    cw = convw_ref[...]
    o1 = A_WIDTH + POOL_WIDTH

    def with_history(ref, halo_ref, j):
        if j == 0:
            head = jnp.where(tile_in_seq != 0, halo_ref[...].astype(F32), 0.0)
        else:
            head = ref[j * SUB - HALO:j * SUB, :].astype(F32)
        return jnp.concatenate([head, ref[j * SUB:(j + 1) * SUB, :].astype(F32)], axis=0)

    def mix(j):
        rows = slice(j * SUB, (j + 1) * SUB)
        ext = with_history(pu_ref, puh_ref, j)
        a2 = ext + pltpu.roll(ext, 1, 0)
        a4 = a2 + pltpu.roll(a2, 2, 0)
        a8 = a4 + pltpu.roll(a4, 4, 0)
        a16 = a8 + pltpu.roll(a8, 8, 0)
        win = jnp.where(lane < gd, a2, jnp.where(lane < 2 * gd, a4, jnp.where(lane < 3 * gd, a8, a16)))
        frames = (tile_in_seq * tm + j * SUB + row - (HALO - 1)).astype(F32)
        cnt = jnp.maximum(jnp.minimum(frames, wsz), 1.0)
        pm = (win / cnt - ext)[HALO:]
        br_b = jnp.dot(pm.astype(BF16), wpool_ref[...], preferred_element_type=F32) * pscale_ref[...]

        cve = with_history(cv_ref, cvh_ref, j)
        u = cve[:, 2 * CONV_WIDTH:] * cve[:, :CONV_WIDTH]
        y = cw[0:1] * pltpu.roll(u, 2, 0) + cw[1:2] * pltpu.roll(u, 1, 0) + cw[2:3] * u
        br_c = (cve[:, CONV_WIDTH:2 * CONV_WIDTH] * y)[HALO:]

        hb = (_rms(x_ref[rows, :], gmix_ref[...]) * (1.0 + scale_m) + shift_m).astype(BF16)

        def gated(n, branch, lo, hi):
            b = jnp.dot(branch, wbr_ref[lo:hi, :], preferred_element_type=F32)
            return b + b * jnp.tanh(_nt_dot(hb, wgl_ref[n * d:(n + 1) * d, :]))

        merged = gated(0, o_ref[rows, :], 0, A_WIDTH)
        merged += gated(1, br_b.astype(BF16), A_WIDTH, o1)
        merged += gated(2, br_c.astype(BF16), o1, o1 + CONV_WIDTH)
        return merged.astype(BF16)

    def project(j, merged):
        rows = slice(j * SUB, (j + 1) * SUB)
        yo = jnp.dot(merged, wout_ref[...], preferred_element_type=F32)
        x1_ref[rows, :] = x_ref[rows, :] + gate_m * _rms(yo, g_ref[...])

    cur = mix(0)
    for j in range(n_sub):
        nxt = mix(j + 1) if j + 1 < n_sub else None
        project(j, cur)
        cur = nxt

    x1 = x1_ref[...]
    hb = (_rms(x1, gpre_ref[...]) * (1.0 + scale_f) + shift_f).astype(BF16)
    dff = w1_ref.shape[1]
    acc = jnp.zeros(x1.shape, F32)
    for c in range(dff // FF_CHUNK):
        up = jnp.dot(hb, w1_ref[:, c * FF_CHUNK:(c + 1) * FF_CHUNK], preferred_element_type=F32)
        act = jnp.square(jnp.maximum(up, 0.0)).astype(BF16)
        acc += jnp.dot(act, w2_ref[c * FF_CHUNK:(c + 1) * FF_CHUNK, :], preferred_element_type=F32)
    out_ref[...] = x1 + gate_f * _rms(acc, gpost_ref[...])


def _tail(o, pu, cv, x2, mod, g_mix_pre, wgl_bf, wpool, pscale, convw, wbr_bf, wout_bf, g_mix_post, g_ff_pre,
          w1_bf, w2_bf, g_ff_post, seq):
    t, d = x2.shape
    nt = t // TMT
    tiles_per_seq = seq // TMT
    hb = TMT // HALO

    def halo_map(i):
        return (jnp.maximum(i * hb - 1, 0), 0)

    consts = (g_mix_pre, wgl_bf, wpool, pscale, convw, wbr_bf, wout_bf, g_mix_post, g_ff_pre, w1_bf, w2_bf,
              g_ff_post)
    return pl.pallas_call(
        functools.partial(_tail_kernel, tiles_per_seq),
        grid=(nt,),
        in_specs=[
            pl.BlockSpec((TMT, A_WIDTH), lambda i: (i, 0)),
            pl.BlockSpec((TMT, POOL_WIDTH), lambda i: (i, 0)),
            pl.BlockSpec((HALO, POOL_WIDTH), halo_map),
            pl.BlockSpec((TMT, 3 * CONV_WIDTH), lambda i: (i, 0)),
            pl.BlockSpec((HALO, 3 * CONV_WIDTH), halo_map),
            pl.BlockSpec((TMT, d), lambda i: (i, 0)),
            pl.BlockSpec((1,) + mod.shape[1:], lambda i: (i // tiles_per_seq, 0, 0, 0)),
        ] + [_const_spec(a.shape) for a in consts],
        out_specs=pl.BlockSpec((TMT, d), lambda i: (i, 0)),
        out_shape=jax.ShapeDtypeStruct((t, d), F32),
        scratch_shapes=[pltpu.VMEM((TMT, d), F32)],
        compiler_params=pltpu.CompilerParams(
            dimension_semantics=("arbitrary",), vmem_limit_bytes=VMEM_LIMIT),
        name="tail",
    )(o, pu, pu, cv, cv, x2, mod, *consts)


def _block_diag(w):
    g, c, dd = w.shape
    eye = jnp.eye(g, dtype=w.dtype)
    return (eye[:, None, :, None] * w[:, :, None, :]).reshape(g * c, g * dd)


def kernel(x, c, w_ada, b_ada, g_mix_pre, g_mix_post, g_ff_pre, g_ff_post, w_in, b_f, w_pool, pool_scale,
           conv_w, w_branch, w_out, w_ff1, w_ff2):
    nb, seq, d = x.shape
    depth = w_ada.shape[0]
    assert all(seq % tile == 0 for tile in (TM, TMT, TQ, TK)) and TMT % SUB == 0
    assert d % LANES == 0 and d % PREP_ROWS == 0
    t = nb * seq
    assert (6 * d) % (ADA_CHUNKS * LANES) == 0
    w_in_t = jnp.swapaxes(w_in, 1, 2)
    mod, w_in_bf = _ada(c, w_ada, b_ada, w_in_t)
    mod = mod.reshape(depth, nb, 6, 1, d)
    x2 = x.reshape(t, d)
    stacked = {"in": w_in_t, "branch": w_branch, "out": w_out, "ff1": w_ff1, "ff2": w_ff2}
    cast_keys = [("in", l) for l in range(1, depth)]
    cast_keys += [(name, l) for l in range(depth) for name in ("branch", "out", "ff1", "ff2")]
    casts = [(stacked[name], l, 0.5 if name == "branch" else 1.0) for name, l in cast_keys]
    bf16_w = {("in", 0): w_in_bf}
    spread = np.zeros((LANES, LANES), np.float32)
    head_of_lane = np.zeros((LANES,), np.int32)
    lane_used = np.zeros((LANES,), bool)
    for hd in range(HEADS):
        spread[_aug_base(hd):_aug_base(hd) + AUG, hd] = 1.0
        head_of_lane[_aug_base(hd):_aug_base(hd) + AUG] = hd
        lane_used[_aug_base(hd):_aug_base(hd) + AUG] = True
    spread = jnp.asarray(spread, BF16)
    bf_all = jnp.where(lane_used, b_f[:, head_of_lane], 0.0)
    for l in range(depth):
        bf = bf_all[l:l + 1]
        row = lambda a: a[l].reshape(1, -1)
        (q8, k8, v8, pu, cv, wgl), cast = _inproj(x2, mod[l], row(g_mix_pre), bf16_w[("in", l)], spread, bf, seq,
                                                  casts if l == 0 else [])
        if l == 0:
            bf16_w.update(zip(cast_keys, cast))
        o = _attention(q8, k8, v8, seq)
        x2 = _tail(o, pu, cv, x2, mod[l], row(g_mix_pre), wgl, _block_diag(w_pool[l]).astype(BF16),
                   row(pool_scale), conv_w[l], bf16_w[("branch", l)], bf16_w[("out", l)], row(g_mix_post),
                   row(g_ff_pre), bf16_w[("ff1", l)], bf16_w[("ff2", l)], row(g_ff_post), seq)
    return x2.reshape(nb, seq, d)
```

```python
import functools

import jax
import jax.numpy as jnp
import numpy as np
from jax import lax
from jax.experimental import pallas as pl
from jax.experimental.pallas import tpu as pltpu

F32 = jnp.float32
BF16 = jnp.bfloat16

LANES = 128
HEAD_DIM = 64
HEADS = 8
PAIRS = HEADS // 2
A_WIDTH = HEADS * HEAD_DIM
POOL_WINDOWS = (2, 4, 8, 16)
POOL_WIDTH = 256
CONV_WIDTH = 256
HALO = 16
RMS_EPS = 1e-6
NEG_INF = -1e30
LOG2E = 1.4426950408889634
AUG = 6
PV_ROWS = 128
ADA_CHUNKS = 8
VMEM_LIMIT = 56 * 1024 * 1024

TM = 512
TMT = 512
TQ = 512
TK = 512
FF_CHUNK = 1024
SUB = 256
PREP_ROWS = 128


def _const_spec(shape):
    n = len(shape)
    return pl.BlockSpec(shape, lambda *_: (0,) * n, pipeline_mode=pl.Buffered(1))


def _rms(x, g):
    ms = jnp.mean(x * x, axis=-1, keepdims=True)
    return x * lax.rsqrt(ms + RMS_EPS) * g


def _row_chunks(k, steps):
    rows = -(-(-(-k // steps)) // 16) * 16
    return rows, -(-k // rows)


def _nt_dot(a, b):
    return lax.dot_general(a, b, (((1,), (1,)), ((), ())), preferred_element_type=F32)


def _ada_kernel(c_ref, w_ref, b_ref, win_ref, o_ref, win_bf_ref):
    c = c_ref[...]
    sc = c * (1.0 / (1.0 + jnp.exp(-c)))
    w = w_ref[0]
    sc_hi, w_hi = sc.astype(BF16), w.astype(BF16)
    sc_lo = (sc - sc_hi.astype(F32)).astype(BF16)
    w_lo = (w - w_hi.astype(F32)).astype(BF16)
    dot = functools.partial(jnp.dot, preferred_element_type=F32)
    o_ref[0] = dot(sc_hi, w_hi) + dot(sc_hi, w_lo) + dot(sc_lo, w_hi) + b_ref[0]
    win_bf_ref[...] = win_ref[0].astype(BF16)


def _ada(c, w_ada, b_ada, w_in_t):
    depth, d, d6 = w_ada.shape
    nb = c.shape[0]
    n = ADA_CHUNKS
    cols = d6 // n
    rows, nblk = _row_chunks(w_in_t.shape[1], depth * n)
    assert nblk == depth * n
    ncol_in = w_in_t.shape[2]
    return pl.pallas_call(
        _ada_kernel,
        grid=(depth, n),
        in_specs=[
            pl.BlockSpec((nb, d), lambda l, j: (0, 0)),
            pl.BlockSpec((1, d, cols), lambda l, j: (l, 0, j)),
            pl.BlockSpec((1, 1, cols), lambda l, j: (l, 0, j)),
            pl.BlockSpec((1, rows, ncol_in), lambda l, j: (0, l * n + j, 0)),
        ],
        out_specs=(
            pl.BlockSpec((1, nb, cols), lambda l, j: (l, 0, j)),
            pl.BlockSpec((rows, ncol_in), lambda l, j: (l * n + j, 0)),
        ),
        out_shape=(
            jax.ShapeDtypeStruct((depth, nb, d6), F32),
            jax.ShapeDtypeStruct(w_in_t.shape[1:], BF16),
        ),
        compiler_params=pltpu.CompilerParams(
            dimension_semantics=("arbitrary", "arbitrary"), vmem_limit_bytes=VMEM_LIMIT),
        name="ada",
    )(c, w_ada, b_ada.reshape(depth, 1, d6), w_in_t)


def _aug_base(h):
    return (HEAD_DIM if h % 2 == 0 else 0) + AUG * (h // 2)


def _inproj_kernel(tiles_per_seq, cast_scales, x_ref, mod_ref, g_ref, w_ref, spread_ref, bf_ref, *refs):
    n_cast = len(cast_scales)
    cast_in = refs[:n_cast]
    q_ref, k_ref, v_ref, pu_ref, cv_ref, wgl_ref = refs[n_cast:n_cast + 6]
    cast_out = refs[n_cast + 6:2 * n_cast + 6]
    carry_ref, wf_ref, wrest_ref = refs[2 * n_cast + 6:]
    i = pl.program_id(0)
    d = x_ref.shape[1]

    for src, dst, s in zip(cast_in, cast_out, cast_scales):
        dst[...] = (src[0] if s == 1.0 else src[0] * s).astype(BF16)

    fo = 3 * A_WIDTH
    n_plain = POOL_WIDTH + 3 * CONV_WIDTH
    n_rest = n_plain + wgl_ref.shape[1]

    @pl.when(i == 0)
    def _():
        wf_ref[...] = jnp.dot(spread_ref[...], w_ref[fo:fo + LANES, :],
                              preferred_element_type=F32).astype(BF16)
        for r in range(0, n_rest, PREP_ROWS):
            end = min(fo + r + PREP_ROWS + 2 * HEADS, w_ref.shape[0])
            blk = w_ref[fo + r:end, :].astype(F32)[HEADS:HEADS + PREP_ROWS]
            if r < n_plain:
                wrest_ref[r:r + PREP_ROWS, :] = blk.astype(BF16)
            else:
                wgl_ref[:, r - n_plain:r - n_plain + PREP_ROWS] = (blk * 0.5).T.astype(BF16)

    shift, scale = mod_ref[0, 0], mod_ref[0, 1]
    hb = (_rms(x_ref[...], g_ref[...]) * (1.0 + scale) + shift).astype(BF16)
    tm = hb.shape[0]

    def proj(lo_, hi_):
        return _nt_dot(hb, w_ref[lo_:hi_, :])

    def rest(lo_, hi_):
        return _nt_dot(hb, wrest_ref[lo_:hi_, :])

    zf = _nt_dot(hb, wf_ref[...]) + bf_ref[...]
    zv = proj(2 * A_WIDTH, 3 * A_WIDTH)
    groups = [slice(g * 2 * LANES, (g + 1) * 2 * LANES) for g in range(PAIRS // 2)]
    zq = [proj(s.start, s.stop) * (HEAD_DIM ** -0.5 * LOG2E) for s in groups]
    zk = [proj(A_WIDTH + s.start, A_WIDTH + s.stop) for s in groups]

    lf = jnp.minimum(zf, 0.0) - jnp.log(1.0 + jnp.exp(-jnp.abs(zf)))
    row = lax.broadcasted_iota(jnp.int32, lf.shape, 0)
    step = 1
    while step < tm:
        lf = lf + jnp.where(row >= step, pltpu.roll(lf, step, 0), 0.0)
        step *= 2

    @pl.when(i % tiles_per_seq == 0)
    def _():
        carry_ref[...] = jnp.zeros_like(carry_ref)

    fc = lf + carry_ref[0:1, :]
    carry_ref[...] = jnp.broadcast_to(fc[tm - 1:tm, :], carry_ref.shape)

    f2 = fc * LOG2E
    hi = f2.astype(BF16).astype(F32)
    rem = f2 - hi
    mid = rem.astype(BF16).astype(F32)
    lo = rem - mid
    lane = lax.broadcasted_iota(jnp.int32, f2.shape, 1)
    j = (lane % HEAD_DIM) % AUG
    part = jnp.where(j % 3 == 0, hi, jnp.where(j % 3 == 1, mid, lo))
    aug_q = jnp.where(j < 3, 1.0, part)
    aug_k = jnp.where(j < 3, -part, 1.0)

    lane_b = lane.astype(F32).astype(BF16)
    aug_q_b, aug_k_b = aug_q.astype(BF16), aug_k.astype(BF16)
    zero_b = jnp.zeros_like(aug_q_b)
    one_b = jnp.ones_like(aug_q_b)

    def lanes_in(lo_, hi_):
        return (lane_b >= lo_) & (lane_b < hi_)

    for p in range(PAIRS):
        pair = slice(p * LANES, (p + 1) * LANES)
        in_group = slice((p % 2) * LANES, (p % 2 + 1) * LANES)
        zq_b, zk_b = zq[p // 2][:, in_group].astype(BF16), zk[p // 2][:, in_group].astype(BF16)
        zv_b = zv[:, pair].astype(BF16)
        for hd in (2 * p, 2 * p + 1):
            in_head = lanes_in((hd % 2) * HEAD_DIM, (hd % 2 + 1) * HEAD_DIM)
            in_aug = lanes_in(_aug_base(hd), _aug_base(hd) + AUG)
            q_ref[hd] = jnp.where(in_head, zq_b, jnp.where(in_aug, aug_q_b, zero_b))
            k_ref[hd] = jnp.where(in_head, zk_b, jnp.where(in_aug, aug_k_b, zero_b))
            ones_lane = HEAD_DIM if hd % 2 == 0 else HEAD_DIM - 1
            v_ref[hd] = jnp.where(in_head, zv_b, jnp.where(lane_b == ones_lane, one_b, zero_b))

    pu_ref[...] = rest(0, POOL_WIDTH).astype(BF16)
    cv_ref[...] = rest(POOL_WIDTH, n_plain).astype(BF16)


def _inproj(x2, mod, g, w_in_bf, spread, bf, seq, casts):
    t, d = x2.shape
    nt = t // TM
    tiles_per_seq = seq // TM
    n_plain = POOL_WIDTH + 3 * CONV_WIDTH
    n_gate = w_in_bf.shape[0] - 3 * A_WIDTH - HEADS - n_plain
    cast_in_specs, cast_out_specs, cast_shapes = [], [], []
    for w, l, _ in casts:
        _, k, n = w.shape
        rows, nblk = _row_chunks(k, nt)
        cast_in_specs.append(pl.BlockSpec(
            (1, rows, n), functools.partial(lambda l_, nb_, i: (l_, jnp.minimum(i, nb_ - 1), 0), l, nblk)))
        cast_out_specs.append(pl.BlockSpec(
            (rows, n), functools.partial(lambda nb_, i: (jnp.minimum(i, nb_ - 1), 0), nblk)))
        cast_shapes.append(jax.ShapeDtypeStruct((k, n), BF16))
    out_shape = (
        jax.ShapeDtypeStruct((HEADS, t, LANES), BF16),
        jax.ShapeDtypeStruct((HEADS, t, LANES), BF16),
        jax.ShapeDtypeStruct((HEADS, t, LANES), BF16),
        jax.ShapeDtypeStruct((t, POOL_WIDTH), BF16),
        jax.ShapeDtypeStruct((t, 3 * CONV_WIDTH), BF16),
        jax.ShapeDtypeStruct((d, n_gate), BF16),
    )
    head_spec = pl.BlockSpec((HEADS, TM, LANES), lambda i: (0, i, 0))
    out_specs = (
        head_spec, head_spec, head_spec,
        pl.BlockSpec((TM, POOL_WIDTH), lambda i: (i, 0)),
        pl.BlockSpec((TM, 3 * CONV_WIDTH), lambda i: (i, 0)),
        pl.BlockSpec((d, n_gate), lambda i: (0, 0)),
    )
    outs = pl.pallas_call(
        functools.partial(_inproj_kernel, tiles_per_seq, tuple(s for _, _, s in casts)),
        grid=(nt,),
        in_specs=[
            pl.BlockSpec((TM, d), lambda i: (i, 0)),
            pl.BlockSpec((1, 2, 1, d), lambda i: (i // tiles_per_seq, 0, 0, 0)),
            _const_spec((1, d)),
            _const_spec(w_in_bf.shape),
            _const_spec(spread.shape),
            _const_spec((1, LANES)),
        ] + cast_in_specs,
        out_specs=out_specs + tuple(cast_out_specs),
        out_shape=out_shape + tuple(cast_shapes),
        scratch_shapes=[
            pltpu.VMEM((8, LANES), F32),
            pltpu.VMEM((LANES, d), BF16),
            pltpu.VMEM((n_plain, d), BF16),
        ],
        compiler_params=pltpu.CompilerParams(
            dimension_semantics=("arbitrary",), vmem_limit_bytes=VMEM_LIMIT),
        name="inproj",
    )(x2, mod, g, w_in_bf, spread, bf, *[w for w, _, _ in casts])
    return outs[:6], outs[6:]


def _attn_kernel(q_ref, k_ref, v_ref, o_ref):
    seq = q_ref.shape[1]
    nq = seq // TQ
    half = TK // 2
    steps = [(qi, kt) for qi in range(nq) for kt in range(qi + 1)]
    mask0 = (lax.broadcasted_iota(jnp.int32, (half, TQ), 0)
             <= lax.broadcasted_iota(jnp.int32, (half, TQ), 1))
    mask1 = (lax.broadcasted_iota(jnp.int32, (half, half), 0)
             <= lax.broadcasted_iota(jnp.int32, (half, half), 1))
    vts = [v_ref[hh].astype(F32).T.astype(BF16)[hh * (LANES - PV_ROWS):hh * (LANES - PV_ROWS) + PV_ROWS]
           for hh in range(2)]

    def logits(step, hh):
        qi, kt = step
        q = q_ref[hh, qi * TQ:(qi + 1) * TQ, :]
        if kt < qi:
            st = _nt_dot(k_ref[hh, kt * TK:(kt + 1) * TK, :], q)
            return (st,), jnp.max(st, axis=0, keepdims=True)
        st0 = jnp.where(mask0, _nt_dot(k_ref[hh, kt * TK:kt * TK + half, :], q), NEG_INF)
        st1 = jnp.where(mask1, _nt_dot(k_ref[hh, kt * TK + half:(kt + 1) * TK, :], q[half:]), NEG_INF)
        c0 = jnp.max(st0, axis=0, keepdims=True)
        c1 = jnp.max(st1, axis=0, keepdims=True)
        return (st0, st1), jnp.concatenate([c0[:, :half], jnp.maximum(c0[:, half:], c1)], axis=1)

    def update(step, hh, sts, cmax, m, acc):
        qi, kt = step
        vt = vts[hh]
        m_new = cmax if m is None else jnp.maximum(m, cmax)
        if kt < qi:
            pt = jnp.exp2(sts[0] - m_new).astype(BF16)
            pv = jnp.dot(vt[:, kt * TK:(kt + 1) * TK], pt, preferred_element_type=F32)
        else:
            pt0 = jnp.exp2(sts[0] - m_new).astype(BF16)
            pt1 = jnp.exp2(sts[1] - m_new[:, half:]).astype(BF16)
            pv = jnp.dot(vt[:, kt * TK:kt * TK + half], pt0, preferred_element_type=F32)
            pv1 = jnp.dot(vt[:, kt * TK + half:(kt + 1) * TK], pt1, preferred_element_type=F32)
            pv = jnp.concatenate([pv[:, :half], pv[:, half:] + pv1], axis=1)
        if m is not None:
            pv = jnp.exp2(m - m_new) * acc + pv
        return m_new, pv

    cur = [logits(steps[0], hh) for hh in range(2)]
    state = [(None, None), (None, None)]
    for s, step in enumerate(steps):
        qi, kt = step
        nxt = [None, None]
        for hh in range(2):
            if s + 1 < len(steps):
                nxt[hh] = logits(steps[s + 1], hh)
            state[hh] = update(step, hh, *cur[hh], *state[hh])
        cur = nxt
        if kt == qi:
            a0, a1 = state[0][1], state[1][1]
            pad = PV_ROWS - HEAD_DIM
            out = jnp.concatenate([a0[0:HEAD_DIM] / a0[HEAD_DIM:HEAD_DIM + 1],
                                   a1[pad:] / a1[pad - 1:pad]], axis=0)
            o_ref[qi * TQ:(qi + 1) * TQ, :] = out.T.astype(BF16)
            state = [(None, None), (None, None)]


def _attention(q8, k8, v8, seq):
    _, t, _ = q8.shape
    nb = t // seq
    pair_spec = pl.BlockSpec((2, seq, LANES), lambda b, p: (p, b, 0))
    return pl.pallas_call(
        _attn_kernel,
        grid=(nb, PAIRS),
        in_specs=[pair_spec, pair_spec, pair_spec],
        out_specs=pl.BlockSpec((seq, LANES), lambda b, p: (b, p)),
        out_shape=jax.ShapeDtypeStruct((t, A_WIDTH), BF16),
        compiler_params=pltpu.CompilerParams(
            dimension_semantics=("arbitrary", "arbitrary"), vmem_limit_bytes=VMEM_LIMIT),
        name="attn",
    )(q8, k8, v8)


def _tail_kernel(tiles_per_seq, o_ref, pu_ref, puh_ref, cv_ref, cvh_ref, x_ref, mod_ref,
                 gmix_ref, wgl_ref, wpool_ref, pscale_ref, convw_ref, wbr_ref, wout_ref, g_ref,
                 gpre_ref, w1_ref, w2_ref, gpost_ref, out_ref, x1_ref):
    i = pl.program_id(0)
    shift_m, scale_m, gate_m = mod_ref[0, 0], mod_ref[0, 1], mod_ref[0, 2]
    shift_f, scale_f, gate_f = mod_ref[0, 3], mod_ref[0, 4], mod_ref[0, 5]
    tile_in_seq = i % tiles_per_seq
    tm, d = x_ref.shape
    n_sub = tm // SUB
    lane = lax.broadcasted_iota(jnp.int32, (SUB + HALO, POOL_WIDTH), 1)
    row = lax.broadcasted_iota(jnp.int32, (SUB + HALO, POOL_WIDTH), 0)
    gd = POOL_WIDTH // len(POOL_WINDOWS)
    wsz = jnp.where(lane < gd, 2.0, jnp.where(lane < 2 * gd, 4.0, jnp.where(lane < 3 * gd, 8.0, 16.0)))
    cw = convw_ref[...]
    o1 = A_WIDTH + POOL_WIDTH

    def with_history(ref, halo_ref, j):
        if j == 0:
            head = jnp.where(tile_in_seq != 0, halo_ref[...].astype(F32), 0.0)
        else:
            head = ref[j * SUB - HALO:j * SUB, :].astype(F32)
        return jnp.concatenate([head, ref[j * SUB:(j + 1) * SUB, :].astype(F32)], axis=0)

    def mix(j):
        rows = slice(j * SUB, (j + 1) * SUB)
        ext = with_history(pu_ref, puh_ref, j)
        a2 = ext + pltpu.roll(ext, 1, 0)
        a4 = a2 + pltpu.roll(a2, 2, 0)
        a8 = a4 + pltpu.roll(a4, 4, 0)
        a16 = a8 + pltpu.roll(a8, 8, 0)
        win = jnp.where(lane < gd, a2, jnp.where(lane < 2 * gd, a4, jnp.where(lane < 3 * gd, a8, a16)))
        frames = (tile_in_seq * tm + j * SUB + row - (HALO - 1)).astype(F32)
        cnt = jnp.maximum(jnp.minimum(frames, wsz), 1.0)
        pm = (win / cnt - ext)[HALO:]
        br_b = jnp.dot(pm.astype(BF16), wpool_ref[...], preferred_element_type=F32) * pscale_ref[...]

        cve = with_history(cv_ref, cvh_ref, j)
        u = cve[:, 2 * CONV_WIDTH:] * cve[:, :CONV_WIDTH]
        y = cw[0:1] * pltpu.roll(u, 2, 0) + cw[1:2] * pltpu.roll(u, 1, 0) + cw[2:3] * u
        br_c = (cve[:, CONV_WIDTH:2 * CONV_WIDTH] * y)[HALO:]

        hb = (_rms(x_ref[rows, :], gmix_ref[...]) * (1.0 + scale_m) + shift_m).astype(BF16)

        def gated(n, branch, lo, hi):
            b = jnp.dot(branch, wbr_ref[lo:hi, :], preferred_element_type=F32)
            return b + b * jnp.tanh(jnp.dot(hb, wgl_ref[:, n * d:(n + 1) * d], preferred_element_type=F32))

        merged = gated(0, o_ref[rows, :], 0, A_WIDTH)
        merged += gated(1, br_b.astype(BF16), A_WIDTH, o1)
        merged += gated(2, br_c.astype(BF16), o1, o1 + CONV_WIDTH)
        return merged.astype(BF16)

    def project(j, merged):
        rows = slice(j * SUB, (j + 1) * SUB)
        yo = jnp.dot(merged, wout_ref[...], preferred_element_type=F32)
        x1_ref[rows, :] = x_ref[rows, :] + gate_m * _rms(yo, g_ref[...])

    cur = mix(0)
    for j in range(n_sub):
        nxt = mix(j + 1) if j + 1 < n_sub else None
        project(j, cur)
        cur = nxt

    x1 = x1_ref[...]
    hb = (_rms(x1, gpre_ref[...]) * (1.0 + scale_f) + shift_f).astype(BF16)
    dff = w1_ref.shape[1]
    acc = jnp.zeros(x1.shape, F32)
    for c in range(dff // FF_CHUNK):
        up = jnp.dot(hb, w1_ref[:, c * FF_CHUNK:(c + 1) * FF_CHUNK], preferred_element_type=F32)
        act = jnp.square(jnp.maximum(up, 0.0)).astype(BF16)
        acc += jnp.dot(act, w2_ref[c * FF_CHUNK:(c + 1) * FF_CHUNK, :], preferred_element_type=F32)
    out_ref[...] = x1 + gate_f * _rms(acc, gpost_ref[...])


def _tail(o, pu, cv, x2, mod, g_mix_pre, wgl_bf, wpool, pscale, convw, wbr_bf, wout_bf, g_mix_post, g_ff_pre,
          w1_bf, w2_bf, g_ff_post, seq):
    t, d = x2.shape
    nt = t // TMT
    tiles_per_seq = seq // TMT
    hb = TMT // HALO

    def halo_map(i):
        return (jnp.maximum(i * hb - 1, 0), 0)

    consts = (g_mix_pre, wgl_bf, wpool, pscale, convw, wbr_bf, wout_bf, g_mix_post, g_ff_pre, w1_bf, w2_bf,
              g_ff_post)
    return pl.pallas_call(
        functools.partial(_tail_kernel, tiles_per_seq),
        grid=(nt,),
        in_specs=[
            pl.BlockSpec((TMT, A_WIDTH), lambda i: (i, 0)),
            pl.BlockSpec((TMT, POOL_WIDTH), lambda i: (i, 0)),
            pl.BlockSpec((HALO, POOL_WIDTH), halo_map),
            pl.BlockSpec((TMT, 3 * CONV_WIDTH), lambda i: (i, 0)),
            pl.BlockSpec((HALO, 3 * CONV_WIDTH), halo_map),
            pl.BlockSpec((TMT, d), lambda i: (i, 0)),
            pl.BlockSpec((1,) + mod.shape[1:], lambda i: (i // tiles_per_seq, 0, 0, 0)),
        ] + [_const_spec(a.shape) for a in consts],
        out_specs=pl.BlockSpec((TMT, d), lambda i: (i, 0)),
        out_shape=jax.ShapeDtypeStruct((t, d), F32),
        scratch_shapes=[pltpu.VMEM((TMT, d), F32)],
        compiler_params=pltpu.CompilerParams(
            dimension_semantics=("arbitrary",), vmem_limit_bytes=VMEM_LIMIT),
        name="tail",
    )(o, pu, pu, cv, cv, x2, mod, *consts)


def _block_diag(w):
    g, c, dd = w.shape
    eye = jnp.eye(g, dtype=w.dtype)
    return (eye[:, None, :, None] * w[:, :, None, :]).reshape(g * c, g * dd)


def kernel(x, c, w_ada, b_ada, g_mix_pre, g_mix_post, g_ff_pre, g_ff_post, w_in, b_f, w_pool, pool_scale,
           conv_w, w_branch, w_out, w_ff1, w_ff2):
    nb, seq, d = x.shape
    depth = w_ada.shape[0]
    assert all(seq % tile == 0 for tile in (TM, TMT, TQ, TK)) and TMT % SUB == 0
    assert d % LANES == 0 and d % PREP_ROWS == 0
    t = nb * seq
    assert (6 * d) % (ADA_CHUNKS * LANES) == 0
    w_in_t = jnp.swapaxes(w_in, 1, 2)
    mod, w_in_bf = _ada(c, w_ada, b_ada, w_in_t)
    mod = mod.reshape(depth, nb, 6, 1, d)
    x2 = x.reshape(t, d)
    stacked = {"in": w_in_t, "branch": w_branch, "out": w_out, "ff1": w_ff1, "ff2": w_ff2}
    cast_keys = [("in", l) for l in range(1, depth)]
    cast_keys += [(name, l) for l in range(depth) for name in ("branch", "out", "ff1", "ff2")]
    casts = [(stacked[name], l, 0.5 if name == "branch" else 1.0) for name, l in cast_keys]
    bf16_w = {("in", 0): w_in_bf}
    spread = np.zeros((LANES, LANES), np.float32)
    head_of_lane = np.zeros((LANES,), np.int32)
    lane_used = np.zeros((LANES,), bool)
    for hd in range(HEADS):
        spread[_aug_base(hd):_aug_base(hd) + AUG, hd] = 1.0
        head_of_lane[_aug_base(hd):_aug_base(hd) + AUG] = hd
        lane_used[_aug_base(hd):_aug_base(hd) + AUG] = True
    spread = jnp.asarray(spread, BF16)
    bf_all = jnp.where(lane_used, b_f[:, head_of_lane], 0.0)
    for l in range(depth):
        bf = bf_all[l:l + 1]
        row = lambda a: a[l].reshape(1, -1)
        (q8, k8, v8, pu, cv, wgl), cast = _inproj(x2, mod[l], row(g_mix_pre), bf16_w[("in", l)], spread, bf, seq,
                                                  casts if l == 0 else [])
        if l == 0:
            bf16_w.update(zip(cast_keys, cast))
        o = _attention(q8, k8, v8, seq)
        x2 = _tail(o, pu, cv, x2, mod[l], row(g_mix_pre), wgl, _block_diag(w_pool[l]).astype(BF16),
                   row(pool_scale), conv_w[l], bf16_w[("branch", l)], bf16_w[("out", l)], row(g_mix_post),
                   row(g_ff_pre), bf16_w[("ff1", l)], bf16_w[("ff2", l)], row(g_ff_post), seq)
    return x2.reshape(nb, seq, d)
```

```python
import functools

import jax
import jax.numpy as jnp
import numpy as np
from jax import lax
from jax.experimental import pallas as pl
from jax.experimental.pallas import tpu as pltpu

F32 = jnp.float32
BF16 = jnp.bfloat16

LANES = 128
HEAD_DIM = 64
HEADS = 8
PAIRS = HEADS // 2
A_WIDTH = HEADS * HEAD_DIM
POOL_WINDOWS = (2, 4, 8, 16)
POOL_WIDTH = 256
CONV_WIDTH = 256
HALO = 16
RMS_EPS = 1e-6
NEG_INF = -1e30
LOG2E = 1.4426950408889634
AUG = 6
PV_ROWS = 128
ADA_CHUNKS = 8
VMEM_LIMIT = 56 * 1024 * 1024

TM = 512
TMT = 512
TQ = 512
TK = 512
FF_CHUNK = 1024
SUB = 512
PREP_ROWS = 128


def _const_spec(shape):
    n = len(shape)
    return pl.BlockSpec(shape, lambda *_: (0,) * n, pipeline_mode=pl.Buffered(1))


def _rms(x, g):
    ms = jnp.mean(x * x, axis=-1, keepdims=True)
    return x * lax.rsqrt(ms + RMS_EPS) * g


def _row_chunks(k, steps):
    rows = -(-(-(-k // steps)) // 16) * 16
    return rows, -(-k // rows)


def _nt_dot(a, b):
    return lax.dot_general(a, b, (((1,), (1,)), ((), ())), preferred_element_type=F32)


def _ada_kernel(c_ref, w_ref, b_ref, win_ref, o_ref, win_bf_ref):
    c = c_ref[...]
    sc = c * (1.0 / (1.0 + jnp.exp(-c)))
    w = w_ref[0]
    sc_hi, w_hi = sc.astype(BF16), w.astype(BF16)
    sc_lo = (sc - sc_hi.astype(F32)).astype(BF16)
    w_lo = (w - w_hi.astype(F32)).astype(BF16)
    dot = functools.partial(jnp.dot, preferred_element_type=F32)
    o_ref[0] = dot(sc_hi, w_hi) + dot(sc_hi, w_lo) + dot(sc_lo, w_hi) + b_ref[0]
    win_bf_ref[...] = win_ref[0].astype(BF16)


def _ada(c, w_ada, b_ada, w_in_t):
    depth, d, d6 = w_ada.shape
    nb = c.shape[0]
    n = ADA_CHUNKS
    cols = d6 // n
    rows, nblk = _row_chunks(w_in_t.shape[1], depth * n)
    assert nblk == depth * n
    ncol_in = w_in_t.shape[2]
    return pl.pallas_call(
        _ada_kernel,
        grid=(depth, n),
        in_specs=[
            pl.BlockSpec((nb, d), lambda l, j: (0, 0)),
            pl.BlockSpec((1, d, cols), lambda l, j: (l, 0, j)),
            pl.BlockSpec((1, 1, cols), lambda l, j: (l, 0, j)),
            pl.BlockSpec((1, rows, ncol_in), lambda l, j: (0, l * n + j, 0)),
        ],
        out_specs=(
            pl.BlockSpec((1, nb, cols), lambda l, j: (l, 0, j)),
            pl.BlockSpec((rows, ncol_in), lambda l, j: (l * n + j, 0)),
        ),
        out_shape=(
            jax.ShapeDtypeStruct((depth, nb, d6), F32),
            jax.ShapeDtypeStruct(w_in_t.shape[1:], BF16),
        ),
        compiler_params=pltpu.CompilerParams(
            dimension_semantics=("arbitrary", "arbitrary"), vmem_limit_bytes=VMEM_LIMIT),
        name="ada",
    )(c, w_ada, b_ada.reshape(depth, 1, d6), w_in_t)


def _aug_base(h):
    return (HEAD_DIM if h % 2 == 0 else 0) + AUG * (h // 2)


def _inproj_kernel(tiles_per_seq, cast_scales, x_ref, mod_ref, g_ref, w_ref, spread_ref, bf_ref, *refs):
    n_cast = len(cast_scales)
    cast_in = refs[:n_cast]
    q_ref, k_ref, v_ref, pu_ref, cv_ref, wgl_ref = refs[n_cast:n_cast + 6]
    cast_out = refs[n_cast + 6:2 * n_cast + 6]
    carry_ref, wf_ref, wrest_ref = refs[2 * n_cast + 6:]
    i = pl.program_id(0)
    d = x_ref.shape[1]

    for src, dst, s in zip(cast_in, cast_out, cast_scales):
        dst[...] = (src[0] if s == 1.0 else src[0] * s).astype(BF16)

    fo = 3 * A_WIDTH
    n_plain = POOL_WIDTH + 3 * CONV_WIDTH
    n_rest = n_plain + wgl_ref.shape[0]

    @pl.when(i == 0)
    def _():
        wf_ref[...] = jnp.dot(spread_ref[...], w_ref[fo:fo + LANES, :],
                              preferred_element_type=F32).astype(BF16)
        for r in range(0, n_rest, PREP_ROWS):
            end = min(fo + r + PREP_ROWS + 2 * HEADS, w_ref.shape[0])
            blk = w_ref[fo + r:end, :].astype(F32)[HEADS:HEADS + PREP_ROWS]
            if r < n_plain:
                wrest_ref[r:r + PREP_ROWS, :] = blk.astype(BF16)
            else:
                wgl_ref[r - n_plain:r - n_plain + PREP_ROWS, :] = (blk * 0.5).astype(BF16)

    shift, scale = mod_ref[0, 0], mod_ref[0, 1]
    hb = (_rms(x_ref[...], g_ref[...]) * (1.0 + scale) + shift).astype(BF16)
    tm = hb.shape[0]

    def proj(lo_, hi_):
        return _nt_dot(hb, w_ref[lo_:hi_, :])

    def rest(lo_, hi_):
        return _nt_dot(hb, wrest_ref[lo_:hi_, :])

    zf = _nt_dot(hb, wf_ref[...]) + bf_ref[...]
    pu_ref[...] = rest(0, POOL_WIDTH).astype(BF16)
    cv_ref[...] = rest(POOL_WIDTH, n_plain).astype(BF16)
    zq = proj(0, A_WIDTH) * (HEAD_DIM ** -0.5 * LOG2E)
    zk = proj(A_WIDTH, 2 * A_WIDTH)
    zv = proj(2 * A_WIDTH, 3 * A_WIDTH)

    lf = jnp.minimum(zf, 0.0) - jnp.log(1.0 + jnp.exp(-jnp.abs(zf)))
    row = lax.broadcasted_iota(jnp.int32, lf.shape, 0)
    step = 1
    while step < tm:
        lf = lf + jnp.where(row >= step, pltpu.roll(lf, step, 0), 0.0)
        step *= 2

    @pl.when(i % tiles_per_seq == 0)
    def _():
        carry_ref[...] = jnp.zeros_like(carry_ref)

    fc = lf + carry_ref[0:1, :]
    carry_ref[...] = jnp.broadcast_to(fc[tm - 1:tm, :], carry_ref.shape)

    f2 = fc * LOG2E
    hi = f2.astype(BF16).astype(F32)
    rem = f2 - hi
    mid = rem.astype(BF16).astype(F32)
    lo = rem - mid
    lane = lax.broadcasted_iota(jnp.int32, f2.shape, 1)
    j = (lane % HEAD_DIM) % AUG
    part = jnp.where(j % 3 == 0, hi, jnp.where(j % 3 == 1, mid, lo))
    aug_q = jnp.where(j < 3, 1.0, part)
    aug_k = jnp.where(j < 3, -part, 1.0)

    for hd in range(HEADS):
        p = hd // 2
        in_head = (lane >= (hd % 2) * HEAD_DIM) & (lane < (hd % 2 + 1) * HEAD_DIM)
        base = _aug_base(hd)
        in_aug = (lane >= base) & (lane < base + AUG)
        pair = slice(p * LANES, (p + 1) * LANES)
        q_ref[hd] = jnp.where(in_head, zq[:, pair], jnp.where(in_aug, aug_q, 0.0)).astype(BF16)
        k_ref[hd] = jnp.where(in_head, zk[:, pair], jnp.where(in_aug, aug_k, 0.0)).astype(BF16)
        ones_lane = HEAD_DIM if hd % 2 == 0 else HEAD_DIM - 1
        v_ref[hd] = jnp.where(in_head, zv[:, pair], jnp.where(lane == ones_lane, 1.0, 0.0)).astype(BF16)


def _inproj(x2, mod, g, w_in_bf, spread, bf, seq, casts):
    t, d = x2.shape
    nt = t // TM
    tiles_per_seq = seq // TM
    n_plain = POOL_WIDTH + 3 * CONV_WIDTH
    n_gate = w_in_bf.shape[0] - 3 * A_WIDTH - HEADS - n_plain
    cast_in_specs, cast_out_specs, cast_shapes = [], [], []
    for w, l, _ in casts:
        _, k, n = w.shape
        rows, nblk = _row_chunks(k, nt)
        cast_in_specs.append(pl.BlockSpec(
            (1, rows, n), functools.partial(lambda l_, nb_, i: (l_, jnp.minimum(i, nb_ - 1), 0), l, nblk)))
        cast_out_specs.append(pl.BlockSpec(
            (rows, n), functools.partial(lambda nb_, i: (jnp.minimum(i, nb_ - 1), 0), nblk)))
        cast_shapes.append(jax.ShapeDtypeStruct((k, n), BF16))
    out_shape = (
        jax.ShapeDtypeStruct((HEADS, t, LANES), BF16),
        jax.ShapeDtypeStruct((HEADS, t, LANES), BF16),
        jax.ShapeDtypeStruct((HEADS, t, LANES), BF16),
        jax.ShapeDtypeStruct((t, POOL_WIDTH), BF16),
        jax.ShapeDtypeStruct((t, 3 * CONV_WIDTH), BF16),
        jax.ShapeDtypeStruct((n_gate, d), BF16),
    )
    head_spec = pl.BlockSpec((HEADS, TM, LANES), lambda i: (0, i, 0))
    out_specs = (
        head_spec, head_spec, head_spec,
        pl.BlockSpec((TM, POOL_WIDTH), lambda i: (i, 0)),
        pl.BlockSpec((TM, 3 * CONV_WIDTH), lambda i: (i, 0)),
        pl.BlockSpec((n_gate, d), lambda i: (0, 0)),
    )
    outs = pl.pallas_call(
        functools.partial(_inproj_kernel, tiles_per_seq, tuple(s for _, _, s in casts)),
        grid=(nt,),
        in_specs=[
            pl.BlockSpec((TM, d), lambda i: (i, 0)),
            pl.BlockSpec((1, 2, 1, d), lambda i: (i // tiles_per_seq, 0, 0, 0)),
            _const_spec((1, d)),
            _const_spec(w_in_bf.shape),
            _const_spec(spread.shape),
            _const_spec((1, LANES)),
        ] + cast_in_specs,
        out_specs=out_specs + tuple(cast_out_specs),
        out_shape=out_shape + tuple(cast_shapes),
        scratch_shapes=[
            pltpu.VMEM((8, LANES), F32),
            pltpu.VMEM((LANES, d), BF16),
            pltpu.VMEM((n_plain, d), BF16),
        ],
        compiler_params=pltpu.CompilerParams(
            dimension_semantics=("arbitrary",), vmem_limit_bytes=VMEM_LIMIT),
        name="inproj",
    )(x2, mod, g, w_in_bf, spread, bf, *[w for w, _, _ in casts])
    return outs[:6], outs[6:]


def _attn_kernel(q_ref, k_ref, v_ref, o_ref):
    seq = q_ref.shape[1]
    nq = seq // TQ
    half = TK // 2
    steps = [(qi, kt) for qi in range(nq) for kt in range(qi + 1)]
    mask0 = (lax.broadcasted_iota(jnp.int32, (half, TQ), 0)
             <= lax.broadcasted_iota(jnp.int32, (half, TQ), 1))
    mask1 = (lax.broadcasted_iota(jnp.int32, (half, half), 0)
             <= lax.broadcasted_iota(jnp.int32, (half, half), 1))
    vts = [v_ref[hh].astype(F32).T.astype(BF16)[hh * (LANES - PV_ROWS):hh * (LANES - PV_ROWS) + PV_ROWS]
           for hh in range(2)]

    def logits(step, hh):
        qi, kt = step
        q = q_ref[hh, qi * TQ:(qi + 1) * TQ, :]
        if kt < qi:
            st = _nt_dot(k_ref[hh, kt * TK:(kt + 1) * TK, :], q)
            return (st,), jnp.max(st, axis=0, keepdims=True)
        st0 = jnp.where(mask0, _nt_dot(k_ref[hh, kt * TK:kt * TK + half, :], q), NEG_INF)
        st1 = jnp.where(mask1, _nt_dot(k_ref[hh, kt * TK + half:(kt + 1) * TK, :], q[half:]), NEG_INF)
        c0 = jnp.max(st0, axis=0, keepdims=True)
        c1 = jnp.max(st1, axis=0, keepdims=True)
        return (st0, st1), jnp.concatenate([c0[:, :half], jnp.maximum(c0[:, half:], c1)], axis=1)

    def update(step, hh, sts, cmax, m, acc):
        qi, kt = step
        vt = vts[hh]
        m_new = cmax if m is None else jnp.maximum(m, cmax)
        if kt < qi:
            pt = jnp.exp2(sts[0] - m_new).astype(BF16)
            pv = jnp.dot(vt[:, kt * TK:(kt + 1) * TK], pt, preferred_element_type=F32)
        else:
            pt0 = jnp.exp2(sts[0] - m_new).astype(BF16)
            pt1 = jnp.exp2(sts[1] - m_new[:, half:]).astype(BF16)
            pv = jnp.dot(vt[:, kt * TK:kt * TK + half], pt0, preferred_element_type=F32)
            pv1 = jnp.dot(vt[:, kt * TK + half:(kt + 1) * TK], pt1, preferred_element_type=F32)
            pv = jnp.concatenate([pv[:, :half], pv[:, half:] + pv1], axis=1)
        if m is not None:
            pv = jnp.exp2(m - m_new) * acc + pv
        return m_new, pv

    cur = [logits(steps[0], hh) for hh in range(2)]
    state = [(None, None), (None, None)]
    for s, step in enumerate(steps):
        qi, kt = step
        nxt = [None, None]
        for hh in range(2):
            if s + 1 < len(steps):
                nxt[hh] = logits(steps[s + 1], hh)
            state[hh] = update(step, hh, *cur[hh], *state[hh])
        cur = nxt
        if kt == qi:
            a0, a1 = state[0][1], state[1][1]
            pad = PV_ROWS - HEAD_DIM
            out = jnp.concatenate([a0[0:HEAD_DIM] / a0[HEAD_DIM:HEAD_DIM + 1],
                                   a1[pad:] / a1[pad - 1:pad]], axis=0)
            o_ref[qi * TQ:(qi + 1) * TQ, :] = out.T.astype(BF16)
            state = [(None, None), (None, None)]


def _attention(q8, k8, v8, seq):
    _, t, _ = q8.shape
    nb = t // seq
    pair_spec = pl.BlockSpec((2, seq, LANES), lambda b, p: (p, b, 0))
    return pl.pallas_call(
        _attn_kernel,
        grid=(nb, PAIRS),
        in_specs=[pair_spec, pair_spec, pair_spec],
        out_specs=pl.BlockSpec((seq, LANES), lambda b, p: (b, p)),
        out_shape=jax.ShapeDtypeStruct((t, A_WIDTH), BF16),
        compiler_params=pltpu.CompilerParams(
            dimension_semantics=("arbitrary", "arbitrary"), vmem_limit_bytes=VMEM_LIMIT),
        name="attn",
    )(q8, k8, v8)


def _tail_kernel(tiles_per_seq, o_ref, pu_ref, puh_ref, cv_ref, cvh_ref, x_ref, mod_ref,
                 gmix_ref, wgl_ref, wpool_ref, pscale_ref, convw_ref, wbr_ref, wout_ref, g_ref,
                 gpre_ref, w1_ref, w2_ref, gpost_ref, out_ref, x1_ref):
    i = pl.program_id(0)
    shift_m, scale_m, gate_m = mod_ref[0, 0], mod_ref[0, 1], mod_ref[0, 2]
    shift_f, scale_f, gate_f = mod_ref[0, 3], mod_ref[0, 4], mod_ref[0, 5]
    tile_in_seq = i % tiles_per_seq
    tm, d = x_ref.shape
    n_sub = tm // SUB
    lane = lax.broadcasted_iota(jnp.int32, (SUB + HALO, POOL_WIDTH), 1)
    row = lax.broadcasted_iota(jnp.int32, (SUB + HALO, POOL_WIDTH), 0)
    gd = POOL_WIDTH // len(POOL_WINDOWS)
    wsz = jnp.where(lane < gd, 2.0, jnp.where(lane < 2 * gd, 4.0, jnp.where(lane < 3 * gd, 8.0, 16.0)))
    cw = convw_ref[...]
    o1 = A_WIDTH + POOL_WIDTH

    def with_history(ref, halo_ref, j):
        if j == 0:
            head = jnp.where(tile_in_seq != 0, halo_ref[...].astype(F32), 0.0)
        else:
            head = ref[j * SUB - HALO:j * SUB, :].astype(F32)
        return jnp.concatenate([head, ref[j * SUB:(j + 1) * SUB, :].astype(F32)], axis=0)

    def mix(j):
        rows = slice(j * SUB, (j + 1) * SUB)
        ext = with_history(pu_ref, puh_ref, j)
        a2 = ext + pltpu.roll(ext, 1, 0)
        a4 = a2 + pltpu.roll(a2, 2, 0)
        a8 = a4 + pltpu.roll(a4, 4, 0)
        a16 = a8 + pltpu.roll(a8, 8, 0)
        win = jnp.where(lane < gd, a2, jnp.where(lane < 2 * gd, a4, jnp.where(lane < 3 * gd, a8, a16)))
        frames = (tile_in_seq * tm + j * SUB + row - (HALO - 1)).astype(F32)
        cnt = jnp.maximum(jnp.minimum(frames, wsz), 1.0)
        pm = (win / cnt - ext)[HALO:]
        br_b = jnp.dot(pm.astype(BF16), wpool_ref[...], preferred_element_type=F32) * pscale_ref[...]

        cve = with_history(cv_ref, cvh_ref, j)
        u = cve[:, 2 * CONV_WIDTH:] * cve[:, :CONV_WIDTH]
        y = cw[0:1] * pltpu.roll(u, 2, 0) + cw[1:2] * pltpu.roll(u, 1, 0) + cw[2:3] * u
        br_c = (cve[:, CONV_WIDTH:2 * CONV_WIDTH] * y)[HALO:]

        hb = (_rms(x_ref[rows, :], gmix_ref[...]) * (1.0 + scale_m) + shift_m).astype(BF16)

        def gated(n, branch, lo, hi):
            b = jnp.dot(branch, wbr_ref[lo:hi, :], preferred_element_type=F32)
            return b + b * jnp.tanh(_nt_dot(hb, wgl_ref[n * d:(n + 1) * d, :]))

        merged = gated(0, o_ref[rows, :], 0, A_WIDTH)
        merged += gated(1, br_b.astype(BF16), A_WIDTH, o1)
        merged += gated(2, br_c.astype(BF16), o1, o1 + CONV_WIDTH)
        return merged.astype(BF16)

    def project(j, merged):
        rows = slice(j * SUB, (j + 1) * SUB)
        yo = jnp.dot(merged, wout_ref[...], preferred_element_type=F32)
        x1_ref[rows, :] = x_ref[rows, :] + gate_m * _rms(yo, g_ref[...])

    cur = mix(0)
    for j in range(n_sub):
        nxt = mix(j + 1) if j + 1 < n_sub else None
        project(j, cur)
        cur = nxt

    x1 = x1_ref[...]
    hb = (_rms(x1, gpre_ref[...]) * (1.0 + scale_f) + shift_f).astype(BF16)
    dff = w1_ref.shape[1]
    acc = jnp.zeros(x1.shape, F32)
    for c in range(dff // FF_CHUNK):
        up = jnp.dot(hb, w1_ref[:, c * FF_CHUNK:(c + 1) * FF_CHUNK], preferred_element_type=F32)
        act = jnp.square(jnp.maximum(up, 0.0)).astype(BF16)
        acc += jnp.dot(act, w2_ref[c * FF_CHUNK:(c + 1) * FF_CHUNK, :], preferred_element_type=F32)
    out_ref[...] = x1 + gate_f * _rms(acc, gpost_ref[...])


def _tail(o, pu, cv, x2, mod, g_mix_pre, wgl_bf, wpool, pscale, convw, wbr_bf, wout_bf, g_mix_post, g_ff_pre,
          w1_bf, w2_bf, g_ff_post, seq):
    t, d = x2.shape
    nt = t // TMT
    tiles_per_seq = seq // TMT
    hb = TMT // HALO

    def halo_map(i):
        return (jnp.maximum(i * hb - 1, 0), 0)

    consts = (g_mix_pre, wgl_bf, wpool, pscale, convw, wbr_bf, wout_bf, g_mix_post, g_ff_pre, w1_bf, w2_bf,
              g_ff_post)
    return pl.pallas_call(
        functools.partial(_tail_kernel, tiles_per_seq),
        grid=(nt,),
        in_specs=[
            pl.BlockSpec((TMT, A_WIDTH), lambda i: (i, 0)),
            pl.BlockSpec((TMT, POOL_WIDTH), lambda i: (i, 0)),
            pl.BlockSpec((HALO, POOL_WIDTH), halo_map),
            pl.BlockSpec((TMT, 3 * CONV_WIDTH), lambda i: (i, 0)),
            pl.BlockSpec((HALO, 3 * CONV_WIDTH), halo_map),
            pl.BlockSpec((TMT, d), lambda i: (i, 0)),
            pl.BlockSpec((1,) + mod.shape[1:], lambda i: (i // tiles_per_seq, 0, 0, 0)),
        ] + [_const_spec(a.shape) for a in consts],
        out_specs=pl.BlockSpec((TMT, d), lambda i: (i, 0)),
        out_shape=jax.ShapeDtypeStruct((t, d), F32),
        scratch_shapes=[pltpu.VMEM((TMT, d), F32)],
        compiler_params=pltpu.CompilerParams(
            dimension_semantics=("arbitrary",), vmem_limit_bytes=VMEM_LIMIT),
        name="tail",
    )(o, pu, pu, cv, cv, x2, mod, *consts)


def _block_diag(w):
    g, c, dd = w.shape
    eye = jnp.eye(g, dtype=w.dtype)
    return (eye[:, None, :, None] * w[:, :, None, :]).reshape(g * c, g * dd)


def kernel(x, c, w_ada, b_ada, g_mix_pre, g_mix_post, g_ff_pre, g_ff_post, w_in, b_f, w_pool, pool_scale,
           conv_w, w_branch, w_out, w_ff1, w_ff2):
    nb, seq, d = x.shape
    depth = w_ada.shape[0]
    assert all(seq % tile == 0 for tile in (TM, TMT, TQ, TK)) and TMT % SUB == 0
    assert d % LANES == 0 and d % PREP_ROWS == 0
    t = nb * seq
    assert (6 * d) % (ADA_CHUNKS * LANES) == 0
    w_in_t = jnp.swapaxes(w_in, 1, 2)
    mod, w_in_bf = _ada(c, w_ada, b_ada, w_in_t)
    mod = mod.reshape(depth, nb, 6, 1, d)
    x2 = x.reshape(t, d)
    stacked = {"in": w_in_t, "branch": w_branch, "out": w_out, "ff1": w_ff1, "ff2": w_ff2}
    cast_keys = [("in", l) for l in range(1, depth)]
    cast_keys += [(name, l) for l in range(depth) for name in ("branch", "out", "ff1", "ff2")]
    casts = [(stacked[name], l, 0.5 if name == "branch" else 1.0) for name, l in cast_keys]
    bf16_w = {("in", 0): w_in_bf}
    spread = np.zeros((LANES, LANES), np.float32)
    head_of_lane = np.zeros((LANES,), np.int32)
    lane_used = np.zeros((LANES,), bool)
    for hd in range(HEADS):
        spread[_aug_base(hd):_aug_base(hd) + AUG, hd] = 1.0
        head_of_lane[_aug_base(hd):_aug_base(hd) + AUG] = hd
        lane_used[_aug_base(hd):_aug_base(hd) + AUG] = True
    spread = jnp.asarray(spread, BF16)
    bf_all = jnp.where(lane_used, b_f[:, head_of_lane], 0.0)
    for l in range(depth):
        bf = bf_all[l:l + 1]
        row = lambda a: a[l].reshape(1, -1)
        (q8, k8, v8, pu, cv, wgl), cast = _inproj(x2, mod[l], row(g_mix_pre), bf16_w[("in", l)], spread, bf, seq,
                                                  casts if l == 0 else [])
        if l == 0:
            bf16_w.update(zip(cast_keys, cast))
        o = _attention(q8, k8, v8, seq)
        x2 = _tail(o, pu, cv, x2, mod[l], row(g_mix_pre), wgl, _block_diag(w_pool[l]).astype(BF16),
                   row(pool_scale), conv_w[l], bf16_w[("branch", l)], bf16_w[("out", l)], row(g_mix_post),
                   row(g_ff_pre), bf16_w[("ff1", l)], bf16_w[("ff2", l)], row(g_ff_post), seq)
    return x2.reshape(nb, seq, d)
```

```python
import functools

import jax
import jax.numpy as jnp
import numpy as np
from jax import lax
from jax.experimental import pallas as pl
from jax.experimental.pallas import tpu as pltpu

F32 = jnp.float32
BF16 = jnp.bfloat16

LANES = 128
HEAD_DIM = 64
HEADS = 8
PAIRS = HEADS // 2
A_WIDTH = HEADS * HEAD_DIM
POOL_WINDOWS = (2, 4, 8, 16)
POOL_WIDTH = 256
CONV_WIDTH = 256
HALO = 16
RMS_EPS = 1e-6
NEG_INF = -1e30
LOG2E = 1.4426950408889634
AUG = 6
PV_ROWS = 128
ADA_CHUNKS = 4
VMEM_LIMIT = 56 * 1024 * 1024

TM = 512
TMT = 512
TQ = 512
TK = 512
FF_CHUNK = 1024
SUB = 256
PREP_ROWS = 128


def _const_spec(shape):
    n = len(shape)
    return pl.BlockSpec(shape, lambda *_: (0,) * n, pipeline_mode=pl.Buffered(1))


def _rms(x, g):
    ms = jnp.mean(x * x, axis=-1, keepdims=True)
    return x * lax.rsqrt(ms + RMS_EPS) * g


def _row_chunks(k, steps):
    rows = -(-(-(-k // steps)) // 16) * 16
    return rows, -(-k // rows)


def _nt_dot(a, b):
    return lax.dot_general(a, b, (((1,), (1,)), ((), ())), preferred_element_type=F32)


def _ada_kernel(c_ref, w_ref, b_ref, win_ref, o_ref, win_bf_ref):
    c = c_ref[...]
    sc = c * (1.0 / (1.0 + jnp.exp(-c)))
    w = w_ref[0]
    sc_hi, w_hi = sc.astype(BF16), w.astype(BF16)
    sc_lo = (sc - sc_hi.astype(F32)).astype(BF16)
    w_lo = (w - w_hi.astype(F32)).astype(BF16)
    dot = functools.partial(jnp.dot, preferred_element_type=F32)
    o_ref[0] = dot(sc_hi, w_hi) + dot(sc_hi, w_lo) + dot(sc_lo, w_hi) + b_ref[0]
    win_bf_ref[...] = win_ref[0].astype(BF16)


def _ada(c, w_ada, b_ada, w_in_t):
    depth, d, d6 = w_ada.shape
    nb = c.shape[0]
    n = ADA_CHUNKS
    cols = d6 // n
    rows, nblk = _row_chunks(w_in_t.shape[1], depth * n)
    assert nblk == depth * n
    ncol_in = w_in_t.shape[2]
    return pl.pallas_call(
        _ada_kernel,
        grid=(depth, n),
        in_specs=[
            pl.BlockSpec((nb, d), lambda l, j: (0, 0)),
            pl.BlockSpec((1, d, cols), lambda l, j: (l, 0, j)),
            pl.BlockSpec((1, 1, cols), lambda l, j: (l, 0, j)),
            pl.BlockSpec((1, rows, ncol_in), lambda l, j: (0, l * n + j, 0)),
        ],
        out_specs=(
            pl.BlockSpec((1, nb, cols), lambda l, j: (l, 0, j)),
            pl.BlockSpec((rows, ncol_in), lambda l, j: (l * n + j, 0)),
        ),
        out_shape=(
            jax.ShapeDtypeStruct((depth, nb, d6), F32),
            jax.ShapeDtypeStruct(w_in_t.shape[1:], BF16),
        ),
        compiler_params=pltpu.CompilerParams(
            dimension_semantics=("arbitrary", "arbitrary"), vmem_limit_bytes=VMEM_LIMIT),
        name="ada",
    )(c, w_ada, b_ada.reshape(depth, 1, d6), w_in_t)


def _aug_base(h):
    return (HEAD_DIM if h % 2 == 0 else 0) + AUG * (h // 2)


def _inproj_kernel(tiles_per_seq, cast_scales, x_ref, mod_ref, g_ref, w_ref, spread_ref, bf_ref, *refs):
    n_cast = len(cast_scales)
    cast_in = refs[:n_cast]
    q_ref, k_ref, v_ref, pu_ref, cv_ref, wgl_ref = refs[n_cast:n_cast + 6]
    cast_out = refs[n_cast + 6:2 * n_cast + 6]
    carry_ref, wf_ref, wrest_ref = refs[2 * n_cast + 6:]
    i = pl.program_id(0)
    d = x_ref.shape[1]

    for src, dst, s in zip(cast_in, cast_out, cast_scales):
        dst[...] = (src[0] if s == 1.0 else src[0] * s).astype(BF16)

    fo = 3 * A_WIDTH
    n_plain = POOL_WIDTH + 3 * CONV_WIDTH
    n_rest = n_plain + wgl_ref.shape[0]

    @pl.when(i == 0)
    def _():
        wf_ref[...] = jnp.dot(spread_ref[...], w_ref[fo:fo + LANES, :],
                              preferred_element_type=F32).astype(BF16)
        for r in range(0, n_rest, PREP_ROWS):
            end = min(fo + r + PREP_ROWS + 2 * HEADS, w_ref.shape[0])
            blk = w_ref[fo + r:end, :].astype(F32)[HEADS:HEADS + PREP_ROWS]
            if r < n_plain:
                wrest_ref[r:r + PREP_ROWS, :] = blk.astype(BF16)
            else:
                wgl_ref[r - n_plain:r - n_plain + PREP_ROWS, :] = (blk * 0.5).astype(BF16)

    shift, scale = mod_ref[0, 0], mod_ref[0, 1]
    hb = (_rms(x_ref[...], g_ref[...]) * (1.0 + scale) + shift).astype(BF16)
    tm = hb.shape[0]

    def proj(lo_, hi_):
        return _nt_dot(hb, w_ref[lo_:hi_, :])

    def rest(lo_, hi_):
        return _nt_dot(hb, wrest_ref[lo_:hi_, :])

    zf = _nt_dot(hb, wf_ref[...]) + bf_ref[...]
    pu_ref[...] = rest(0, POOL_WIDTH).astype(BF16)
    cv_ref[...] = rest(POOL_WIDTH, n_plain).astype(BF16)
    zq = proj(0, A_WIDTH) * (HEAD_DIM ** -0.5 * LOG2E)
    zk = proj(A_WIDTH, 2 * A_WIDTH)
    zv = proj(2 * A_WIDTH, 3 * A_WIDTH)

    lf = jnp.minimum(zf, 0.0) - jnp.log(1.0 + jnp.exp(-jnp.abs(zf)))
    row = lax.broadcasted_iota(jnp.int32, lf.shape, 0)
    step = 1
    while step < tm:
        lf = lf + jnp.where(row >= step, pltpu.roll(lf, step, 0), 0.0)
        step *= 2

    @pl.when(i % tiles_per_seq == 0)
    def _():
        carry_ref[...] = jnp.zeros_like(carry_ref)

    fc = lf + carry_ref[0:1, :]
    carry_ref[...] = jnp.broadcast_to(fc[tm - 1:tm, :], carry_ref.shape)

    f2 = fc * LOG2E
    hi = f2.astype(BF16).astype(F32)
    rem = f2 - hi
    mid = rem.astype(BF16).astype(F32)
    lo = rem - mid
    lane = lax.broadcasted_iota(jnp.int32, f2.shape, 1)
    j = (lane % HEAD_DIM) % AUG
    part = jnp.where(j % 3 == 0, hi, jnp.where(j % 3 == 1, mid, lo))
    aug_q = jnp.where(j < 3, 1.0, part)
    aug_k = jnp.where(j < 3, -part, 1.0)

    for hd in range(HEADS):
        p = hd // 2
        in_head = (lane >= (hd % 2) * HEAD_DIM) & (lane < (hd % 2 + 1) * HEAD_DIM)
        base = _aug_base(hd)
        in_aug = (lane >= base) & (lane < base + AUG)
        pair = slice(p * LANES, (p + 1) * LANES)
        q_ref[hd] = jnp.where(in_head, zq[:, pair], jnp.where(in_aug, aug_q, 0.0)).astype(BF16)
        k_ref[hd] = jnp.where(in_head, zk[:, pair], jnp.where(in_aug, aug_k, 0.0)).astype(BF16)
        ones_lane = HEAD_DIM if hd % 2 == 0 else HEAD_DIM - 1
        v_ref[hd] = jnp.where(in_head, zv[:, pair], jnp.where(lane == ones_lane, 1.0, 0.0)).astype(BF16)


def _inproj(x2, mod, g, w_in_bf, spread, bf, seq, casts):
    t, d = x2.shape
    nt = t // TM
    tiles_per_seq = seq // TM
    n_plain = POOL_WIDTH + 3 * CONV_WIDTH
    n_gate = w_in_bf.shape[0] - 3 * A_WIDTH - HEADS - n_plain
    cast_in_specs, cast_out_specs, cast_shapes = [], [], []
    for w, l, _ in casts:
        _, k, n = w.shape
        rows, nblk = _row_chunks(k, nt)
        cast_in_specs.append(pl.BlockSpec(
            (1, rows, n), functools.partial(lambda l_, nb_, i: (l_, jnp.minimum(i, nb_ - 1), 0), l, nblk)))
        cast_out_specs.append(pl.BlockSpec(
            (rows, n), functools.partial(lambda nb_, i: (jnp.minimum(i, nb_ - 1), 0), nblk)))
        cast_shapes.append(jax.ShapeDtypeStruct((k, n), BF16))
    out_shape = (
        jax.ShapeDtypeStruct((HEADS, t, LANES), BF16),
        jax.ShapeDtypeStruct((HEADS, t, LANES), BF16),
        jax.ShapeDtypeStruct((HEADS, t, LANES), BF16),
        jax.ShapeDtypeStruct((t, POOL_WIDTH), BF16),
        jax.ShapeDtypeStruct((t, 3 * CONV_WIDTH), BF16),
        jax.ShapeDtypeStruct((n_gate, d), BF16),
    )
    head_spec = pl.BlockSpec((HEADS, TM, LANES), lambda i: (0, i, 0))
    out_specs = (
        head_spec, head_spec, head_spec,
        pl.BlockSpec((TM, POOL_WIDTH), lambda i: (i, 0)),
        pl.BlockSpec((TM, 3 * CONV_WIDTH), lambda i: (i, 0)),
        pl.BlockSpec((n_gate, d), lambda i: (0, 0)),
    )
    outs = pl.pallas_call(
        functools.partial(_inproj_kernel, tiles_per_seq, tuple(s for _, _, s in casts)),
        grid=(nt,),
        in_specs=[
            pl.BlockSpec((TM, d), lambda i: (i, 0)),
            pl.BlockSpec((1, 2, 1, d), lambda i: (i // tiles_per_seq, 0, 0, 0)),
            _const_spec((1, d)),
            _const_spec(w_in_bf.shape),
            _const_spec(spread.shape),
            _const_spec((1, LANES)),
        ] + cast_in_specs,
        out_specs=out_specs + tuple(cast_out_specs),
        out_shape=out_shape + tuple(cast_shapes),
        scratch_shapes=[
            pltpu.VMEM((8, LANES), F32),
            pltpu.VMEM((LANES, d), BF16),
            pltpu.VMEM((n_plain, d), BF16),
        ],
        compiler_params=pltpu.CompilerParams(
            dimension_semantics=("arbitrary",), vmem_limit_bytes=VMEM_LIMIT),
        name="inproj",
    )(x2, mod, g, w_in_bf, spread, bf, *[w for w, _, _ in casts])
    return outs[:6], outs[6:]


def _attn_kernel(q_ref, k_ref, v_ref, o_ref):
    seq = q_ref.shape[1]
    nq = seq // TQ
    half = TK // 2
    steps = [(qi, kt) for qi in range(nq) for kt in range(qi + 1)]
    mask0 = (lax.broadcasted_iota(jnp.int32, (half, TQ), 0)
             <= lax.broadcasted_iota(jnp.int32, (half, TQ), 1))
    mask1 = (lax.broadcasted_iota(jnp.int32, (half, half), 0)
             <= lax.broadcasted_iota(jnp.int32, (half, half), 1))
    vts = [v_ref[hh].astype(F32).T.astype(BF16)[hh * (LANES - PV_ROWS):hh * (LANES - PV_ROWS) + PV_ROWS]
           for hh in range(2)]

    def logits(step, hh):
        qi, kt = step
        q = q_ref[hh, qi * TQ:(qi + 1) * TQ, :]
        if kt < qi:
            st = _nt_dot(k_ref[hh, kt * TK:(kt + 1) * TK, :], q)
            return (st,), jnp.max(st, axis=0, keepdims=True)
        st0 = jnp.where(mask0, _nt_dot(k_ref[hh, kt * TK:kt * TK + half, :], q), NEG_INF)
        st1 = jnp.where(mask1, _nt_dot(k_ref[hh, kt * TK + half:(kt + 1) * TK, :], q[half:]), NEG_INF)
        c0 = jnp.max(st0, axis=0, keepdims=True)
        c1 = jnp.max(st1, axis=0, keepdims=True)
        return (st0, st1), jnp.concatenate([c0[:, :half], jnp.maximum(c0[:, half:], c1)], axis=1)

    def update(step, hh, sts, cmax, m, acc):
        qi, kt = step
        vt = vts[hh]
        m_new = cmax if m is None else jnp.maximum(m, cmax)
        if kt < qi:
            pt = jnp.exp2(sts[0] - m_new).astype(BF16)
            pv = jnp.dot(vt[:, kt * TK:(kt + 1) * TK], pt, preferred_element_type=F32)
        else:
            pt0 = jnp.exp2(sts[0] - m_new).astype(BF16)
            pt1 = jnp.exp2(sts[1] - m_new[:, half:]).astype(BF16)
            pv = jnp.dot(vt[:, kt * TK:kt * TK + half], pt0, preferred_element_type=F32)
            pv1 = jnp.dot(vt[:, kt * TK + half:(kt + 1) * TK], pt1, preferred_element_type=F32)
            pv = jnp.concatenate([pv[:, :half], pv[:, half:] + pv1], axis=1)
        if m is not None:
            pv = jnp.exp2(m - m_new) * acc + pv
        return m_new, pv

    cur = [logits(steps[0], hh) for hh in range(2)]
    state = [(None, None), (None, None)]
    for s, step in enumerate(steps):
        qi, kt = step
        nxt = [None, None]
        for hh in range(2):
            if s + 1 < len(steps):
                nxt[hh] = logits(steps[s + 1], hh)
            state[hh] = update(step, hh, *cur[hh], *state[hh])
        cur = nxt
        if kt == qi:
            a0, a1 = state[0][1], state[1][1]
            pad = PV_ROWS - HEAD_DIM
            out = jnp.concatenate([a0[0:HEAD_DIM] / a0[HEAD_DIM:HEAD_DIM + 1],
                                   a1[pad:] / a1[pad - 1:pad]], axis=0)
            o_ref[qi * TQ:(qi + 1) * TQ, :] = out.T.astype(BF16)
            state = [(None, None), (None, None)]


def _attention(q8, k8, v8, seq):
    _, t, _ = q8.shape
    nb = t // seq
    pair_spec = pl.BlockSpec((2, seq, LANES), lambda b, p: (p, b, 0))
    return pl.pallas_call(
        _attn_kernel,
        grid=(nb, PAIRS),
        in_specs=[pair_spec, pair_spec, pair_spec],
        out_specs=pl.BlockSpec((seq, LANES), lambda b, p: (b, p)),
        out_shape=jax.ShapeDtypeStruct((t, A_WIDTH), BF16),
        compiler_params=pltpu.CompilerParams(
            dimension_semantics=("arbitrary", "arbitrary"), vmem_limit_bytes=VMEM_LIMIT),
        name="attn",
    )(q8, k8, v8)


def _tail_kernel(tiles_per_seq, o_ref, pu_ref, puh_ref, cv_ref, cvh_ref, x_ref, mod_ref,
                 gmix_ref, wgl_ref, wpool_ref, pscale_ref, convw_ref, wbr_ref, wout_ref, g_ref,
                 gpre_ref, w1_ref, w2_ref, gpost_ref, out_ref, x1_ref):
    i = pl.program_id(0)
    shift_m, scale_m, gate_m = mod_ref[0, 0], mod_ref[0, 1], mod_ref[0, 2]
    shift_f, scale_f, gate_f = mod_ref[0, 3], mod_ref[0, 4], mod_ref[0, 5]
    tile_in_seq = i % tiles_per_seq
    tm, d = x_ref.shape
    n_sub = tm // SUB
    lane = lax.broadcasted_iota(jnp.int32, (SUB + HALO, POOL_WIDTH), 1)
    row = lax.broadcasted_iota(jnp.int32, (SUB + HALO, POOL_WIDTH), 0)
    gd = POOL_WIDTH // len(POOL_WINDOWS)
    wsz = jnp.where(lane < gd, 2.0, jnp.where(lane < 2 * gd, 4.0, jnp.where(lane < 3 * gd, 8.0, 16.0)))
    cw = convw_ref[...]
    o1 = A_WIDTH + POOL_WIDTH

    def with_history(ref, halo_ref, j):
        if j == 0:
            head = jnp.where(tile_in_seq != 0, halo_ref[...].astype(F32), 0.0)
        else:
            head = ref[j * SUB - HALO:j * SUB, :].astype(F32)
        return jnp.concatenate([head, ref[j * SUB:(j + 1) * SUB, :].astype(F32)], axis=0)

    def mix(j):
        rows = slice(j * SUB, (j + 1) * SUB)
        ext = with_history(pu_ref, puh_ref, j)
        a2 = ext + pltpu.roll(ext, 1, 0)
        a4 = a2 + pltpu.roll(a2, 2, 0)
        a8 = a4 + pltpu.roll(a4, 4, 0)
        a16 = a8 + pltpu.roll(a8, 8, 0)
        win = jnp.where(lane < gd, a2, jnp.where(lane < 2 * gd, a4, jnp.where(lane < 3 * gd, a8, a16)))
        frames = (tile_in_seq * tm + j * SUB + row - (HALO - 1)).astype(F32)
        cnt = jnp.maximum(jnp.minimum(frames, wsz), 1.0)
        pm = (win / cnt - ext)[HALO:]
        br_b = jnp.dot(pm.astype(BF16), wpool_ref[...], preferred_element_type=F32) * pscale_ref[...]

        cve = with_history(cv_ref, cvh_ref, j)
        u = cve[:, 2 * CONV_WIDTH:] * cve[:, :CONV_WIDTH]
        y = cw[0:1] * pltpu.roll(u, 2, 0) + cw[1:2] * pltpu.roll(u, 1, 0) + cw[2:3] * u
        br_c = (cve[:, CONV_WIDTH:2 * CONV_WIDTH] * y)[HALO:]

        hb = (_rms(x_ref[rows, :], gmix_ref[...]) * (1.0 + scale_m) + shift_m).astype(BF16)

        def gated(n, branch, lo, hi):
            b = jnp.dot(branch, wbr_ref[lo:hi, :], preferred_element_type=F32)
            return b + b * jnp.tanh(_nt_dot(hb, wgl_ref[n * d:(n + 1) * d, :]))

        merged = gated(0, o_ref[rows, :], 0, A_WIDTH)
        merged += gated(1, br_b.astype(BF16), A_WIDTH, o1)
        merged += gated(2, br_c.astype(BF16), o1, o1 + CONV_WIDTH)
        return merged.astype(BF16)

    def project(j, merged):
        rows = slice(j * SUB, (j + 1) * SUB)
        yo = jnp.dot(merged, wout_ref[...], preferred_element_type=F32)
        x1_ref[rows, :] = x_ref[rows, :] + gate_m * _rms(yo, g_ref[...])

    cur = mix(0)
    for j in range(n_sub):
        nxt = mix(j + 1) if j + 1 < n_sub else None
        project(j, cur)
        cur = nxt

    x1 = x1_ref[...]
    hb = (_rms(x1, gpre_ref[...]) * (1.0 + scale_f) + shift_f).astype(BF16)
    dff = w1_ref.shape[1]
    acc = jnp.zeros(x1.shape, F32)
    for c in range(dff // FF_CHUNK):
        up = jnp.dot(hb, w1_ref[:, c * FF_CHUNK:(c + 1) * FF_CHUNK], preferred_element_type=F32)
        act = jnp.square(jnp.maximum(up, 0.0)).astype(BF16)
        acc += jnp.dot(act, w2_ref[c * FF_CHUNK:(c + 1) * FF_CHUNK, :], preferred_element_type=F32)
    out_ref[...] = x1 + gate_f * _rms(acc, gpost_ref[...])


def _tail(o, pu, cv, x2, mod, g_mix_pre, wgl_bf, wpool, pscale, convw, wbr_bf, wout_bf, g_mix_post, g_ff_pre,
          w1_bf, w2_bf, g_ff_post, seq):
    t, d = x2.shape
    nt = t // TMT
    tiles_per_seq = seq // TMT
    hb = TMT // HALO

    def halo_map(i):
        return (jnp.maximum(i * hb - 1, 0), 0)

    consts = (g_mix_pre, wgl_bf, wpool, pscale, convw, wbr_bf, wout_bf, g_mix_post, g_ff_pre, w1_bf, w2_bf,
              g_ff_post)
    return pl.pallas_call(
        functools.partial(_tail_kernel, tiles_per_seq),
        grid=(nt,),
        in_specs=[
            pl.BlockSpec((TMT, A_WIDTH), lambda i: (i, 0)),
            pl.BlockSpec((TMT, POOL_WIDTH), lambda i: (i, 0)),
            pl.BlockSpec((HALO, POOL_WIDTH), halo_map),
            pl.BlockSpec((TMT, 3 * CONV_WIDTH), lambda i: (i, 0)),
            pl.BlockSpec((HALO, 3 * CONV_WIDTH), halo_map),
            pl.BlockSpec((TMT, d), lambda i: (i, 0)),
            pl.BlockSpec((1,) + mod.shape[1:], lambda i: (i // tiles_per_seq, 0, 0, 0)),
        ] + [_const_spec(a.shape) for a in consts],
        out_specs=pl.BlockSpec((TMT, d), lambda i: (i, 0)),
        out_shape=jax.ShapeDtypeStruct((t, d), F32),
        scratch_shapes=[pltpu.VMEM((TMT, d), F32)],
        compiler_params=pltpu.CompilerParams(
            dimension_semantics=("arbitrary",), vmem_limit_bytes=VMEM_LIMIT),
        name="tail",
    )(o, pu, pu, cv, cv, x2, mod, *consts)


def _block_diag(w):
    g, c, dd = w.shape
    eye = jnp.eye(g, dtype=w.dtype)
    return (eye[:, None, :, None] * w[:, :, None, :]).reshape(g * c, g * dd)


def kernel(x, c, w_ada, b_ada, g_mix_pre, g_mix_post, g_ff_pre, g_ff_post, w_in, b_f, w_pool, pool_scale,
           conv_w, w_branch, w_out, w_ff1, w_ff2):
    nb, seq, d = x.shape
    depth = w_ada.shape[0]
    assert all(seq % tile == 0 for tile in (TM, TMT, TQ, TK)) and TMT % SUB == 0
    assert d % LANES == 0 and d % PREP_ROWS == 0
    t = nb * seq
    assert (6 * d) % (ADA_CHUNKS * LANES) == 0
    w_in_t = jnp.swapaxes(w_in, 1, 2)
    mod, w_in_bf = _ada(c, w_ada, b_ada, w_in_t)
    mod = mod.reshape(depth, nb, 6, 1, d)
    x2 = x.reshape(t, d)
    stacked = {"in": w_in_t, "branch": w_branch, "out": w_out, "ff1": w_ff1, "ff2": w_ff2}
    cast_keys = [("in", l) for l in range(1, depth)]
    cast_keys += [(name, l) for l in range(depth) for name in ("branch", "out", "ff1", "ff2")]
    casts = [(stacked[name], l, 0.5 if name == "branch" else 1.0) for name, l in cast_keys]
    bf16_w = {("in", 0): w_in_bf}
    spread = np.zeros((LANES, LANES), np.float32)
    head_of_lane = np.zeros((LANES,), np.int32)
    lane_used = np.zeros((LANES,), bool)
    for hd in range(HEADS):
        spread[_aug_base(hd):_aug_base(hd) + AUG, hd] = 1.0
        head_of_lane[_aug_base(hd):_aug_base(hd) + AUG] = hd
        lane_used[_aug_base(hd):_aug_base(hd) + AUG] = True
    spread = jnp.asarray(spread, BF16)
    bf_all = jnp.where(lane_used, b_f[:, head_of_lane], 0.0)
    for l in range(depth):
        bf = bf_all[l:l + 1]
        row = lambda a: a[l].reshape(1, -1)
        (q8, k8, v8, pu, cv, wgl), cast = _inproj(x2, mod[l], row(g_mix_pre), bf16_w[("in", l)], spread, bf, seq,
                                                  casts if l == 0 else [])
        if l == 0:
            bf16_w.update(zip(cast_keys, cast))
        o = _attention(q8, k8, v8, seq)
        x2 = _tail(o, pu, cv, x2, mod[l], row(g_mix_pre), wgl, _block_diag(w_pool[l]).astype(BF16),
                   row(pool_scale), conv_w[l], bf16_w[("branch", l)], bf16_w[("out", l)], row(g_mix_post),
                   row(g_ff_pre), bf16_w[("ff1", l)], bf16_w[("ff2", l)], row(g_ff_post), seq)
    return x2.reshape(nb, seq, d)
```

```python
import functools

import jax
import jax.numpy as jnp
import numpy as np
from jax import lax
from jax.experimental import pallas as pl
from jax.experimental.pallas import tpu as pltpu

F32 = jnp.float32
BF16 = jnp.bfloat16

LANES = 128
HEAD_DIM = 64
HEADS = 8
PAIRS = HEADS // 2
A_WIDTH = HEADS * HEAD_DIM
POOL_WINDOWS = (2, 4, 8, 16)
POOL_WIDTH = 256
CONV_WIDTH = 256
HALO = 16
RMS_EPS = 1e-6
NEG_INF = -1e30
LOG2E = 1.4426950408889634
AUG = 6
PV_ROWS = 128
ADA_CHUNKS = 4
VMEM_LIMIT = 56 * 1024 * 1024

TM = 512
TMT = 512
TQ = 512
TK = 512
FF_CHUNK = 1024
SUB = 256
PREP_ROWS = 128


def _const_spec(shape):
    n = len(shape)
    return pl.BlockSpec(shape, lambda *_: (0,) * n, pipeline_mode=pl.Buffered(1))


def _layer_spec(a, l):
    n = a.ndim - 1
    return pl.BlockSpec((None,) + a.shape[1:], lambda *_: (l,) + (0,) * n, pipeline_mode=pl.Buffered(1))


def _rms(x, g):
    ms = jnp.mean(x * x, axis=-1, keepdims=True)
    return x * lax.rsqrt(ms + RMS_EPS) * g


def _row_chunks(k, steps):
    rows = -(-(-(-k // steps)) // 16) * 16
    return rows, -(-k // rows)


def _nt_dot(a, b):
    return lax.dot_general(a, b, (((1,), (1,)), ((), ())), preferred_element_type=F32)


def _ada_kernel(c_ref, w_ref, b_ref, win_ref, o_ref, win_bf_ref):
    c = c_ref[...]
    sc = c * (1.0 / (1.0 + jnp.exp(-c)))
    w = w_ref[0]
    sc_hi, w_hi = sc.astype(BF16), w.astype(BF16)
    sc_lo = (sc - sc_hi.astype(F32)).astype(BF16)
    w_lo = (w - w_hi.astype(F32)).astype(BF16)
    dot = functools.partial(jnp.dot, preferred_element_type=F32)
    o_ref[0] = dot(sc_hi, w_hi) + dot(sc_hi, w_lo) + dot(sc_lo, w_hi) + b_ref[0]
    win_bf_ref[...] = win_ref[0].astype(BF16)


def _ada(c, w_ada, b_ada, w_in_t):
    depth, d, d6 = w_ada.shape
    nb = c.shape[0]
    n = ADA_CHUNKS
    cols = d6 // n
    rows, nblk = _row_chunks(w_in_t.shape[1], depth * n)
    assert nblk == depth * n
    ncol_in = w_in_t.shape[2]
    return pl.pallas_call(
        _ada_kernel,
        grid=(depth, n),
        in_specs=[
            pl.BlockSpec((nb, d), lambda l, j: (0, 0)),
            pl.BlockSpec((1, d, cols), lambda l, j: (l, 0, j)),
            pl.BlockSpec((1, 1, cols), lambda l, j: (l, 0, j)),
            pl.BlockSpec((1, rows, ncol_in), lambda l, j: (0, l * n + j, 0)),
        ],
        out_specs=(
            pl.BlockSpec((1, nb, cols), lambda l, j: (l, 0, j)),
            pl.BlockSpec((rows, ncol_in), lambda l, j: (l * n + j, 0)),
        ),
        out_shape=(
            jax.ShapeDtypeStruct((depth, nb, d6), F32),
            jax.ShapeDtypeStruct(w_in_t.shape[1:], BF16),
        ),
        compiler_params=pltpu.CompilerParams(
            dimension_semantics=("arbitrary", "arbitrary"), vmem_limit_bytes=VMEM_LIMIT),
        name="ada",
    )(c, w_ada, b_ada.reshape(depth, 1, d6), w_in_t)


def _aug_base(h):
    return (HEAD_DIM if h % 2 == 0 else 0) + AUG * (h // 2)


def _inproj_kernel(tiles_per_seq, cast_scales, x_ref, mod_ref, g_ref, w_ref, spread_ref, bf_ref, *refs):
    n_cast = len(cast_scales)
    cast_in = refs[:n_cast]
    q_ref, k_ref, v_ref, pu_ref, cv_ref, wgl_ref = refs[n_cast:n_cast + 6]
    cast_out = refs[n_cast + 6:2 * n_cast + 6]
    carry_ref, wf_ref, wrest_ref = refs[2 * n_cast + 6:]
    i = pl.program_id(0)
    d = x_ref.shape[1]

    for src, dst, s in zip(cast_in, cast_out, cast_scales):
        dst[...] = (src[0] if s == 1.0 else src[0] * s).astype(BF16)

    fo = 3 * A_WIDTH
    n_plain = POOL_WIDTH + 3 * CONV_WIDTH
    n_rest = n_plain + wgl_ref.shape[0]

    @pl.when(i == 0)
    def _():
        wf_ref[...] = jnp.dot(spread_ref[...], w_ref[fo:fo + LANES, :],
                              preferred_element_type=F32).astype(BF16)
        for r in range(0, n_rest, PREP_ROWS):
            end = min(fo + r + PREP_ROWS + 2 * HEADS, w_ref.shape[0])
            blk = w_ref[fo + r:end, :].astype(F32)[HEADS:HEADS + PREP_ROWS]
            if r < n_plain:
                wrest_ref[r:r + PREP_ROWS, :] = blk.astype(BF16)
            else:
                wgl_ref[r - n_plain:r - n_plain + PREP_ROWS, :] = (blk * 0.5).astype(BF16)

    shift, scale = mod_ref[0, 0], mod_ref[0, 1]
    hb = (_rms(x_ref[...], g_ref[...]) * (1.0 + scale) + shift).astype(BF16)
    tm = hb.shape[0]

    def proj(lo_, hi_):
        return _nt_dot(hb, w_ref[lo_:hi_, :])

    def rest(lo_, hi_):
        return _nt_dot(hb, wrest_ref[lo_:hi_, :])

    zf = _nt_dot(hb, wf_ref[...]) + bf_ref[...]
    pu_ref[...] = rest(0, POOL_WIDTH).astype(BF16)
    cv_ref[...] = rest(POOL_WIDTH, n_plain).astype(BF16)
    zq = proj(0, A_WIDTH) * (HEAD_DIM ** -0.5 * LOG2E)
    zk = proj(A_WIDTH, 2 * A_WIDTH)
    zv = proj(2 * A_WIDTH, 3 * A_WIDTH)

    lf = jnp.minimum(zf, 0.0) - jnp.log(1.0 + jnp.exp(-jnp.abs(zf)))
    row = lax.broadcasted_iota(jnp.int32, lf.shape, 0)
    step = 1
    while step < tm:
        lf = lf + jnp.where(row >= step, pltpu.roll(lf, step, 0), 0.0)
        step *= 2

    @pl.when(i % tiles_per_seq == 0)
    def _():
        carry_ref[...] = jnp.zeros_like(carry_ref)

    fc = lf + carry_ref[0:1, :]
    carry_ref[...] = jnp.broadcast_to(fc[tm - 1:tm, :], carry_ref.shape)

    f2 = fc * LOG2E
    hi = f2.astype(BF16).astype(F32)
    rem = f2 - hi
    mid = rem.astype(BF16).astype(F32)
    lo = rem - mid
    lane = lax.broadcasted_iota(jnp.int32, f2.shape, 1)
    j = (lane % HEAD_DIM) % AUG
    part = jnp.where(j % 3 == 0, hi, jnp.where(j % 3 == 1, mid, lo))
    aug_q = jnp.where(j < 3, 1.0, part)
    aug_k = jnp.where(j < 3, -part, 1.0)

    for hd in range(HEADS):
        p = hd // 2
        in_head = (lane >= (hd % 2) * HEAD_DIM) & (lane < (hd % 2 + 1) * HEAD_DIM)
        base = _aug_base(hd)
        in_aug = (lane >= base) & (lane < base + AUG)
        pair = slice(p * LANES, (p + 1) * LANES)
        q_ref[hd] = jnp.where(in_head, zq[:, pair], jnp.where(in_aug, aug_q, 0.0)).astype(BF16)
        k_ref[hd] = jnp.where(in_head, zk[:, pair], jnp.where(in_aug, aug_k, 0.0)).astype(BF16)
        ones_lane = HEAD_DIM if hd % 2 == 0 else HEAD_DIM - 1
        v_ref[hd] = jnp.where(in_head, zv[:, pair], jnp.where(lane == ones_lane, 1.0, 0.0)).astype(BF16)


def _inproj(x2, mod, l, g, w_in_bf, spread, bf, seq, casts):
    t, d = x2.shape
    nt = t // TM
    tiles_per_seq = seq // TM
    mod_row0 = l * (t // seq)
    n_plain = POOL_WIDTH + 3 * CONV_WIDTH
    n_gate = w_in_bf.shape[0] - 3 * A_WIDTH - HEADS - n_plain
    cast_in_specs, cast_out_specs, cast_shapes = [], [], []
    for w, w_layer, _ in casts:
        _, k, n = w.shape
        rows, nblk = _row_chunks(k, nt)
        cast_in_specs.append(pl.BlockSpec(
            (1, rows, n), functools.partial(lambda l_, nb_, i: (l_, jnp.minimum(i, nb_ - 1), 0), w_layer, nblk)))
        cast_out_specs.append(pl.BlockSpec(
            (rows, n), functools.partial(lambda nb_, i: (jnp.minimum(i, nb_ - 1), 0), nblk)))
        cast_shapes.append(jax.ShapeDtypeStruct((k, n), BF16))
    out_shape = (
        jax.ShapeDtypeStruct((HEADS, t, LANES), BF16),
        jax.ShapeDtypeStruct((HEADS, t, LANES), BF16),
        jax.ShapeDtypeStruct((HEADS, t, LANES), BF16),
        jax.ShapeDtypeStruct((t, POOL_WIDTH), BF16),
        jax.ShapeDtypeStruct((t, 3 * CONV_WIDTH), BF16),
        jax.ShapeDtypeStruct((n_gate, d), BF16),
    )
    head_spec = pl.BlockSpec((HEADS, TM, LANES), lambda i: (0, i, 0))
    out_specs = (
        head_spec, head_spec, head_spec,
        pl.BlockSpec((TM, POOL_WIDTH), lambda i: (i, 0)),
        pl.BlockSpec((TM, 3 * CONV_WIDTH), lambda i: (i, 0)),
        pl.BlockSpec((n_gate, d), lambda i: (0, 0)),
    )
    outs = pl.pallas_call(
        functools.partial(_inproj_kernel, tiles_per_seq, tuple(s for _, _, s in casts)),
        grid=(nt,),
        in_specs=[
            pl.BlockSpec((TM, d), lambda i: (i, 0)),
            pl.BlockSpec((1, 2, 1, d), lambda i: (mod_row0 + i // tiles_per_seq, 0, 0, 0)),
            _layer_spec(g, l),
            _const_spec(w_in_bf.shape),
            _const_spec(spread.shape),
            _layer_spec(bf, l),
        ] + cast_in_specs,
        out_specs=out_specs + tuple(cast_out_specs),
        out_shape=out_shape + tuple(cast_shapes),
        scratch_shapes=[
            pltpu.VMEM((8, LANES), F32),
            pltpu.VMEM((LANES, d), BF16),
            pltpu.VMEM((n_plain, d), BF16),
        ],
        compiler_params=pltpu.CompilerParams(
            dimension_semantics=("arbitrary",), vmem_limit_bytes=VMEM_LIMIT),
        name="inproj",
    )(x2, mod, g, w_in_bf, spread, bf, *[w for w, _, _ in casts])
    return outs[:6], outs[6:]


def _attn_kernel(q_ref, k_ref, v_ref, o_ref):
    seq = q_ref.shape[1]
    nq = seq // TQ
    half = TK // 2
    steps = [(qi, kt) for qi in range(nq) for kt in range(qi + 1)]
    mask0 = (lax.broadcasted_iota(jnp.int32, (half, TQ), 0)
             <= lax.broadcasted_iota(jnp.int32, (half, TQ), 1))
    mask1 = (lax.broadcasted_iota(jnp.int32, (half, half), 0)
             <= lax.broadcasted_iota(jnp.int32, (half, half), 1))
    vts = [v_ref[hh].astype(F32).T.astype(BF16)[hh * (LANES - PV_ROWS):hh * (LANES - PV_ROWS) + PV_ROWS]
           for hh in range(2)]

    def logits(step, hh):
        qi, kt = step
        q = q_ref[hh, qi * TQ:(qi + 1) * TQ, :]
        if kt < qi:
            st = _nt_dot(k_ref[hh, kt * TK:(kt + 1) * TK, :], q)
            return (st,), jnp.max(st, axis=0, keepdims=True)
        st0 = jnp.where(mask0, _nt_dot(k_ref[hh, kt * TK:kt * TK + half, :], q), NEG_INF)
        st1 = jnp.where(mask1, _nt_dot(k_ref[hh, kt * TK + half:(kt + 1) * TK, :], q[half:]), NEG_INF)
        c0 = jnp.max(st0, axis=0, keepdims=True)
        c1 = jnp.max(st1, axis=0, keepdims=True)
        return (st0, st1), jnp.concatenate([c0[:, :half], jnp.maximum(c0[:, half:], c1)], axis=1)

    def update(step, hh, sts, cmax, m, acc):
        qi, kt = step
        vt = vts[hh]
        m_new = cmax if m is None else jnp.maximum(m, cmax)
        if kt < qi:
            pt = jnp.exp2(sts[0] - m_new).astype(BF16)
            pv = jnp.dot(vt[:, kt * TK:(kt + 1) * TK], pt, preferred_element_type=F32)
        else:
            pt0 = jnp.exp2(sts[0] - m_new).astype(BF16)
            pt1 = jnp.exp2(sts[1] - m_new[:, half:]).astype(BF16)
            pv = jnp.dot(vt[:, kt * TK:kt * TK + half], pt0, preferred_element_type=F32)
            pv1 = jnp.dot(vt[:, kt * TK + half:(kt + 1) * TK], pt1, preferred_element_type=F32)
            pv = jnp.concatenate([pv[:, :half], pv[:, half:] + pv1], axis=1)
        if m is not None:
            pv = jnp.exp2(m - m_new) * acc + pv
        return m_new, pv

    cur = [logits(steps[0], hh) for hh in range(2)]
    state = [(None, None), (None, None)]
    for s, step in enumerate(steps):
        qi, kt = step
        nxt = [None, None]
        for hh in range(2):
            if s + 1 < len(steps):
                nxt[hh] = logits(steps[s + 1], hh)
            state[hh] = update(step, hh, *cur[hh], *state[hh])
        cur = nxt
        if kt == qi:
            a0, a1 = state[0][1], state[1][1]
            pad = PV_ROWS - HEAD_DIM
            out = jnp.concatenate([a0[0:HEAD_DIM] / a0[HEAD_DIM:HEAD_DIM + 1],
                                   a1[pad:] / a1[pad - 1:pad]], axis=0)
            o_ref[qi * TQ:(qi + 1) * TQ, :] = out.T.astype(BF16)
            state = [(None, None), (None, None)]


def _attention(q8, k8, v8, seq):
    _, t, _ = q8.shape
    nb = t // seq
    pair_spec = pl.BlockSpec((2, seq, LANES), lambda b, p: (p, b, 0))
    return pl.pallas_call(
        _attn_kernel,
        grid=(nb, PAIRS),
        in_specs=[pair_spec, pair_spec, pair_spec],
        out_specs=pl.BlockSpec((seq, LANES), lambda b, p: (b, p)),
        out_shape=jax.ShapeDtypeStruct((t, A_WIDTH), BF16),
        compiler_params=pltpu.CompilerParams(
            dimension_semantics=("arbitrary", "arbitrary"), vmem_limit_bytes=VMEM_LIMIT),
        name="attn",
    )(q8, k8, v8)


def _tail_kernel(tiles_per_seq, o_ref, pu_ref, puh_ref, cv_ref, cvh_ref, x_ref, mod_ref,
                 gmix_ref, wgl_ref, wpool_ref, pscale_ref, convw_ref, wbr_ref, wout_ref, g_ref,
                 gpre_ref, w1_ref, w2_ref, gpost_ref, out_ref, x1_ref):
    i = pl.program_id(0)
    shift_m, scale_m, gate_m = mod_ref[0, 0], mod_ref[0, 1], mod_ref[0, 2]
    shift_f, scale_f, gate_f = mod_ref[0, 3], mod_ref[0, 4], mod_ref[0, 5]
    tile_in_seq = i % tiles_per_seq
    tm, d = x_ref.shape
    n_sub = tm // SUB
    lane = lax.broadcasted_iota(jnp.int32, (SUB + HALO, POOL_WIDTH), 1)
    row = lax.broadcasted_iota(jnp.int32, (SUB + HALO, POOL_WIDTH), 0)
    gd = POOL_WIDTH // len(POOL_WINDOWS)
    wsz = jnp.where(lane < gd, 2.0, jnp.where(lane < 2 * gd, 4.0, jnp.where(lane < 3 * gd, 8.0, 16.0)))
    cw = convw_ref[...]
    o1 = A_WIDTH + POOL_WIDTH

    def with_history(ref, halo_ref, j):
        if j == 0:
            head = jnp.where(tile_in_seq != 0, halo_ref[...].astype(F32), 0.0)
        else:
            head = ref[j * SUB - HALO:j * SUB, :].astype(F32)
        return jnp.concatenate([head, ref[j * SUB:(j + 1) * SUB, :].astype(F32)], axis=0)

    def mix(j):
        rows = slice(j * SUB, (j + 1) * SUB)
        ext = with_history(pu_ref, puh_ref, j)
        a2 = ext + pltpu.roll(ext, 1, 0)
        a4 = a2 + pltpu.roll(a2, 2, 0)
        a8 = a4 + pltpu.roll(a4, 4, 0)
        a16 = a8 + pltpu.roll(a8, 8, 0)
        win = jnp.where(lane < gd, a2, jnp.where(lane < 2 * gd, a4, jnp.where(lane < 3 * gd, a8, a16)))
        frames = (tile_in_seq * tm + j * SUB + row - (HALO - 1)).astype(F32)
        cnt = jnp.maximum(jnp.minimum(frames, wsz), 1.0)
        pm = (win / cnt - ext)[HALO:]
        br_b = jnp.dot(pm.astype(BF16), wpool_ref[...], preferred_element_type=F32) * pscale_ref[...]

        cve = with_history(cv_ref, cvh_ref, j)
        u = cve[:, 2 * CONV_WIDTH:] * cve[:, :CONV_WIDTH]
        y = cw[0:1] * pltpu.roll(u, 2, 0) + cw[1:2] * pltpu.roll(u, 1, 0) + cw[2:3] * u
        br_c = (cve[:, CONV_WIDTH:2 * CONV_WIDTH] * y)[HALO:]

        hb = (_rms(x_ref[rows, :], gmix_ref[...]) * (1.0 + scale_m) + shift_m).astype(BF16)

        def gated(n, branch, lo, hi):
            b = jnp.dot(branch, wbr_ref[lo:hi, :], preferred_element_type=F32)
            return b + b * jnp.tanh(_nt_dot(hb, wgl_ref[n * d:(n + 1) * d, :]))

        merged = gated(0, o_ref[rows, :], 0, A_WIDTH)
        merged += gated(1, br_b.astype(BF16), A_WIDTH, o1)
        merged += gated(2, br_c.astype(BF16), o1, o1 + CONV_WIDTH)
        return merged.astype(BF16)

    def project(j, merged):
        rows = slice(j * SUB, (j + 1) * SUB)
        yo = jnp.dot(merged, wout_ref[...], preferred_element_type=F32)
        x1_ref[rows, :] = x_ref[rows, :] + gate_m * _rms(yo, g_ref[...])

    cur = mix(0)
    for j in range(n_sub):
        nxt = mix(j + 1) if j + 1 < n_sub else None
        project(j, cur)
        cur = nxt

    x1 = x1_ref[...]
    hb = (_rms(x1, gpre_ref[...]) * (1.0 + scale_f) + shift_f).astype(BF16)
    dff = w1_ref.shape[1]
    acc = jnp.zeros(x1.shape, F32)
    for c in range(dff // FF_CHUNK):
        up = jnp.dot(hb, w1_ref[:, c * FF_CHUNK:(c + 1) * FF_CHUNK], preferred_element_type=F32)
        act = jnp.square(jnp.maximum(up, 0.0)).astype(BF16)
        acc += jnp.dot(act, w2_ref[c * FF_CHUNK:(c + 1) * FF_CHUNK, :], preferred_element_type=F32)
    out_ref[...] = x1 + gate_f * _rms(acc, gpost_ref[...])


def _tail(o, pu, cv, x2, mod, l, g_mix_pre, wgl_bf, wpool, pscale, convw, wbr_bf, wout_bf, g_mix_post, g_ff_pre,
          w1_bf, w2_bf, g_ff_post, seq):
    t, d = x2.shape
    nt = t // TMT
    tiles_per_seq = seq // TMT
    hb = TMT // HALO
    mod_row0 = l * (t // seq)

    def halo_map(i):
        return (jnp.maximum(i * hb - 1, 0), 0)

    consts = (g_mix_pre, wgl_bf, wpool, pscale, convw, wbr_bf, wout_bf, g_mix_post, g_ff_pre, w1_bf, w2_bf,
              g_ff_post)
    per_layer = (True, False, True, True, True, False, False, True, True, False, False, True)
    return pl.pallas_call(
        functools.partial(_tail_kernel, tiles_per_seq),
        grid=(nt,),
        in_specs=[
            pl.BlockSpec((TMT, A_WIDTH), lambda i: (i, 0)),
            pl.BlockSpec((TMT, POOL_WIDTH), lambda i: (i, 0)),
            pl.BlockSpec((HALO, POOL_WIDTH), halo_map),
            pl.BlockSpec((TMT, 3 * CONV_WIDTH), lambda i: (i, 0)),
            pl.BlockSpec((HALO, 3 * CONV_WIDTH), halo_map),
            pl.BlockSpec((TMT, d), lambda i: (i, 0)),
            pl.BlockSpec((1,) + mod.shape[1:], lambda i: (mod_row0 + i // tiles_per_seq, 0, 0, 0)),
        ] + [_layer_spec(a, l) if stacked else _const_spec(a.shape) for a, stacked in zip(consts, per_layer)],
        out_specs=pl.BlockSpec((TMT, d), lambda i: (i, 0)),
        out_shape=jax.ShapeDtypeStruct((t, d), F32),
        scratch_shapes=[pltpu.VMEM((TMT, d), F32)],
        compiler_params=pltpu.CompilerParams(
            dimension_semantics=("arbitrary",), vmem_limit_bytes=VMEM_LIMIT),
        name="tail",
    )(o, pu, pu, cv, cv, x2, mod, *consts)


def _block_diag(w):
    depth, g, c, dd = w.shape
    eye = jnp.eye(g, dtype=w.dtype)
    return (eye[None, :, None, :, None] * w[:, :, :, None, :]).reshape(depth, g * c, g * dd)


def kernel(x, c, w_ada, b_ada, g_mix_pre, g_mix_post, g_ff_pre, g_ff_post, w_in, b_f, w_pool, pool_scale,
           conv_w, w_branch, w_out, w_ff1, w_ff2):
    nb, seq, d = x.shape
    depth = w_ada.shape[0]
    assert all(seq % tile == 0 for tile in (TM, TMT, TQ, TK)) and TMT % SUB == 0
    assert d % LANES == 0 and d % PREP_ROWS == 0
    t = nb * seq
    assert (6 * d) % (ADA_CHUNKS * LANES) == 0
    w_in_t = jnp.swapaxes(w_in, 1, 2)
    mod, w_in_bf = _ada(c, w_ada, b_ada, w_in_t)
    mod = mod.reshape(depth * nb, 6, 1, d)
    x2 = x.reshape(t, d)
    stacked = {"in": w_in_t, "branch": w_branch, "out": w_out, "ff1": w_ff1, "ff2": w_ff2}
    cast_keys = [("in", l) for l in range(1, depth)]
    cast_keys += [(name, l) for l in range(depth) for name in ("branch", "out", "ff1", "ff2")]
    casts = [(stacked[name], l, 0.5 if name == "branch" else 1.0) for name, l in cast_keys]
    bf16_w = {("in", 0): w_in_bf}
    spread = np.zeros((LANES, LANES), np.float32)
    head_of_lane = np.zeros((LANES,), np.int32)
    lane_used = np.zeros((LANES,), bool)
    for hd in range(HEADS):
        spread[_aug_base(hd):_aug_base(hd) + AUG, hd] = 1.0
        head_of_lane[_aug_base(hd):_aug_base(hd) + AUG] = hd
        lane_used[_aug_base(hd):_aug_base(hd) + AUG] = True
    spread = jnp.asarray(spread, BF16)
    bf_all = jnp.where(lane_used, b_f[:, head_of_lane], 0.0)[:, None, :]
    rows = lambda a: a[:, None, :]
    w_pool_bd = _block_diag(w_pool).astype(BF16)
    for l in range(depth):
        (q8, k8, v8, pu, cv, wgl), cast = _inproj(x2, mod, l, rows(g_mix_pre), bf16_w[("in", l)], spread, bf_all,
                                                  seq, casts if l == 0 else [])
        if l == 0:
            bf16_w.update(zip(cast_keys, cast))
        o = _attention(q8, k8, v8, seq)
        x2 = _tail(o, pu, cv, x2, mod, l, rows(g_mix_pre), wgl, w_pool_bd, rows(pool_scale), conv_w,
                   bf16_w[("branch", l)], bf16_w[("out", l)], rows(g_mix_post), rows(g_ff_pre),
                   bf16_w[("ff1", l)], bf16_w[("ff2", l)], rows(g_ff_post), seq)
    return x2.reshape(nb, seq, d)
```

```python
import functools

import jax
import jax.numpy as jnp
import numpy as np
from jax import lax
from jax.experimental import pallas as pl
from jax.experimental.pallas import tpu as pltpu

F32 = jnp.float32
BF16 = jnp.bfloat16

LANES = 128
HEAD_DIM = 64
HEADS = 8
PAIRS = HEADS // 2
A_WIDTH = HEADS * HEAD_DIM
POOL_WINDOWS = (2, 4, 8, 16)
POOL_WIDTH = 256
CONV_WIDTH = 256
HALO = 16
RMS_EPS = 1e-6
NEG_INF = -1e30
LOG2E = 1.4426950408889634
AUG = 6
PV_ROWS = 128
ADA_CHUNKS = 4
VMEM_LIMIT = 56 * 1024 * 1024

TM = 512
TMT = 512
TQ = 512
TK = 512
FF_CHUNK = 1024
SUB = 256
PREP_ROWS = 128


def _const_spec(shape):
    n = len(shape)
    return pl.BlockSpec(shape, lambda *_: (0,) * n, pipeline_mode=pl.Buffered(1))


def _layer_spec(a, l):
    n = a.ndim - 1
    return pl.BlockSpec((None,) + a.shape[1:], lambda *_: (l,) + (0,) * n, pipeline_mode=pl.Buffered(1))


def _rms(x, g):
    ms = jnp.mean(x * x, axis=-1, keepdims=True)
    return x * lax.rsqrt(ms + RMS_EPS) * g


def _row_chunks(k, steps):
    rows = -(-(-(-k // steps)) // 16) * 16
    return rows, -(-k // rows)


def _nt_dot(a, b):
    return lax.dot_general(a, b, (((1,), (1,)), ((), ())), preferred_element_type=F32)


def _ada_kernel(c_ref, w_ref, b_ref, win_ref, o_ref, win_bf_ref):
    c = c_ref[...]
    sc = c * (1.0 / (1.0 + jnp.exp(-c)))
    w = w_ref[0]
    sc_hi, w_hi = sc.astype(BF16), w.astype(BF16)
    sc_lo = (sc - sc_hi.astype(F32)).astype(BF16)
    w_lo = (w - w_hi.astype(F32)).astype(BF16)
    dot = functools.partial(jnp.dot, preferred_element_type=F32)
    o_ref[0] = dot(sc_hi, w_hi) + dot(sc_hi, w_lo) + dot(sc_lo, w_hi) + b_ref[0]
    win_bf_ref[...] = win_ref[0].astype(BF16)


def _ada(c, w_ada, b_ada, w_in_t):
    depth, d, d6 = w_ada.shape
    nb = c.shape[0]
    n = ADA_CHUNKS
    cols = d6 // n
    rows, nblk = _row_chunks(w_in_t.shape[1], depth * n)
    assert nblk == depth * n
    ncol_in = w_in_t.shape[2]
    return pl.pallas_call(
        _ada_kernel,
        grid=(depth, n),
        in_specs=[
            pl.BlockSpec((nb, d), lambda l, j: (0, 0)),
            pl.BlockSpec((1, d, cols), lambda l, j: (l, 0, j)),
            pl.BlockSpec((1, 1, cols), lambda l, j: (l, 0, j)),
            pl.BlockSpec((1, rows, ncol_in), lambda l, j: (0, l * n + j, 0)),
        ],
        out_specs=(
            pl.BlockSpec((1, nb, cols), lambda l, j: (l, 0, j)),
            pl.BlockSpec((rows, ncol_in), lambda l, j: (l * n + j, 0)),
        ),
        out_shape=(
            jax.ShapeDtypeStruct((depth, nb, d6), F32),
            jax.ShapeDtypeStruct(w_in_t.shape[1:], BF16),
        ),
        compiler_params=pltpu.CompilerParams(
            dimension_semantics=("arbitrary", "arbitrary"), vmem_limit_bytes=VMEM_LIMIT),
        name="ada",
    )(c, w_ada, b_ada.reshape(depth, 1, d6), w_in_t)


def _aug_base(h):
    return (HEAD_DIM if h % 2 == 0 else 0) + AUG * (h // 2)


def _inproj_kernel(tiles_per_seq, cast_scales, x_ref, mod_ref, g_ref, w_ref, spread_ref, bf_ref, *refs):
    n_cast = len(cast_scales)
    cast_in = refs[:n_cast]
    q_ref, k_ref, v_ref, pu_ref, cv_ref, wgl_ref = refs[n_cast:n_cast + 6]
    cast_out = refs[n_cast + 6:2 * n_cast + 6]
    carry_ref, wf_ref, wrest_ref = refs[2 * n_cast + 6:]
    i = pl.program_id(0)
    d = x_ref.shape[1]

    for src, dst, s in zip(cast_in, cast_out, cast_scales):
        dst[...] = (src[0] if s == 1.0 else src[0] * s).astype(BF16)

    fo = 3 * A_WIDTH
    n_plain = POOL_WIDTH + 3 * CONV_WIDTH
    n_rest = n_plain + wgl_ref.shape[0]

    @pl.when(i == 0)
    def _():
        wf_ref[...] = jnp.dot(spread_ref[...], w_ref[fo:fo + LANES, :],
                              preferred_element_type=F32).astype(BF16)
        for r in range(0, n_rest, PREP_ROWS):
            end = min(fo + r + PREP_ROWS + 2 * HEADS, w_ref.shape[0])
            blk = w_ref[fo + r:end, :].astype(F32)[HEADS:HEADS + PREP_ROWS]
            if r < n_plain:
                wrest_ref[r:r + PREP_ROWS, :] = blk.astype(BF16)
            else:
                wgl_ref[r - n_plain:r - n_plain + PREP_ROWS, :] = (blk * 0.5).astype(BF16)

    shift, scale = mod_ref[0, 0], mod_ref[0, 1]
    hb = (_rms(x_ref[...], g_ref[...]) * (1.0 + scale) + shift).astype(BF16)
    tm = hb.shape[0]

    def proj(lo_, hi_):
        return _nt_dot(hb, w_ref[lo_:hi_, :])

    def rest(lo_, hi_):
        return _nt_dot(hb, wrest_ref[lo_:hi_, :])

    zf = _nt_dot(hb, wf_ref[...]) + bf_ref[...]
    pu_ref[...] = rest(0, POOL_WIDTH).astype(BF16)
    cv_ref[...] = rest(POOL_WIDTH, n_plain).astype(BF16)
    zq = proj(0, A_WIDTH) * (HEAD_DIM ** -0.5 * LOG2E)
    zk = proj(A_WIDTH, 2 * A_WIDTH)
    zv = proj(2 * A_WIDTH, 3 * A_WIDTH)

    lf = jnp.minimum(zf, 0.0) - jnp.log(1.0 + jnp.exp(-jnp.abs(zf)))
    row = lax.broadcasted_iota(jnp.int32, lf.shape, 0)
    step = 1
    while step < tm:
        lf = lf + jnp.where(row >= step, pltpu.roll(lf, step, 0), 0.0)
        step *= 2

    @pl.when(i % tiles_per_seq == 0)
    def _():
        carry_ref[...] = jnp.zeros_like(carry_ref)

    fc = lf + carry_ref[0:1, :]
    carry_ref[...] = jnp.broadcast_to(fc[tm - 1:tm, :], carry_ref.shape)

    f2 = fc * LOG2E
    hi = f2.astype(BF16).astype(F32)
    rem = f2 - hi
    mid = rem.astype(BF16).astype(F32)
    lo = rem - mid
    lane = lax.broadcasted_iota(jnp.int32, f2.shape, 1)
    j = (lane % HEAD_DIM) % AUG
    part = jnp.where(j % 3 == 0, hi, jnp.where(j % 3 == 1, mid, lo))
    aug_q = jnp.where(j < 3, 1.0, part)
    aug_k = jnp.where(j < 3, -part, 1.0)

    for hd in range(HEADS):
        p = hd // 2
        in_head = (lane >= (hd % 2) * HEAD_DIM) & (lane < (hd % 2 + 1) * HEAD_DIM)
        base = _aug_base(hd)
        in_aug = (lane >= base) & (lane < base + AUG)
        pair = slice(p * LANES, (p + 1) * LANES)
        q_ref[hd] = jnp.where(in_head, zq[:, pair], jnp.where(in_aug, aug_q, 0.0)).astype(BF16)
        k_ref[hd] = jnp.where(in_head, zk[:, pair], jnp.where(in_aug, aug_k, 0.0)).astype(BF16)
        ones_lane = HEAD_DIM if hd % 2 == 0 else HEAD_DIM - 1
        v_ref[hd] = jnp.where(in_head, zv[:, pair], jnp.where(lane == ones_lane, 1.0, 0.0)).astype(BF16)


def _inproj(x2, mod, l, g, w_in_bf, spread, bf, seq, casts):
    t, d = x2.shape
    nt = t // TM
    tiles_per_seq = seq // TM
    mod_row0 = l * (t // seq)
    n_plain = POOL_WIDTH + 3 * CONV_WIDTH
    n_gate = w_in_bf.shape[0] - 3 * A_WIDTH - HEADS - n_plain
    cast_in_specs, cast_out_specs, cast_shapes = [], [], []
    for w, w_layer, _ in casts:
        _, k, n = w.shape
        rows, nblk = _row_chunks(k, nt)
        cast_in_specs.append(pl.BlockSpec(
            (1, rows, n), functools.partial(lambda l_, nb_, i: (l_, jnp.minimum(i, nb_ - 1), 0), w_layer, nblk)))
        cast_out_specs.append(pl.BlockSpec(
            (rows, n), functools.partial(lambda nb_, i: (jnp.minimum(i, nb_ - 1), 0), nblk)))
        cast_shapes.append(jax.ShapeDtypeStruct((k, n), BF16))
    out_shape = (
        jax.ShapeDtypeStruct((HEADS, t, LANES), BF16),
        jax.ShapeDtypeStruct((HEADS, t, LANES), BF16),
        jax.ShapeDtypeStruct((HEADS, t, LANES), BF16),
        jax.ShapeDtypeStruct((t, POOL_WIDTH), BF16),
        jax.ShapeDtypeStruct((t, 3 * CONV_WIDTH), BF16),
        jax.ShapeDtypeStruct((n_gate, d), BF16),
    )
    head_spec = pl.BlockSpec((HEADS, TM, LANES), lambda i: (0, i, 0))
    out_specs = (
        head_spec, head_spec, head_spec,
        pl.BlockSpec((TM, POOL_WIDTH), lambda i: (i, 0)),
        pl.BlockSpec((TM, 3 * CONV_WIDTH), lambda i: (i, 0)),
        pl.BlockSpec((n_gate, d), lambda i: (0, 0)),
    )
    outs = pl.pallas_call(
        functools.partial(_inproj_kernel, tiles_per_seq, tuple(s for _, _, s in casts)),
        grid=(nt,),
        in_specs=[
            pl.BlockSpec((TM, d), lambda i: (i, 0)),
            pl.BlockSpec((1, 2, 1, d), lambda i: (mod_row0 + i // tiles_per_seq, 0, 0, 0)),
            _layer_spec(g, l),
            _const_spec(w_in_bf.shape),
            _const_spec(spread.shape),
            _layer_spec(bf, l),
        ] + cast_in_specs,
        out_specs=out_specs + tuple(cast_out_specs),
        out_shape=out_shape + tuple(cast_shapes),
        scratch_shapes=[
            pltpu.VMEM((8, LANES), F32),
            pltpu.VMEM((LANES, d), BF16),
            pltpu.VMEM((n_plain, d), BF16),
        ],
        compiler_params=pltpu.CompilerParams(
            dimension_semantics=("arbitrary",), vmem_limit_bytes=VMEM_LIMIT),
        name="inproj",
    )(x2, mod, g, w_in_bf, spread, bf, *[w for w, _, _ in casts])
    return outs[:6], outs[6:]


def _attn_kernel(q_ref, k_ref, v_ref, o_ref):
    seq = q_ref.shape[1]
    nq = seq // TQ
    half = TK // 2
    steps = [(qi, kt) for qi in range(nq) for kt in range(qi + 1)]
    mask0 = (lax.broadcasted_iota(jnp.int32, (half, TQ), 0)
             <= lax.broadcasted_iota(jnp.int32, (half, TQ), 1))
    mask1 = (lax.broadcasted_iota(jnp.int32, (half, half), 0)
             <= lax.broadcasted_iota(jnp.int32, (half, half), 1))
    vts = [v_ref[hh].T[hh * (LANES - PV_ROWS):hh * (LANES - PV_ROWS) + PV_ROWS]
           for hh in range(2)]

    def logits(step, hh):
        qi, kt = step
        q = q_ref[hh, qi * TQ:(qi + 1) * TQ, :]
        if kt < qi:
            st = _nt_dot(k_ref[hh, kt * TK:(kt + 1) * TK, :], q)
            return (st,), jnp.max(st, axis=0, keepdims=True)
        st0 = jnp.where(mask0, _nt_dot(k_ref[hh, kt * TK:kt * TK + half, :], q), NEG_INF)
        st1 = jnp.where(mask1, _nt_dot(k_ref[hh, kt * TK + half:(kt + 1) * TK, :], q[half:]), NEG_INF)
        c0 = jnp.max(st0, axis=0, keepdims=True)
        c1 = jnp.max(st1, axis=0, keepdims=True)
        return (st0, st1), jnp.concatenate([c0[:, :half], jnp.maximum(c0[:, half:], c1)], axis=1)

    def update(step, hh, sts, cmax, m, acc):
        qi, kt = step
        vt = vts[hh]
        m_new = cmax if m is None else jnp.maximum(m, cmax)
        if kt < qi:
            pt = jnp.exp2(sts[0] - m_new).astype(BF16)
            pv = jnp.dot(vt[:, kt * TK:(kt + 1) * TK], pt, preferred_element_type=F32)
        else:
            pt0 = jnp.exp2(sts[0] - m_new).astype(BF16)
            pt1 = jnp.exp2(sts[1] - m_new[:, half:]).astype(BF16)
            pv = jnp.dot(vt[:, kt * TK:kt * TK + half], pt0, preferred_element_type=F32)
            pv1 = jnp.dot(vt[:, kt * TK + half:(kt + 1) * TK], pt1, preferred_element_type=F32)
            pv = jnp.concatenate([pv[:, :half], pv[:, half:] + pv1], axis=1)
        if m is not None:
            pv = jnp.exp2(m - m_new) * acc + pv
        return m_new, pv

    cur = [logits(steps[0], hh) for hh in range(2)]
    state = [(None, None), (None, None)]
    for s, step in enumerate(steps):
        qi, kt = step
        nxt = [None, None]
        for hh in range(2):
            if s + 1 < len(steps):
                nxt[hh] = logits(steps[s + 1], hh)
            state[hh] = update(step, hh, *cur[hh], *state[hh])
        cur = nxt
        if kt == qi:
            a0, a1 = state[0][1], state[1][1]
            pad = PV_ROWS - HEAD_DIM
            out = jnp.concatenate([a0[0:HEAD_DIM] / a0[HEAD_DIM:HEAD_DIM + 1],
                                   a1[pad:] / a1[pad - 1:pad]], axis=0)
            o_ref[qi * TQ:(qi + 1) * TQ, :] = out.T.astype(BF16)
            state = [(None, None), (None, None)]


def _attention(q8, k8, v8, seq):
    _, t, _ = q8.shape
    nb = t // seq
    pair_spec = pl.BlockSpec((2, seq, LANES), lambda b, p: (p, b, 0))
    return pl.pallas_call(
        _attn_kernel,
        grid=(nb, PAIRS),
        in_specs=[pair_spec, pair_spec, pair_spec],
        out_specs=pl.BlockSpec((seq, LANES), lambda b, p: (b, p)),
        out_shape=jax.ShapeDtypeStruct((t, A_WIDTH), BF16),
        compiler_params=pltpu.CompilerParams(
            dimension_semantics=("arbitrary", "arbitrary"), vmem_limit_bytes=VMEM_LIMIT),
        name="attn",
    )(q8, k8, v8)


def _tail_kernel(tiles_per_seq, o_ref, pu_ref, puh_ref, cv_ref, cvh_ref, x_ref, mod_ref,
                 gmix_ref, wgl_ref, wpool_ref, pscale_ref, convw_ref, wbr_ref, wout_ref, g_ref,
                 gpre_ref, w1_ref, w2_ref, gpost_ref, out_ref, x1_ref):
    i = pl.program_id(0)
    shift_m, scale_m, gate_m = mod_ref[0, 0], mod_ref[0, 1], mod_ref[0, 2]
    shift_f, scale_f, gate_f = mod_ref[0, 3], mod_ref[0, 4], mod_ref[0, 5]
    tile_in_seq = i % tiles_per_seq
    tm, d = x_ref.shape
    n_sub = tm // SUB
    lane = lax.broadcasted_iota(jnp.int32, (SUB + HALO, POOL_WIDTH), 1)
    row = lax.broadcasted_iota(jnp.int32, (SUB + HALO, POOL_WIDTH), 0)
    gd = POOL_WIDTH // len(POOL_WINDOWS)
    wsz = jnp.where(lane < gd, 2.0, jnp.where(lane < 2 * gd, 4.0, jnp.where(lane < 3 * gd, 8.0, 16.0)))
    cw = convw_ref[...]
    o1 = A_WIDTH + POOL_WIDTH

    def with_history(ref, halo_ref, j):
        if j == 0:
            head = jnp.where(tile_in_seq != 0, halo_ref[...].astype(F32), 0.0)
        else:
            head = ref[j * SUB - HALO:j * SUB, :].astype(F32)
        return jnp.concatenate([head, ref[j * SUB:(j + 1) * SUB, :].astype(F32)], axis=0)

    def mix(j):
        rows = slice(j * SUB, (j + 1) * SUB)
        ext = with_history(pu_ref, puh_ref, j)
        a2 = ext + pltpu.roll(ext, 1, 0)
        a4 = a2 + pltpu.roll(a2, 2, 0)
        a8 = a4 + pltpu.roll(a4, 4, 0)
        a16 = a8 + pltpu.roll(a8, 8, 0)
        win = jnp.where(lane < gd, a2, jnp.where(lane < 2 * gd, a4, jnp.where(lane < 3 * gd, a8, a16)))
        frames = (tile_in_seq * tm + j * SUB + row - (HALO - 1)).astype(F32)
        cnt = jnp.maximum(jnp.minimum(frames, wsz), 1.0)
        pm = (win / cnt - ext)[HALO:]
        br_b = jnp.dot(pm.astype(BF16), wpool_ref[...], preferred_element_type=F32) * pscale_ref[...]

        cve = with_history(cv_ref, cvh_ref, j)
        u = cve[:, 2 * CONV_WIDTH:] * cve[:, :CONV_WIDTH]
        y = cw[0:1] * pltpu.roll(u, 2, 0) + cw[1:2] * pltpu.roll(u, 1, 0) + cw[2:3] * u
        br_c = (cve[:, CONV_WIDTH:2 * CONV_WIDTH] * y)[HALO:]

        hb = (_rms(x_ref[rows, :], gmix_ref[...]) * (1.0 + scale_m) + shift_m).astype(BF16)

        def gated(n, branch, lo, hi):
            b = jnp.dot(branch, wbr_ref[lo:hi, :], preferred_element_type=F32)
            return b + b * jnp.tanh(_nt_dot(hb, wgl_ref[n * d:(n + 1) * d, :]))

        merged = gated(0, o_ref[rows, :], 0, A_WIDTH)
        merged += gated(1, br_b.astype(BF16), A_WIDTH, o1)
        merged += gated(2, br_c.astype(BF16), o1, o1 + CONV_WIDTH)
        return merged.astype(BF16)

    def project(j, merged):
        rows = slice(j * SUB, (j + 1) * SUB)
        yo = jnp.dot(merged, wout_ref[...], preferred_element_type=F32)
        x1_ref[rows, :] = x_ref[rows, :] + gate_m * _rms(yo, g_ref[...])

    cur = mix(0)
    for j in range(n_sub):
        nxt = mix(j + 1) if j + 1 < n_sub else None
        project(j, cur)
        cur = nxt

    x1 = x1_ref[...]
    hb = (_rms(x1, gpre_ref[...]) * (1.0 + scale_f) + shift_f).astype(BF16)
    dff = w1_ref.shape[1]
    acc = jnp.zeros(x1.shape, F32)
    for c in range(dff // FF_CHUNK):
        up = jnp.dot(hb, w1_ref[:, c * FF_CHUNK:(c + 1) * FF_CHUNK], preferred_element_type=F32)
        act = jnp.square(jnp.maximum(up, 0.0)).astype(BF16)
        acc += jnp.dot(act, w2_ref[c * FF_CHUNK:(c + 1) * FF_CHUNK, :], preferred_element_type=F32)
    out_ref[...] = x1 + gate_f * _rms(acc, gpost_ref[...])


def _tail(o, pu, cv, x2, mod, l, g_mix_pre, wgl_bf, wpool, pscale, convw, wbr_bf, wout_bf, g_mix_post, g_ff_pre,
          w1_bf, w2_bf, g_ff_post, seq):
    t, d = x2.shape
    nt = t // TMT
    tiles_per_seq = seq // TMT
    hb = TMT // HALO
    mod_row0 = l * (t // seq)

    def halo_map(i):
        return (jnp.maximum(i * hb - 1, 0), 0)

    consts = (g_mix_pre, wgl_bf, wpool, pscale, convw, wbr_bf, wout_bf, g_mix_post, g_ff_pre, w1_bf, w2_bf,
              g_ff_post)
    per_layer = (True, False, True, True, True, False, False, True, True, False, False, True)
    return pl.pallas_call(
        functools.partial(_tail_kernel, tiles_per_seq),
        grid=(nt,),
        in_specs=[
            pl.BlockSpec((TMT, A_WIDTH), lambda i: (i, 0)),
            pl.BlockSpec((TMT, POOL_WIDTH), lambda i: (i, 0)),
            pl.BlockSpec((HALO, POOL_WIDTH), halo_map),
            pl.BlockSpec((TMT, 3 * CONV_WIDTH), lambda i: (i, 0)),
            pl.BlockSpec((HALO, 3 * CONV_WIDTH), halo_map),
            pl.BlockSpec((TMT, d), lambda i: (i, 0)),
            pl.BlockSpec((1,) + mod.shape[1:], lambda i: (mod_row0 + i // tiles_per_seq, 0, 0, 0)),
        ] + [_layer_spec(a, l) if stacked else _const_spec(a.shape) for a, stacked in zip(consts, per_layer)],
        out_specs=pl.BlockSpec((TMT, d), lambda i: (i, 0)),
        out_shape=jax.ShapeDtypeStruct((t, d), F32),
        scratch_shapes=[pltpu.VMEM((TMT, d), F32)],
        compiler_params=pltpu.CompilerParams(
            dimension_semantics=("arbitrary",), vmem_limit_bytes=VMEM_LIMIT),
        name="tail",
    )(o, pu, pu, cv, cv, x2, mod, *consts)


def _block_diag(w):
    depth, g, c, dd = w.shape
    eye = jnp.eye(g, dtype=w.dtype)
    return (eye[None, :, None, :, None] * w[:, :, :, None, :]).reshape(depth, g * c, g * dd)


def kernel(x, c, w_ada, b_ada, g_mix_pre, g_mix_post, g_ff_pre, g_ff_post, w_in, b_f, w_pool, pool_scale,
           conv_w, w_branch, w_out, w_ff1, w_ff2):
    nb, seq, d = x.shape
    depth = w_ada.shape[0]
    assert all(seq % tile == 0 for tile in (TM, TMT, TQ, TK)) and TMT % SUB == 0
    assert d % LANES == 0 and d % PREP_ROWS == 0
    t = nb * seq
    assert (6 * d) % (ADA_CHUNKS * LANES) == 0
    w_in_t = jnp.swapaxes(w_in, 1, 2)
    mod, w_in_bf = _ada(c, w_ada, b_ada, w_in_t)
    mod = mod.reshape(depth * nb, 6, 1, d)
    x2 = x.reshape(t, d)
    stacked = {"in": w_in_t, "branch": w_branch, "out": w_out, "ff1": w_ff1, "ff2": w_ff2}
    cast_keys = [("in", l) for l in range(1, depth)]
    cast_keys += [(name, l) for l in range(depth) for name in ("branch", "out", "ff1", "ff2")]
    casts = [(stacked[name], l, 0.5 if name == "branch" else 1.0) for name, l in cast_keys]
    bf16_w = {("in", 0): w_in_bf}
    spread = np.zeros((LANES, LANES), np.float32)
    head_of_lane = np.zeros((LANES,), np.int32)
    lane_used = np.zeros((LANES,), bool)
    for hd in range(HEADS):
        spread[_aug_base(hd):_aug_base(hd) + AUG, hd] = 1.0
        head_of_lane[_aug_base(hd):_aug_base(hd) + AUG] = hd
        lane_used[_aug_base(hd):_aug_base(hd) + AUG] = True
    spread = jnp.asarray(spread, BF16)
    bf_all = jnp.where(lane_used, b_f[:, head_of_lane], 0.0)[:, None, :]
    rows = lambda a: a[:, None, :]
    w_pool_bd = _block_diag(w_pool).astype(BF16)
    for l in range(depth):
        (q8, k8, v8, pu, cv, wgl), cast = _inproj(x2, mod, l, rows(g_mix_pre), bf16_w[("in", l)], spread, bf_all,
                                                  seq, casts if l == 0 else [])
        if l == 0:
            bf16_w.update(zip(cast_keys, cast))
        o = _attention(q8, k8, v8, seq)
        x2 = _tail(o, pu, cv, x2, mod, l, rows(g_mix_pre), wgl, w_pool_bd, rows(pool_scale), conv_w,
                   bf16_w[("branch", l)], bf16_w[("out", l)], rows(g_mix_post), rows(g_ff_pre),
                   bf16_w[("ff1", l)], bf16_w[("ff2", l)], rows(g_ff_post), seq)
    return x2.reshape(nb, seq, d)
```

```python
import functools

import jax
import jax.numpy as jnp
import numpy as np
from jax import lax
from jax.experimental import pallas as pl
from jax.experimental.pallas import tpu as pltpu

F32 = jnp.float32
BF16 = jnp.bfloat16

LANES = 128
HEAD_DIM = 64
HEADS = 8
PAIRS = HEADS // 2
A_WIDTH = HEADS * HEAD_DIM
POOL_WINDOWS = (2, 4, 8, 16)
POOL_WIDTH = 256
CONV_WIDTH = 256
HALO = 16
RMS_EPS = 1e-6
NEG_INF = -1e30
LOG2E = 1.4426950408889634
AUG = 6
PV_ROWS = 128
ADA_CHUNKS = 4
VMEM_LIMIT = 56 * 1024 * 1024

TM = 512
TMT = 512
TQ = 512
TK = 512
FF_CHUNK = 1024
SUB = 256
PREP_ROWS = 128


def _const_spec(shape):
    n = len(shape)
    return pl.BlockSpec(shape, lambda *_: (0,) * n, pipeline_mode=pl.Buffered(1))


def _layer_spec(a, l):
    n = a.ndim - 1
    return pl.BlockSpec((None,) + a.shape[1:], lambda *_: (l,) + (0,) * n, pipeline_mode=pl.Buffered(1))


def _rms(x, g):
    ms = jnp.mean(x * x, axis=-1, keepdims=True)
    return x * lax.rsqrt(ms + RMS_EPS) * g


def _row_chunks(k, steps):
    rows = -(-(-(-k // steps)) // 16) * 16
    return rows, -(-k // rows)


def _nt_dot(a, b):
    return lax.dot_general(a, b, (((1,), (1,)), ((), ())), preferred_element_type=F32)


def _ada_kernel(c_ref, w_ref, b_ref, win_ref, o_ref, win_bf_ref):
    c = c_ref[...]
    sc = c * (1.0 / (1.0 + jnp.exp(-c)))
    w = w_ref[0]
    sc_hi, w_hi = sc.astype(BF16), w.astype(BF16)
    sc_lo = (sc - sc_hi.astype(F32)).astype(BF16)
    w_lo = (w - w_hi.astype(F32)).astype(BF16)
    dot = functools.partial(jnp.dot, preferred_element_type=F32)
    o_ref[0] = dot(sc_hi, w_hi) + dot(sc_hi, w_lo) + dot(sc_lo, w_hi) + b_ref[0]
    win_bf_ref[...] = win_ref[0].astype(BF16)


def _ada(c, w_ada, b_ada, w_in_t):
    depth, d, d6 = w_ada.shape
    nb = c.shape[0]
    n = ADA_CHUNKS
    cols = d6 // n
    rows, nblk = _row_chunks(w_in_t.shape[1], depth * n)
    assert nblk == depth * n
    ncol_in = w_in_t.shape[2]
    return pl.pallas_call(
        _ada_kernel,
        grid=(depth, n),
        in_specs=[
            pl.BlockSpec((nb, d), lambda l, j: (0, 0)),
            pl.BlockSpec((1, d, cols), lambda l, j: (l, 0, j)),
            pl.BlockSpec((1, 1, cols), lambda l, j: (l, 0, j)),
            pl.BlockSpec((1, rows, ncol_in), lambda l, j: (0, l * n + j, 0)),
        ],
        out_specs=(
            pl.BlockSpec((1, nb, cols), lambda l, j: (l, 0, j)),
            pl.BlockSpec((rows, ncol_in), lambda l, j: (l * n + j, 0)),
        ),
        out_shape=(
            jax.ShapeDtypeStruct((depth, nb, d6), F32),
            jax.ShapeDtypeStruct(w_in_t.shape[1:], BF16),
        ),
        compiler_params=pltpu.CompilerParams(
            dimension_semantics=("arbitrary", "arbitrary"), vmem_limit_bytes=VMEM_LIMIT),
        name="ada",
    )(c, w_ada, b_ada.reshape(depth, 1, d6), w_in_t)


def _aug_base(h):
    return (HEAD_DIM if h % 2 == 0 else 0) + AUG * (h // 2)


def _inproj_kernel(tiles_per_seq, cast_scales, x_ref, mod_ref, g_ref, w_ref, spread_ref, bf_ref, *refs):
    n_cast = len(cast_scales)
    cast_in = refs[:n_cast]
    q_ref, k_ref, v_ref, pu_ref, cv_ref, wgl_ref = refs[n_cast:n_cast + 6]
    cast_out = refs[n_cast + 6:2 * n_cast + 6]
    carry_ref, wf_ref, wrest_ref = refs[2 * n_cast + 6:]
    i = pl.program_id(0)
    d = x_ref.shape[1]

    for src, dst, s in zip(cast_in, cast_out, cast_scales):
        dst[...] = (src[0] if s == 1.0 else src[0] * s).astype(BF16)

    fo = 3 * A_WIDTH
    n_plain = POOL_WIDTH + 3 * CONV_WIDTH
    n_rest = n_plain + wgl_ref.shape[0]

    @pl.when(i == 0)
    def _():
        wf_ref[...] = jnp.dot(spread_ref[...], w_ref[fo:fo + LANES, :],
                              preferred_element_type=F32).astype(BF16)
        for r in range(0, n_rest, PREP_ROWS):
            end = min(fo + r + PREP_ROWS + 2 * HEADS, w_ref.shape[0])
            blk = w_ref[fo + r:end, :].astype(F32)[HEADS:HEADS + PREP_ROWS]
            if r < n_plain:
                wrest_ref[r:r + PREP_ROWS, :] = blk.astype(BF16)
            else:
                wgl_ref[r - n_plain:r - n_plain + PREP_ROWS, :] = (blk * 0.5).astype(BF16)

    shift, scale = mod_ref[0, 0], mod_ref[0, 1]
    hb = (_rms(x_ref[...], g_ref[...]) * (1.0 + scale) + shift).astype(BF16)
    tm = hb.shape[0]

    def proj(lo_, hi_):
        return _nt_dot(hb, w_ref[lo_:hi_, :])

    def rest(lo_, hi_):
        return _nt_dot(hb, wrest_ref[lo_:hi_, :])

    zf = _nt_dot(hb, wf_ref[...]) + bf_ref[...]
    pu_ref[...] = rest(0, POOL_WIDTH).astype(BF16)
    cv_ref[...] = rest(POOL_WIDTH, n_plain).astype(BF16)
    zq = proj(0, A_WIDTH) * (HEAD_DIM ** -0.5 * LOG2E)
    zk = proj(A_WIDTH, 2 * A_WIDTH)
    zv = proj(2 * A_WIDTH, 3 * A_WIDTH)

    lf = jnp.minimum(zf, 0.0) - jnp.log(1.0 + jnp.exp(-jnp.abs(zf)))
    row = lax.broadcasted_iota(jnp.int32, lf.shape, 0)
    step = 1
    while step < tm:
        lf = lf + jnp.where(row >= step, pltpu.roll(lf, step, 0), 0.0)
        step *= 2

    @pl.when(i % tiles_per_seq == 0)
    def _():
        carry_ref[...] = jnp.zeros_like(carry_ref)

    fc = lf + carry_ref[0:1, :]
    carry_ref[...] = jnp.broadcast_to(fc[tm - 1:tm, :], carry_ref.shape)

    f2 = fc * LOG2E
    hi = f2.astype(BF16).astype(F32)
    rem = f2 - hi
    mid = rem.astype(BF16).astype(F32)
    lo = rem - mid
    lane = lax.broadcasted_iota(jnp.int32, f2.shape, 1)
    j = (lane % HEAD_DIM) % AUG
    part = jnp.where(j % 3 == 0, hi, jnp.where(j % 3 == 1, mid, lo))
    aug_q = jnp.where(j < 3, 1.0, part)
    aug_k = jnp.where(j < 3, -part, 1.0)

    for hd in range(HEADS):
        p = hd // 2
        in_head = (lane >= (hd % 2) * HEAD_DIM) & (lane < (hd % 2 + 1) * HEAD_DIM)
        base = _aug_base(hd)
        in_aug = (lane >= base) & (lane < base + AUG)
        pair = slice(p * LANES, (p + 1) * LANES)
        q_ref[hd] = jnp.where(in_head, zq[:, pair], jnp.where(in_aug, aug_q, 0.0)).astype(BF16)
        k_ref[hd] = jnp.where(in_head, zk[:, pair], jnp.where(in_aug, aug_k, 0.0)).astype(BF16)
        ones_lane = HEAD_DIM if hd % 2 == 0 else HEAD_DIM - 1
        v_ref[hd] = jnp.where(in_head, zv[:, pair], jnp.where(lane == ones_lane, 1.0, 0.0)).astype(BF16)


def _inproj(x2, mod, l, g, w_in_bf, spread, bf, seq, casts):
    t, d = x2.shape
    nt = t // TM
    tiles_per_seq = seq // TM
    mod_row0 = l * (t // seq)
    n_plain = POOL_WIDTH + 3 * CONV_WIDTH
    n_gate = w_in_bf.shape[0] - 3 * A_WIDTH - HEADS - n_plain
    cast_in_specs, cast_out_specs, cast_shapes = [], [], []
    for w, w_layer, _ in casts:
        _, k, n = w.shape
        rows, nblk = _row_chunks(k, nt)
        cast_in_specs.append(pl.BlockSpec(
            (1, rows, n), functools.partial(lambda l_, nb_, i: (l_, jnp.minimum(i, nb_ - 1), 0), w_layer, nblk)))
        cast_out_specs.append(pl.BlockSpec(
            (rows, n), functools.partial(lambda nb_, i: (jnp.minimum(i, nb_ - 1), 0), nblk)))
        cast_shapes.append(jax.ShapeDtypeStruct((k, n), BF16))
    out_shape = (
        jax.ShapeDtypeStruct((HEADS, t, LANES), BF16),
        jax.ShapeDtypeStruct((HEADS, t, LANES), BF16),
        jax.ShapeDtypeStruct((HEADS, t, LANES), BF16),
        jax.ShapeDtypeStruct((t, POOL_WIDTH), BF16),
        jax.ShapeDtypeStruct((t, 3 * CONV_WIDTH), BF16),
        jax.ShapeDtypeStruct((n_gate, d), BF16),
    )
    head_spec = pl.BlockSpec((HEADS, TM, LANES), lambda i: (0, i, 0))
    out_specs = (
        head_spec, head_spec, head_spec,
        pl.BlockSpec((TM, POOL_WIDTH), lambda i: (i, 0)),
        pl.BlockSpec((TM, 3 * CONV_WIDTH), lambda i: (i, 0)),
        pl.BlockSpec((n_gate, d), lambda i: (0, 0)),
    )
    outs = pl.pallas_call(
        functools.partial(_inproj_kernel, tiles_per_seq, tuple(s for _, _, s in casts)),
        grid=(nt,),
        in_specs=[
            pl.BlockSpec((TM, d), lambda i: (i, 0)),
            pl.BlockSpec((1, 2, 1, d), lambda i: (mod_row0 + i // tiles_per_seq, 0, 0, 0)),
            _layer_spec(g, l),
            _const_spec(w_in_bf.shape),
            _const_spec(spread.shape),
            _layer_spec(bf, l),
        ] + cast_in_specs,
        out_specs=out_specs + tuple(cast_out_specs),
        out_shape=out_shape + tuple(cast_shapes),
        scratch_shapes=[
            pltpu.VMEM((8, LANES), F32),
            pltpu.VMEM((LANES, d), BF16),
            pltpu.VMEM((n_plain, d), BF16),
        ],
        compiler_params=pltpu.CompilerParams(
            dimension_semantics=("arbitrary",), vmem_limit_bytes=VMEM_LIMIT),
        name="inproj",
    )(x2, mod, g, w_in_bf, spread, bf, *[w for w, _, _ in casts])
    return outs[:6], outs[6:]


def _attn_kernel(q_ref, k_ref, v_ref, o_ref):
    seq = q_ref.shape[1]
    nq = seq // TQ
    half = TK // 2
    steps = [(qi, kt) for qi in range(nq) for kt in range(qi + 1)]
    mask0 = (lax.broadcasted_iota(jnp.int32, (half, TQ), 0)
             <= lax.broadcasted_iota(jnp.int32, (half, TQ), 1))
    mask1 = (lax.broadcasted_iota(jnp.int32, (half, half), 0)
             <= lax.broadcasted_iota(jnp.int32, (half, half), 1))
    vts = [v_ref[hh].T[hh * (LANES - PV_ROWS):hh * (LANES - PV_ROWS) + PV_ROWS]
           for hh in range(2)]

    def logits(step, hh):
        qi, kt = step
        q = q_ref[hh, qi * TQ:(qi + 1) * TQ, :]
        if kt < qi:
            st = _nt_dot(k_ref[hh, kt * TK:(kt + 1) * TK, :], q)
            return (st,), jnp.max(st, axis=0, keepdims=True)
        st0 = jnp.where(mask0, _nt_dot(k_ref[hh, kt * TK:kt * TK + half, :], q), NEG_INF)
        st1 = jnp.where(mask1, _nt_dot(k_ref[hh, kt * TK + half:(kt + 1) * TK, :], q[half:]), NEG_INF)
        c0 = jnp.max(st0, axis=0, keepdims=True)
        c1 = jnp.max(st1, axis=0, keepdims=True)
        return (st0, st1), jnp.concatenate([c0[:, :half], jnp.maximum(c0[:, half:], c1)], axis=1)

    def update(step, hh, sts, cmax, m, acc):
        qi, kt = step
        vt = vts[hh]
        m_new = cmax if m is None else jnp.maximum(m, cmax)
        if kt < qi:
            pt = jnp.exp2(sts[0] - m_new).astype(BF16)
            pv = jnp.dot(vt[:, kt * TK:(kt + 1) * TK], pt, preferred_element_type=F32)
        else:
            pt0 = jnp.exp2(sts[0] - m_new).astype(BF16)
            pt1 = jnp.exp2(sts[1] - m_new[:, half:]).astype(BF16)
            pv = jnp.dot(vt[:, kt * TK:kt * TK + half], pt0, preferred_element_type=F32)
            pv1 = jnp.dot(vt[:, kt * TK + half:(kt + 1) * TK], pt1, preferred_element_type=F32)
            pv = jnp.concatenate([pv[:, :half], pv[:, half:] + pv1], axis=1)
        if m is not None:
            pv = jnp.exp2(m - m_new) * acc + pv
        return m_new, pv

    cur = [logits(steps[0], hh) for hh in range(2)]
    state = [(None, None), (None, None)]
    for s, step in enumerate(steps):
        qi, kt = step
        nxt = [None, None]
        for hh in range(2):
            if s + 1 < len(steps):
                nxt[hh] = logits(steps[s + 1], hh)
            state[hh] = update(step, hh, *cur[hh], *state[hh])
        cur = nxt
        if kt == qi:
            a0, a1 = state[0][1], state[1][1]
            pad = PV_ROWS - HEAD_DIM
            out = jnp.concatenate([a0[0:HEAD_DIM] / a0[HEAD_DIM:HEAD_DIM + 1],
                                   a1[pad:] / a1[pad - 1:pad]], axis=0)
            o_ref[qi * TQ:(qi + 1) * TQ, :] = out.astype(BF16).T
            state = [(None, None), (None, None)]


def _attention(q8, k8, v8, seq):
    _, t, _ = q8.shape
    nb = t // seq
    pair_spec = pl.BlockSpec((2, seq, LANES), lambda b, p: (p, b, 0))
    return pl.pallas_call(
        _attn_kernel,
        grid=(nb, PAIRS),
        in_specs=[pair_spec, pair_spec, pair_spec],
        out_specs=pl.BlockSpec((seq, LANES), lambda b, p: (b, p)),
        out_shape=jax.ShapeDtypeStruct((t, A_WIDTH), BF16),
        compiler_params=pltpu.CompilerParams(
            dimension_semantics=("arbitrary", "arbitrary"), vmem_limit_bytes=VMEM_LIMIT),
        name="attn",
    )(q8, k8, v8)


def _tail_kernel(tiles_per_seq, o_ref, pu_ref, puh_ref, cv_ref, cvh_ref, x_ref, mod_ref,
                 gmix_ref, wgl_ref, wpool_ref, pscale_ref, convw_ref, wbr_ref, wout_ref, g_ref,
                 gpre_ref, w1_ref, w2_ref, gpost_ref, out_ref, x1_ref):
    i = pl.program_id(0)
    shift_m, scale_m, gate_m = mod_ref[0, 0], mod_ref[0, 1], mod_ref[0, 2]
    shift_f, scale_f, gate_f = mod_ref[0, 3], mod_ref[0, 4], mod_ref[0, 5]
    tile_in_seq = i % tiles_per_seq
    tm, d = x_ref.shape
    n_sub = tm // SUB
    lane = lax.broadcasted_iota(jnp.int32, (SUB + HALO, POOL_WIDTH), 1)
    row = lax.broadcasted_iota(jnp.int32, (SUB + HALO, POOL_WIDTH), 0)
    gd = POOL_WIDTH // len(POOL_WINDOWS)
    wsz = jnp.where(lane < gd, 2.0, jnp.where(lane < 2 * gd, 4.0, jnp.where(lane < 3 * gd, 8.0, 16.0)))
    cw = convw_ref[...]
    o1 = A_WIDTH + POOL_WIDTH

    def with_history(ref, halo_ref, j):
        if j == 0:
            head = jnp.where(tile_in_seq != 0, halo_ref[...].astype(F32), 0.0)
        else:
            head = ref[j * SUB - HALO:j * SUB, :].astype(F32)
        return jnp.concatenate([head, ref[j * SUB:(j + 1) * SUB, :].astype(F32)], axis=0)

    def mix(j):
        rows = slice(j * SUB, (j + 1) * SUB)
        ext = with_history(pu_ref, puh_ref, j)
        a2 = ext + pltpu.roll(ext, 1, 0)
        a4 = a2 + pltpu.roll(a2, 2, 0)
        a8 = a4 + pltpu.roll(a4, 4, 0)
        a16 = a8 + pltpu.roll(a8, 8, 0)
        win = jnp.where(lane < gd, a2, jnp.where(lane < 2 * gd, a4, jnp.where(lane < 3 * gd, a8, a16)))
        frames = (tile_in_seq * tm + j * SUB + row - (HALO - 1)).astype(F32)
        cnt = jnp.maximum(jnp.minimum(frames, wsz), 1.0)
        pm = (win / cnt - ext)[HALO:]
        br_b = jnp.dot(pm.astype(BF16), wpool_ref[...], preferred_element_type=F32) * pscale_ref[...]

        cve = with_history(cv_ref, cvh_ref, j)
        u = cve[:, 2 * CONV_WIDTH:] * cve[:, :CONV_WIDTH]
        y = cw[0:1] * pltpu.roll(u, 2, 0) + cw[1:2] * pltpu.roll(u, 1, 0) + cw[2:3] * u
        br_c = (cve[:, CONV_WIDTH:2 * CONV_WIDTH] * y)[HALO:]

        hb = (_rms(x_ref[rows, :], gmix_ref[...]) * (1.0 + scale_m) + shift_m).astype(BF16)

        def gated(n, branch, lo, hi):
            b = jnp.dot(branch, wbr_ref[lo:hi, :], preferred_element_type=F32)
            return b + b * jnp.tanh(_nt_dot(hb, wgl_ref[n * d:(n + 1) * d, :]))

        merged = gated(0, o_ref[rows, :], 0, A_WIDTH)
        merged += gated(1, br_b.astype(BF16), A_WIDTH, o1)
        merged += gated(2, br_c.astype(BF16), o1, o1 + CONV_WIDTH)
        return merged.astype(BF16)

    def project(j, merged):
        rows = slice(j * SUB, (j + 1) * SUB)
        yo = jnp.dot(merged, wout_ref[...], preferred_element_type=F32)
        x1_ref[rows, :] = x_ref[rows, :] + gate_m * _rms(yo, g_ref[...])

    cur = mix(0)
    for j in range(n_sub):
        nxt = mix(j + 1) if j + 1 < n_sub else None
        project(j, cur)
        cur = nxt

    x1 = x1_ref[...]
    hb = (_rms(x1, gpre_ref[...]) * (1.0 + scale_f) + shift_f).astype(BF16)
    dff = w1_ref.shape[1]
    acc = jnp.zeros(x1.shape, F32)
    for c in range(dff // FF_CHUNK):
        up = jnp.dot(hb, w1_ref[:, c * FF_CHUNK:(c + 1) * FF_CHUNK], preferred_element_type=F32)
        act = jnp.square(jnp.maximum(up, 0.0)).astype(BF16)
        acc += jnp.dot(act, w2_ref[c * FF_CHUNK:(c + 1) * FF_CHUNK, :], preferred_element_type=F32)
    out_ref[...] = x1 + gate_f * _rms(acc, gpost_ref[...])


def _tail(o, pu, cv, x2, mod, l, g_mix_pre, wgl_bf, wpool, pscale, convw, wbr_bf, wout_bf, g_mix_post, g_ff_pre,
          w1_bf, w2_bf, g_ff_post, seq):
    t, d = x2.shape
    nt = t // TMT
    tiles_per_seq = seq // TMT
    hb = TMT // HALO
    mod_row0 = l * (t // seq)

    def halo_map(i):
        return (jnp.maximum(i * hb - 1, 0), 0)

    consts = (g_mix_pre, wgl_bf, wpool, pscale, convw, wbr_bf, wout_bf, g_mix_post, g_ff_pre, w1_bf, w2_bf,
              g_ff_post)
    per_layer = (True, False, True, True, True, False, False, True, True, False, False, True)
    return pl.pallas_call(
        functools.partial(_tail_kernel, tiles_per_seq),
        grid=(nt,),
        in_specs=[
            pl.BlockSpec((TMT, A_WIDTH), lambda i: (i, 0)),
            pl.BlockSpec((TMT, POOL_WIDTH), lambda i: (i, 0)),
            pl.BlockSpec((HALO, POOL_WIDTH), halo_map),
            pl.BlockSpec((TMT, 3 * CONV_WIDTH), lambda i: (i, 0)),
            pl.BlockSpec((HALO, 3 * CONV_WIDTH), halo_map),
            pl.BlockSpec((TMT, d), lambda i: (i, 0)),
            pl.BlockSpec((1,) + mod.shape[1:], lambda i: (mod_row0 + i // tiles_per_seq, 0, 0, 0)),
        ] + [_layer_spec(a, l) if stacked else _const_spec(a.shape) for a, stacked in zip(consts, per_layer)],
        out_specs=pl.BlockSpec((TMT, d), lambda i: (i, 0)),
        out_shape=jax.ShapeDtypeStruct((t, d), F32),
        scratch_shapes=[pltpu.VMEM((TMT, d), F32)],
        compiler_params=pltpu.CompilerParams(
            dimension_semantics=("arbitrary",), vmem_limit_bytes=VMEM_LIMIT),
        name="tail",
    )(o, pu, pu, cv, cv, x2, mod, *consts)


def _block_diag(w):
    depth, g, c, dd = w.shape
    eye = jnp.eye(g, dtype=w.dtype)
    return (eye[None, :, None, :, None] * w[:, :, :, None, :]).reshape(depth, g * c, g * dd)


def kernel(x, c, w_ada, b_ada, g_mix_pre, g_mix_post, g_ff_pre, g_ff_post, w_in, b_f, w_pool, pool_scale,
           conv_w, w_branch, w_out, w_ff1, w_ff2):
    nb, seq, d = x.shape
    depth = w_ada.shape[0]
    assert all(seq % tile == 0 for tile in (TM, TMT, TQ, TK)) and TMT % SUB == 0
    assert d % LANES == 0 and d % PREP_ROWS == 0
    t = nb * seq
    assert (6 * d) % (ADA_CHUNKS * LANES) == 0
    w_in_t = jnp.swapaxes(w_in, 1, 2)
    mod, w_in_bf = _ada(c, w_ada, b_ada, w_in_t)
    mod = mod.reshape(depth * nb, 6, 1, d)
    x2 = x.reshape(t, d)
    stacked = {"in": w_in_t, "branch": w_branch, "out": w_out, "ff1": w_ff1, "ff2": w_ff2}
    cast_keys = [("in", l) for l in range(1, depth)]
    cast_keys += [(name, l) for l in range(depth) for name in ("branch", "out", "ff1", "ff2")]
    casts = [(stacked[name], l, 0.5 if name == "branch" else 1.0) for name, l in cast_keys]
    bf16_w = {("in", 0): w_in_bf}
    spread = np.zeros((LANES, LANES), np.float32)
    head_of_lane = np.zeros((LANES,), np.int32)
    lane_used = np.zeros((LANES,), bool)
    for hd in range(HEADS):
        spread[_aug_base(hd):_aug_base(hd) + AUG, hd] = 1.0
        head_of_lane[_aug_base(hd):_aug_base(hd) + AUG] = hd
        lane_used[_aug_base(hd):_aug_base(hd) + AUG] = True
    spread = jnp.asarray(spread, BF16)
    bf_all = jnp.where(lane_used, b_f[:, head_of_lane], 0.0)[:, None, :]
    rows = lambda a: a[:, None, :]
    w_pool_bd = _block_diag(w_pool).astype(BF16)
    for l in range(depth):
        (q8, k8, v8, pu, cv, wgl), cast = _inproj(x2, mod, l, rows(g_mix_pre), bf16_w[("in", l)], spread, bf_all,
                                                  seq, casts if l == 0 else [])
        if l == 0:
            bf16_w.update(zip(cast_keys, cast))
        o = _attention(q8, k8, v8, seq)
        x2 = _tail(o, pu, cv, x2, mod, l, rows(g_mix_pre), wgl, w_pool_bd, rows(pool_scale), conv_w,
                   bf16_w[("branch", l)], bf16_w[("out", l)], rows(g_mix_post), rows(g_ff_pre),
                   bf16_w[("ff1", l)], bf16_w[("ff2", l)], rows(g_ff_post), seq)
    return x2.reshape(nb, seq, d)
```

```python
import functools

import jax
import jax.numpy as jnp
import numpy as np
from jax import lax
from jax.experimental import pallas as pl
from jax.experimental.pallas import tpu as pltpu

F32 = jnp.float32
BF16 = jnp.bfloat16

LANES = 128
HEAD_DIM = 64
HEADS = 8
PAIRS = HEADS // 2
A_WIDTH = HEADS * HEAD_DIM
POOL_WINDOWS = (2, 4, 8, 16)
POOL_WIDTH = 256
CONV_WIDTH = 256
HALO = 16
RMS_EPS = 1e-6
NEG_INF = -1e30
LOG2E = 1.4426950408889634
AUG = 6
PV_ROWS = 80
ADA_CHUNKS = 4
VMEM_LIMIT = 56 * 1024 * 1024

TM = 512
TMT = 512
TQ = 512
TK = 512
FF_CHUNK = 2048
SUB = 256
PREP_ROWS = 128


def _const_spec(shape):
    n = len(shape)
    return pl.BlockSpec(shape, lambda *_: (0,) * n, pipeline_mode=pl.Buffered(1))


def _layer_spec(a, l):
    n = a.ndim - 1
    return pl.BlockSpec((None,) + a.shape[1:], lambda *_: (l,) + (0,) * n, pipeline_mode=pl.Buffered(1))


def _rms(x, g):
    ms = jnp.mean(x * x, axis=-1, keepdims=True)
    return x * lax.rsqrt(ms + RMS_EPS) * g


def _row_chunks(k, steps):
    rows = -(-(-(-k // steps)) // 16) * 16
    return rows, -(-k // rows)


def _nt_dot(a, b):
    return lax.dot_general(a, b, (((1,), (1,)), ((), ())), preferred_element_type=F32)


def _ada_kernel(c_ref, w_ref, b_ref, win_ref, o_ref, win_bf_ref):
    c = c_ref[...]
    sc = c * (1.0 / (1.0 + jnp.exp(-c)))
    w = w_ref[0]
    sc_hi, w_hi = sc.astype(BF16), w.astype(BF16)
    sc_lo = (sc - sc_hi.astype(F32)).astype(BF16)
    w_lo = (w - w_hi.astype(F32)).astype(BF16)
    dot = functools.partial(jnp.dot, preferred_element_type=F32)
    o_ref[0] = dot(sc_hi, w_hi) + dot(sc_hi, w_lo) + dot(sc_lo, w_hi) + b_ref[0]
    win_bf_ref[...] = win_ref[0].astype(BF16)


def _ada(c, w_ada, b_ada, w_in_t):
    depth, d, d6 = w_ada.shape
    nb = c.shape[0]
    n = ADA_CHUNKS
    cols = d6 // n
    rows, nblk = _row_chunks(w_in_t.shape[1], depth * n)
    assert nblk == depth * n
    ncol_in = w_in_t.shape[2]
    return pl.pallas_call(
        _ada_kernel,
        grid=(depth, n),
        in_specs=[
            pl.BlockSpec((nb, d), lambda l, j: (0, 0)),
            pl.BlockSpec((1, d, cols), lambda l, j: (l, 0, j)),
            pl.BlockSpec((1, 1, cols), lambda l, j: (l, 0, j)),
            pl.BlockSpec((1, rows, ncol_in), lambda l, j: (0, l * n + j, 0)),
        ],
        out_specs=(
            pl.BlockSpec((1, nb, cols), lambda l, j: (l, 0, j)),
            pl.BlockSpec((rows, ncol_in), lambda l, j: (l * n + j, 0)),
        ),
        out_shape=(
            jax.ShapeDtypeStruct((depth, nb, d6), F32),
            jax.ShapeDtypeStruct(w_in_t.shape[1:], BF16),
        ),
        compiler_params=pltpu.CompilerParams(
            dimension_semantics=("arbitrary", "arbitrary"), vmem_limit_bytes=VMEM_LIMIT),
        name="ada",
    )(c, w_ada, b_ada.reshape(depth, 1, d6), w_in_t)


def _aug_base(h):
    return (HEAD_DIM if h % 2 == 0 else 0) + AUG * (h // 2)


def _inproj_kernel(tiles_per_seq, cast_scales, x_ref, mod_ref, g_ref, w_ref, spread_ref, bf_ref, *refs):
    n_cast = len(cast_scales)
    cast_in = refs[:n_cast]
    q_ref, k_ref, v_ref, pu_ref, cv_ref, wgl_ref = refs[n_cast:n_cast + 6]
    cast_out = refs[n_cast + 6:2 * n_cast + 6]
    carry_ref, wf_ref, wrest_ref = refs[2 * n_cast + 6:]
    i = pl.program_id(0)
    d = x_ref.shape[1]

    for src, dst, s in zip(cast_in, cast_out, cast_scales):
        dst[...] = (src[0] if s == 1.0 else src[0] * s).astype(BF16)

    fo = 3 * A_WIDTH
    n_plain = POOL_WIDTH + 3 * CONV_WIDTH
    n_rest = n_plain + wgl_ref.shape[0]

    @pl.when(i == 0)
    def _():
        wf_ref[...] = jnp.dot(spread_ref[...], w_ref[fo:fo + LANES, :],
                              preferred_element_type=F32).astype(BF16)
        for r in range(0, n_rest, PREP_ROWS):
            end = min(fo + r + PREP_ROWS + 2 * HEADS, w_ref.shape[0])
            blk = w_ref[fo + r:end, :].astype(F32)[HEADS:HEADS + PREP_ROWS]
            if r < n_plain:
                wrest_ref[r:r + PREP_ROWS, :] = blk.astype(BF16)
            else:
                wgl_ref[r - n_plain:r - n_plain + PREP_ROWS, :] = (blk * 0.5).astype(BF16)

    shift, scale = mod_ref[0, 0], mod_ref[0, 1]
    hb = (_rms(x_ref[...], g_ref[...]) * (1.0 + scale) + shift).astype(BF16)
    tm = hb.shape[0]

    def proj(lo_, hi_):
        return _nt_dot(hb, w_ref[lo_:hi_, :])

    def rest(lo_, hi_):
        return _nt_dot(hb, wrest_ref[lo_:hi_, :])

    zf = _nt_dot(hb, wf_ref[...]) + bf_ref[...]
    pu_ref[...] = rest(0, POOL_WIDTH).astype(BF16)
    cv_ref[...] = rest(POOL_WIDTH, n_plain).astype(BF16)
    zq = proj(0, A_WIDTH) * (HEAD_DIM ** -0.5 * LOG2E)
    zk = proj(A_WIDTH, 2 * A_WIDTH)
    zv = proj(2 * A_WIDTH, 3 * A_WIDTH)

    lf = jnp.minimum(zf, 0.0) - jnp.log(1.0 + jnp.exp(-jnp.abs(zf)))
    row = lax.broadcasted_iota(jnp.int32, lf.shape, 0)
    step = 1
    while step < tm:
        lf = lf + jnp.where(row >= step, pltpu.roll(lf, step, 0), 0.0)
        step *= 2

    @pl.when(i % tiles_per_seq == 0)
    def _():
        carry_ref[...] = jnp.zeros_like(carry_ref)

    fc = lf + carry_ref[0:1, :]
    carry_ref[...] = jnp.broadcast_to(fc[tm - 1:tm, :], carry_ref.shape)

    f2 = fc * LOG2E
    hi = f2.astype(BF16).astype(F32)
    rem = f2 - hi
    mid = rem.astype(BF16).astype(F32)
    lo = rem - mid
    lane = lax.broadcasted_iota(jnp.int32, f2.shape, 1)
    j = (lane % HEAD_DIM) % AUG
    part = jnp.where(j % 3 == 0, hi, jnp.where(j % 3 == 1, mid, lo))
    aug_q = jnp.where(j < 3, 1.0, part)
    aug_k = jnp.where(j < 3, -part, 1.0)

    for hd in range(HEADS):
        p = hd // 2
        in_head = (lane >= (hd % 2) * HEAD_DIM) & (lane < (hd % 2 + 1) * HEAD_DIM)
        base = _aug_base(hd)
        in_aug = (lane >= base) & (lane < base + AUG)
        pair = slice(p * LANES, (p + 1) * LANES)
        q_ref[hd] = jnp.where(in_head, zq[:, pair], jnp.where(in_aug, aug_q, 0.0)).astype(BF16)
        k_ref[hd] = jnp.where(in_head, zk[:, pair], jnp.where(in_aug, aug_k, 0.0)).astype(BF16)
        ones_lane = HEAD_DIM if hd % 2 == 0 else HEAD_DIM - 1
        v_ref[hd] = jnp.where(in_head, zv[:, pair], jnp.where(lane == ones_lane, 1.0, 0.0)).astype(BF16)


def _inproj(x2, mod, l, g, w_in_bf, spread, bf, seq, casts):
    t, d = x2.shape
    nt = t // TM
    tiles_per_seq = seq // TM
    mod_row0 = l * (t // seq)
    n_plain = POOL_WIDTH + 3 * CONV_WIDTH
    n_gate = w_in_bf.shape[0] - 3 * A_WIDTH - HEADS - n_plain
    cast_in_specs, cast_out_specs, cast_shapes = [], [], []
    for w, w_layer, _ in casts:
        _, k, n = w.shape
        rows, nblk = _row_chunks(k, nt)
        cast_in_specs.append(pl.BlockSpec(
            (1, rows, n), functools.partial(lambda l_, nb_, i: (l_, jnp.minimum(i, nb_ - 1), 0), w_layer, nblk)))
        cast_out_specs.append(pl.BlockSpec(
            (rows, n), functools.partial(lambda nb_, i: (jnp.minimum(i, nb_ - 1), 0), nblk)))
        cast_shapes.append(jax.ShapeDtypeStruct((k, n), BF16))
    out_shape = (
        jax.ShapeDtypeStruct((HEADS, t, LANES), BF16),
        jax.ShapeDtypeStruct((HEADS, t, LANES), BF16),
        jax.ShapeDtypeStruct((HEADS, t, LANES), BF16),
        jax.ShapeDtypeStruct((t, POOL_WIDTH), BF16),
        jax.ShapeDtypeStruct((t, 3 * CONV_WIDTH), BF16),
        jax.ShapeDtypeStruct((n_gate, d), BF16),
    )
    head_spec = pl.BlockSpec((HEADS, TM, LANES), lambda i: (0, i, 0))
    out_specs = (
        head_spec, head_spec, head_spec,
        pl.BlockSpec((TM, POOL_WIDTH), lambda i: (i, 0)),
        pl.BlockSpec((TM, 3 * CONV_WIDTH), lambda i: (i, 0)),
        pl.BlockSpec((n_gate, d), lambda i: (0, 0)),
    )
    outs = pl.pallas_call(
        functools.partial(_inproj_kernel, tiles_per_seq, tuple(s for _, _, s in casts)),
        grid=(nt,),
        in_specs=[
            pl.BlockSpec((TM, d), lambda i: (i, 0)),
            pl.BlockSpec((1, 2, 1, d), lambda i: (mod_row0 + i // tiles_per_seq, 0, 0, 0)),
            _layer_spec(g, l),
            _const_spec(w_in_bf.shape),
            _const_spec(spread.shape),
            _layer_spec(bf, l),
        ] + cast_in_specs,
        out_specs=out_specs + tuple(cast_out_specs),
        out_shape=out_shape + tuple(cast_shapes),
        scratch_shapes=[
            pltpu.VMEM((8, LANES), F32),
            pltpu.VMEM((LANES, d), BF16),
            pltpu.VMEM((n_plain, d), BF16),
        ],
        compiler_params=pltpu.CompilerParams(
            dimension_semantics=("arbitrary",), vmem_limit_bytes=VMEM_LIMIT),
        name="inproj",
    )(x2, mod, g, w_in_bf, spread, bf, *[w for w, _, _ in casts])
    return outs[:6], outs[6:]


def _attn_kernel(q_ref, k_ref, v_ref, o_ref):
    seq = q_ref.shape[1]
    nq = seq // TQ
    half = TK // 2
    steps = [(qi, kt) for qi in range(nq) for kt in range(qi + 1)]
    mask0 = (lax.broadcasted_iota(jnp.int32, (half, TQ), 0)
             <= lax.broadcasted_iota(jnp.int32, (half, TQ), 1))
    mask1 = (lax.broadcasted_iota(jnp.int32, (half, half), 0)
             <= lax.broadcasted_iota(jnp.int32, (half, half), 1))
    vts = [v_ref[hh].T[hh * (LANES - PV_ROWS):hh * (LANES - PV_ROWS) + PV_ROWS]
           for hh in range(2)]

    def logits(step, hh):
        qi, kt = step
        q = q_ref[hh, qi * TQ:(qi + 1) * TQ, :]
        if kt < qi:
            st = _nt_dot(k_ref[hh, kt * TK:(kt + 1) * TK, :], q)
            return (st,), jnp.max(st, axis=0, keepdims=True)
        st0 = jnp.where(mask0, _nt_dot(k_ref[hh, kt * TK:kt * TK + half, :], q), NEG_INF)
        st1 = jnp.where(mask1, _nt_dot(k_ref[hh, kt * TK + half:(kt + 1) * TK, :], q[half:]), NEG_INF)
        c0 = jnp.max(st0, axis=0, keepdims=True)
        c1 = jnp.max(st1, axis=0, keepdims=True)
        return (st0, st1), jnp.concatenate([c0[:, :half], jnp.maximum(c0[:, half:], c1)], axis=1)

    def update(step, hh, sts, cmax, m, acc):
        qi, kt = step
        vt = vts[hh]
        m_new = cmax if m is None else jnp.maximum(m, cmax)
        if kt < qi:
            pt = jnp.exp2(sts[0] - m_new).astype(BF16)
            pv = jnp.dot(vt[:, kt * TK:(kt + 1) * TK], pt, preferred_element_type=F32)
        else:
            pt0 = jnp.exp2(sts[0] - m_new).astype(BF16)
            pt1 = jnp.exp2(sts[1] - m_new[:, half:]).astype(BF16)
            pv = jnp.dot(vt[:, kt * TK:kt * TK + half], pt0, preferred_element_type=F32)
            pv1 = jnp.dot(vt[:, kt * TK + half:(kt + 1) * TK], pt1, preferred_element_type=F32)
            pv = jnp.concatenate([pv[:, :half], pv[:, half:] + pv1], axis=1)
        if m is not None:
            pv = jnp.exp2(m - m_new) * acc + pv
        return m_new, pv

    cur = [logits(steps[0], hh) for hh in range(2)]
    state = [(None, None), (None, None)]
    for s, step in enumerate(steps):
        qi, kt = step
        nxt = [None, None]
        for hh in range(2):
            if s + 1 < len(steps):
                nxt[hh] = logits(steps[s + 1], hh)
            state[hh] = update(step, hh, *cur[hh], *state[hh])
        cur = nxt
        if kt == qi:
            a0, a1 = state[0][1], state[1][1]
            pad = PV_ROWS - HEAD_DIM
            out = jnp.concatenate([a0[0:HEAD_DIM] / a0[HEAD_DIM:HEAD_DIM + 1],
                                   a1[pad:] / a1[pad - 1:pad]], axis=0)
            o_ref[qi * TQ:(qi + 1) * TQ, :] = out.astype(BF16).T
            state = [(None, None), (None, None)]


def _attention(q8, k8, v8, seq):
    _, t, _ = q8.shape
    nb = t // seq
    pair_spec = pl.BlockSpec((2, seq, LANES), lambda b, p: (p, b, 0))
    return pl.pallas_call(
        _attn_kernel,
        grid=(nb, PAIRS),
        in_specs=[pair_spec, pair_spec, pair_spec],
        out_specs=pl.BlockSpec((seq, LANES), lambda b, p: (b, p)),
        out_shape=jax.ShapeDtypeStruct((t, A_WIDTH), BF16),
        compiler_params=pltpu.CompilerParams(
            dimension_semantics=("arbitrary", "arbitrary"), vmem_limit_bytes=VMEM_LIMIT),
        name="attn",
    )(q8, k8, v8)


def _tail_kernel(tiles_per_seq, o_ref, pu_ref, puh_ref, cv_ref, cvh_ref, x_ref, mod_ref,
                 gmix_ref, wgl_ref, wpool_ref, pscale_ref, convw_ref, wbr_ref, wout_ref, g_ref,
                 gpre_ref, w1_ref, w2_ref, gpost_ref, out_ref, x1_ref):
    i = pl.program_id(0)
    shift_m, scale_m, gate_m = mod_ref[0, 0], mod_ref[0, 1], mod_ref[0, 2]
    shift_f, scale_f, gate_f = mod_ref[0, 3], mod_ref[0, 4], mod_ref[0, 5]
    tile_in_seq = i % tiles_per_seq
    tm, d = x_ref.shape
    n_sub = tm // SUB
    lane = lax.broadcasted_iota(jnp.int32, (SUB + HALO, POOL_WIDTH), 1)
    row = lax.broadcasted_iota(jnp.int32, (SUB + HALO, POOL_WIDTH), 0)
    gd = POOL_WIDTH // len(POOL_WINDOWS)
    wsz = jnp.where(lane < gd, 2.0, jnp.where(lane < 2 * gd, 4.0, jnp.where(lane < 3 * gd, 8.0, 16.0)))
    cw = convw_ref[...]
    o1 = A_WIDTH + POOL_WIDTH

    def with_history(ref, halo_ref, j):
        if j == 0:
            head = jnp.where(tile_in_seq != 0, halo_ref[...].astype(F32), 0.0)
        else:
            head = ref[j * SUB - HALO:j * SUB, :].astype(F32)
        return jnp.concatenate([head, ref[j * SUB:(j + 1) * SUB, :].astype(F32)], axis=0)

    def mix(j):
        rows = slice(j * SUB, (j + 1) * SUB)
        ext = with_history(pu_ref, puh_ref, j)
        a2 = ext + pltpu.roll(ext, 1, 0)
        a4 = a2 + pltpu.roll(a2, 2, 0)
        a8 = a4 + pltpu.roll(a4, 4, 0)
        a16 = a8 + pltpu.roll(a8, 8, 0)
        win = jnp.where(lane < gd, a2, jnp.where(lane < 2 * gd, a4, jnp.where(lane < 3 * gd, a8, a16)))
        frames = (tile_in_seq * tm + j * SUB + row - (HALO - 1)).astype(F32)
        cnt = jnp.maximum(jnp.minimum(frames, wsz), 1.0)
        pm = (win / cnt - ext)[HALO:]
        br_b = jnp.dot(pm.astype(BF16), wpool_ref[...], preferred_element_type=F32) * pscale_ref[...]

        cve = with_history(cv_ref, cvh_ref, j)
        u = cve[:, 2 * CONV_WIDTH:] * cve[:, :CONV_WIDTH]
        y = cw[0:1] * pltpu.roll(u, 2, 0) + cw[1:2] * pltpu.roll(u, 1, 0) + cw[2:3] * u
        br_c = (cve[:, CONV_WIDTH:2 * CONV_WIDTH] * y)[HALO:]

        hb = (_rms(x_ref[rows, :], gmix_ref[...]) * (1.0 + scale_m) + shift_m).astype(BF16)

        def gated(n, branch, lo, hi):
            b = jnp.dot(branch, wbr_ref[lo:hi, :], preferred_element_type=F32)
            return b + b * jnp.tanh(_nt_dot(hb, wgl_ref[n * d:(n + 1) * d, :]))

        merged = gated(0, o_ref[rows, :], 0, A_WIDTH)
        merged += gated(1, br_b.astype(BF16), A_WIDTH, o1)
        merged += gated(2, br_c.astype(BF16), o1, o1 + CONV_WIDTH)
        return merged.astype(BF16)

    def project(j, merged):
        rows = slice(j * SUB, (j + 1) * SUB)
        yo = jnp.dot(merged, wout_ref[...], preferred_element_type=F32)
        x1_ref[rows, :] = x_ref[rows, :] + gate_m * _rms(yo, g_ref[...])

    cur = mix(0)
    for j in range(n_sub):
        nxt = mix(j + 1) if j + 1 < n_sub else None
        project(j, cur)
        cur = nxt

    x1 = x1_ref[...]
    hb = (_rms(x1, gpre_ref[...]) * (1.0 + scale_f) + shift_f).astype(BF16)
    dff = w1_ref.shape[1]
    acc = jnp.zeros(x1.shape, F32)
    for c in range(dff // FF_CHUNK):
        up = jnp.dot(hb, w1_ref[:, c * FF_CHUNK:(c + 1) * FF_CHUNK], preferred_element_type=F32)
        act = jnp.square(jnp.maximum(up, 0.0)).astype(BF16)
        acc += jnp.dot(act, w2_ref[c * FF_CHUNK:(c + 1) * FF_CHUNK, :], preferred_element_type=F32)
    out_ref[...] = x1 + gate_f * _rms(acc, gpost_ref[...])


def _tail(o, pu, cv, x2, mod, l, g_mix_pre, wgl_bf, wpool, pscale, convw, wbr_bf, wout_bf, g_mix_post, g_ff_pre,
          w1_bf, w2_bf, g_ff_post, seq):
    t, d = x2.shape
    nt = t // TMT
    tiles_per_seq = seq // TMT
    hb = TMT // HALO
    mod_row0 = l * (t // seq)

    def halo_map(i):
        return (jnp.maximum(i * hb - 1, 0), 0)

    consts = (g_mix_pre, wgl_bf, wpool, pscale, convw, wbr_bf, wout_bf, g_mix_post, g_ff_pre, w1_bf, w2_bf,
              g_ff_post)
    per_layer = (True, False, True, True, True, False, False, True, True, False, False, True)
    return pl.pallas_call(
        functools.partial(_tail_kernel, tiles_per_seq),
        grid=(nt,),
        in_specs=[
            pl.BlockSpec((TMT, A_WIDTH), lambda i: (i, 0)),
            pl.BlockSpec((TMT, POOL_WIDTH), lambda i: (i, 0)),
            pl.BlockSpec((HALO, POOL_WIDTH), halo_map),
            pl.BlockSpec((TMT, 3 * CONV_WIDTH), lambda i: (i, 0)),
            pl.BlockSpec((HALO, 3 * CONV_WIDTH), halo_map),
            pl.BlockSpec((TMT, d), lambda i: (i, 0)),
            pl.BlockSpec((1,) + mod.shape[1:], lambda i: (mod_row0 + i // tiles_per_seq, 0, 0, 0)),
        ] + [_layer_spec(a, l) if stacked else _const_spec(a.shape) for a, stacked in zip(consts, per_layer)],
        out_specs=pl.BlockSpec((TMT, d), lambda i: (i, 0)),
        out_shape=jax.ShapeDtypeStruct((t, d), F32),
        scratch_shapes=[pltpu.VMEM((TMT, d), F32)],
        compiler_params=pltpu.CompilerParams(
            dimension_semantics=("arbitrary",), vmem_limit_bytes=VMEM_LIMIT),
        name="tail",
    )(o, pu, pu, cv, cv, x2, mod, *consts)


def _block_diag(w):
    depth, g, c, dd = w.shape
    eye = jnp.eye(g, dtype=w.dtype)
    return (eye[None, :, None, :, None] * w[:, :, :, None, :]).reshape(depth, g * c, g * dd)


def kernel(x, c, w_ada, b_ada, g_mix_pre, g_mix_post, g_ff_pre, g_ff_post, w_in, b_f, w_pool, pool_scale,
           conv_w, w_branch, w_out, w_ff1, w_ff2):
    nb, seq, d = x.shape
    depth = w_ada.shape[0]
    assert all(seq % tile == 0 for tile in (TM, TMT, TQ, TK)) and TMT % SUB == 0
    assert d % LANES == 0 and d % PREP_ROWS == 0
    t = nb * seq
    assert (6 * d) % (ADA_CHUNKS * LANES) == 0
    w_in_t = jnp.swapaxes(w_in, 1, 2)
    mod, w_in_bf = _ada(c, w_ada, b_ada, w_in_t)
    mod = mod.reshape(depth * nb, 6, 1, d)
    x2 = x.reshape(t, d)
    stacked = {"in": w_in_t, "branch": w_branch, "out": w_out, "ff1": w_ff1, "ff2": w_ff2}
    cast_keys = [("in", l) for l in range(1, depth)]
    cast_keys += [(name, l) for l in range(depth) for name in ("branch", "out", "ff1", "ff2")]
    casts = [(stacked[name], l, 0.5 if name == "branch" else 1.0) for name, l in cast_keys]
    bf16_w = {("in", 0): w_in_bf}
    spread = np.zeros((LANES, LANES), np.float32)
    head_of_lane = np.zeros((LANES,), np.int32)
    lane_used = np.zeros((LANES,), bool)
    for hd in range(HEADS):
        spread[_aug_base(hd):_aug_base(hd) + AUG, hd] = 1.0
        head_of_lane[_aug_base(hd):_aug_base(hd) + AUG] = hd
        lane_used[_aug_base(hd):_aug_base(hd) + AUG] = True
    spread = jnp.asarray(spread, BF16)
    bf_all = jnp.where(lane_used, b_f[:, head_of_lane], 0.0)[:, None, :]
    rows = lambda a: a[:, None, :]
    w_pool_bd = _block_diag(w_pool).astype(BF16)
    for l in range(depth):
        (q8, k8, v8, pu, cv, wgl), cast = _inproj(x2, mod, l, rows(g_mix_pre), bf16_w[("in", l)], spread, bf_all,
                                                  seq, casts if l == 0 else [])
        if l == 0:
            bf16_w.update(zip(cast_keys, cast))
        o = _attention(q8, k8, v8, seq)
        x2 = _tail(o, pu, cv, x2, mod, l, rows(g_mix_pre), wgl, w_pool_bd, rows(pool_scale), conv_w,
                   bf16_w[("branch", l)], bf16_w[("out", l)], rows(g_mix_post), rows(g_ff_pre),
                   bf16_w[("ff1", l)], bf16_w[("ff2", l)], rows(g_ff_post), seq)
    return x2.reshape(nb, seq, d)
```

```python
import functools

import jax
import jax.numpy as jnp
import numpy as np
from jax import lax
from jax.experimental import pallas as pl
from jax.experimental.pallas import tpu as pltpu

F32 = jnp.float32
BF16 = jnp.bfloat16

LANES = 128
HEAD_DIM = 64
HEADS = 8
PAIRS = HEADS // 2
A_WIDTH = HEADS * HEAD_DIM
POOL_WINDOWS = (2, 4, 8, 16)
POOL_WIDTH = 256
CONV_WIDTH = 256
HALO = 16
RMS_EPS = 1e-6
NEG_INF = -1e30
LOG2E = 1.4426950408889634
AUG = 6
PV_ROWS = 128
ADA_CHUNKS = 4
VMEM_LIMIT = 56 * 1024 * 1024

TM = 512
TMT = 512
TQ = 512
TK = 512
FF_CHUNK = 1024
SUB = 256
PREP_ROWS = 128


def _const_spec(shape):
    n = len(shape)
    return pl.BlockSpec(shape, lambda *_: (0,) * n, pipeline_mode=pl.Buffered(1))


def _layer_spec(a, l):
    n = a.ndim - 1
    return pl.BlockSpec((None,) + a.shape[1:], lambda *_: (l,) + (0,) * n, pipeline_mode=pl.Buffered(1))


def _rms(x, g):
    ms = jnp.mean(x * x, axis=-1, keepdims=True)
    return x * lax.rsqrt(ms + RMS_EPS) * g


def _row_chunks(k, steps):
    rows = -(-(-(-k // steps)) // 16) * 16
    return rows, -(-k // rows)


def _nt_dot(a, b):
    return lax.dot_general(a, b, (((1,), (1,)), ((), ())), preferred_element_type=F32)


def _ada_kernel(c_ref, w_ref, b_ref, win_ref, o_ref, win_bf_ref):
    c = c_ref[...]
    sc = c * (1.0 / (1.0 + jnp.exp(-c)))
    w = w_ref[0]
    sc_hi, w_hi = sc.astype(BF16), w.astype(BF16)
    sc_lo = (sc - sc_hi.astype(F32)).astype(BF16)
    w_lo = (w - w_hi.astype(F32)).astype(BF16)
    dot = functools.partial(jnp.dot, preferred_element_type=F32)
    bias = b_ref[pl.ds(pl.program_id(0), 1), :]
    o_ref[0] = dot(sc_hi, w_hi) + dot(sc_hi, w_lo) + dot(sc_lo, w_hi) + bias
    win_bf_ref[...] = win_ref[0].astype(BF16)


def _ada(c, w_ada, b_ada, w_in_t):
    depth, d, d6 = w_ada.shape
    nb = c.shape[0]
    n = ADA_CHUNKS
    cols = d6 // n
    rows, nblk = _row_chunks(w_in_t.shape[1], depth * n)
    assert nblk == depth * n
    ncol_in = w_in_t.shape[2]
    return pl.pallas_call(
        _ada_kernel,
        grid=(depth, n),
        in_specs=[
            pl.BlockSpec((nb, d), lambda l, j: (0, 0)),
            pl.BlockSpec((1, d, cols), lambda l, j: (l, 0, j)),
            pl.BlockSpec((depth, cols), lambda l, j: (0, j)),
            pl.BlockSpec((1, rows, ncol_in), lambda l, j: (0, l * n + j, 0)),
        ],
        out_specs=(
            pl.BlockSpec((1, nb, cols), lambda l, j: (l, 0, j)),
            pl.BlockSpec((rows, ncol_in), lambda l, j: (l * n + j, 0)),
        ),
        out_shape=(
            jax.ShapeDtypeStruct((depth, nb, d6), F32),
            jax.ShapeDtypeStruct(w_in_t.shape[1:], BF16),
        ),
        compiler_params=pltpu.CompilerParams(
            dimension_semantics=("arbitrary", "arbitrary"), vmem_limit_bytes=VMEM_LIMIT),
        name="ada",
    )(c, w_ada, b_ada, w_in_t)


def _aug_base(h):
    return (HEAD_DIM if h % 2 == 0 else 0) + AUG * (h // 2)


def _inproj_kernel(tiles_per_seq, cast_scales, x_ref, mod_ref, g_ref, w_ref, spread_ref, bf_ref, *refs):
    n_cast = len(cast_scales)
    cast_in = refs[:n_cast]
    q_ref, k_ref, v_ref, pu_ref, cv_ref, wgl_ref = refs[n_cast:n_cast + 6]
    cast_out = refs[n_cast + 6:2 * n_cast + 6]
    carry_ref, wf_ref, wrest_ref = refs[2 * n_cast + 6:]
    i = pl.program_id(0)
    d = x_ref.shape[1]

    for src, dst, s in zip(cast_in, cast_out, cast_scales):
        dst[...] = (src[0] if s == 1.0 else src[0] * s).astype(BF16)

    fo = 3 * A_WIDTH
    n_plain = POOL_WIDTH + 3 * CONV_WIDTH
    n_rest = n_plain + wgl_ref.shape[0]

    @pl.when(i == 0)
    def _():
        wf_ref[...] = jnp.dot(spread_ref[...], w_ref[fo:fo + LANES, :],
                              preferred_element_type=F32).astype(BF16)
        for r in range(0, n_rest, PREP_ROWS):
            end = min(fo + r + PREP_ROWS + 2 * HEADS, w_ref.shape[0])
            blk = w_ref[fo + r:end, :].astype(F32)[HEADS:HEADS + PREP_ROWS]
            if r < n_plain:
                wrest_ref[r:r + PREP_ROWS, :] = blk.astype(BF16)
            else:
                wgl_ref[r - n_plain:r - n_plain + PREP_ROWS, :] = (blk * 0.5).astype(BF16)

    seq_row = pl.ds(i // tiles_per_seq, 1)
    shift, scale = mod_ref[0, seq_row, 0:d], mod_ref[0, seq_row, d:2 * d]
    hb = (_rms(x_ref[...], g_ref[...]) * (1.0 + scale) + shift).astype(BF16)
    tm = hb.shape[0]

    def proj(lo_, hi_):
        return _nt_dot(hb, w_ref[lo_:hi_, :])

    def rest(lo_, hi_):
        return _nt_dot(hb, wrest_ref[lo_:hi_, :])

    zf = _nt_dot(hb, wf_ref[...]) + bf_ref[...]
    pu_ref[...] = rest(0, POOL_WIDTH).astype(BF16)
    cv_ref[...] = rest(POOL_WIDTH, n_plain).astype(BF16)
    zq = proj(0, A_WIDTH) * (HEAD_DIM ** -0.5 * LOG2E)
    zk = proj(A_WIDTH, 2 * A_WIDTH)
    zv = proj(2 * A_WIDTH, 3 * A_WIDTH)

    lf = jnp.minimum(zf, 0.0) - jnp.log(1.0 + jnp.exp(-jnp.abs(zf)))
    row = lax.broadcasted_iota(jnp.int32, lf.shape, 0)
    step = 1
    while step < tm:
        lf = lf + jnp.where(row >= step, pltpu.roll(lf, step, 0), 0.0)
        step *= 2

    @pl.when(i % tiles_per_seq == 0)
    def _():
        carry_ref[...] = jnp.zeros_like(carry_ref)

    fc = lf + carry_ref[0:1, :]
    carry_ref[...] = jnp.broadcast_to(fc[tm - 1:tm, :], carry_ref.shape)

    f2 = fc * LOG2E
    hi = f2.astype(BF16).astype(F32)
    rem = f2 - hi
    mid = rem.astype(BF16).astype(F32)
    lo = rem - mid
    lane = lax.broadcasted_iota(jnp.int32, f2.shape, 1)
    j = (lane % HEAD_DIM) % AUG
    part = jnp.where(j % 3 == 0, hi, jnp.where(j % 3 == 1, mid, lo))
    aug_q = jnp.where(j < 3, 1.0, part)
    aug_k = jnp.where(j < 3, -part, 1.0)

    for hd in range(HEADS):
        p = hd // 2
        in_head = (lane >= (hd % 2) * HEAD_DIM) & (lane < (hd % 2 + 1) * HEAD_DIM)
        base = _aug_base(hd)
        in_aug = (lane >= base) & (lane < base + AUG)
        pair = slice(p * LANES, (p + 1) * LANES)
        q_ref[hd] = jnp.where(in_head, zq[:, pair], jnp.where(in_aug, aug_q, 0.0)).astype(BF16)
        k_ref[hd] = jnp.where(in_head, zk[:, pair], jnp.where(in_aug, aug_k, 0.0)).astype(BF16)
        ones_lane = HEAD_DIM if hd % 2 == 0 else HEAD_DIM - 1
        v_ref[hd] = jnp.where(in_head, zv[:, pair], jnp.where(lane == ones_lane, 1.0, 0.0)).astype(BF16)


def _inproj(x2, mod, l, g, w_in_bf, spread, bf, seq, casts):
    t, d = x2.shape
    nt = t // TM
    tiles_per_seq = seq // TM
    n_plain = POOL_WIDTH + 3 * CONV_WIDTH
    n_gate = w_in_bf.shape[0] - 3 * A_WIDTH - HEADS - n_plain
    cast_in_specs, cast_out_specs, cast_shapes = [], [], []
    for w, w_layer, _ in casts:
        _, k, n = w.shape
        rows, nblk = _row_chunks(k, nt)
        cast_in_specs.append(pl.BlockSpec(
            (1, rows, n), functools.partial(lambda l_, nb_, i: (l_, jnp.minimum(i, nb_ - 1), 0), w_layer, nblk)))
        cast_out_specs.append(pl.BlockSpec(
            (rows, n), functools.partial(lambda nb_, i: (jnp.minimum(i, nb_ - 1), 0), nblk)))
        cast_shapes.append(jax.ShapeDtypeStruct((k, n), BF16))
    out_shape = (
        jax.ShapeDtypeStruct((HEADS, t, LANES), BF16),
        jax.ShapeDtypeStruct((HEADS, t, LANES), BF16),
        jax.ShapeDtypeStruct((HEADS, t, LANES), BF16),
        jax.ShapeDtypeStruct((t, POOL_WIDTH), BF16),
        jax.ShapeDtypeStruct((t, 3 * CONV_WIDTH), BF16),
        jax.ShapeDtypeStruct((n_gate, d), BF16),
    )
    head_spec = pl.BlockSpec((HEADS, TM, LANES), lambda i: (0, i, 0))
    out_specs = (
        head_spec, head_spec, head_spec,
        pl.BlockSpec((TM, POOL_WIDTH), lambda i: (i, 0)),
        pl.BlockSpec((TM, 3 * CONV_WIDTH), lambda i: (i, 0)),
        pl.BlockSpec((n_gate, d), lambda i: (0, 0)),
    )
    outs = pl.pallas_call(
        functools.partial(_inproj_kernel, tiles_per_seq, tuple(s for _, _, s in casts)),
        grid=(nt,),
        in_specs=[
            pl.BlockSpec((TM, d), lambda i: (i, 0)),
            pl.BlockSpec((1, mod.shape[1], 2 * d), lambda i: (l, 0, 0), pipeline_mode=pl.Buffered(1)),
            _layer_spec(g, l),
            _const_spec(w_in_bf.shape),
            _const_spec(spread.shape),
            _layer_spec(bf, l),
        ] + cast_in_specs,
        out_specs=out_specs + tuple(cast_out_specs),
        out_shape=out_shape + tuple(cast_shapes),
        scratch_shapes=[
            pltpu.VMEM((8, LANES), F32),
            pltpu.VMEM((LANES, d), BF16),
            pltpu.VMEM((n_plain, d), BF16),
        ],
        compiler_params=pltpu.CompilerParams(
            dimension_semantics=("arbitrary",), vmem_limit_bytes=VMEM_LIMIT),
        name="inproj",
    )(x2, mod, g, w_in_bf, spread, bf, *[w for w, _, _ in casts])
    return outs[:6], outs[6:]


def _attn_kernel(q_ref, k_ref, v_ref, o_ref):
    seq = q_ref.shape[1]
    nq = seq // TQ
    half = TK // 2
    steps = [(qi, kt) for qi in range(nq) for kt in range(qi + 1)]
    mask0 = (lax.broadcasted_iota(jnp.int32, (half, TQ), 0)
             <= lax.broadcasted_iota(jnp.int32, (half, TQ), 1))
    mask1 = (lax.broadcasted_iota(jnp.int32, (half, half), 0)
             <= lax.broadcasted_iota(jnp.int32, (half, half), 1))
    vts = [v_ref[hh].T[hh * (LANES - PV_ROWS):hh * (LANES - PV_ROWS) + PV_ROWS]
           for hh in range(2)]

    def logits(step, hh):
        qi, kt = step
        q = q_ref[hh, qi * TQ:(qi + 1) * TQ, :]
        if kt < qi:
            st = _nt_dot(k_ref[hh, kt * TK:(kt + 1) * TK, :], q)
            return (st,), jnp.max(st, axis=0, keepdims=True)
        st0 = jnp.where(mask0, _nt_dot(k_ref[hh, kt * TK:kt * TK + half, :], q), NEG_INF)
        st1 = jnp.where(mask1, _nt_dot(k_ref[hh, kt * TK + half:(kt + 1) * TK, :], q[half:]), NEG_INF)
        c0 = jnp.max(st0, axis=0, keepdims=True)
        c1 = jnp.max(st1, axis=0, keepdims=True)
        return (st0, st1), jnp.concatenate([c0[:, :half], jnp.maximum(c0[:, half:], c1)], axis=1)

    def update(step, hh, sts, cmax, m, acc):
        qi, kt = step
        vt = vts[hh]
        m_new = cmax if m is None else jnp.maximum(m, cmax)
        if kt < qi:
            pt = jnp.exp2(sts[0] - m_new).astype(BF16)
            pv = jnp.dot(vt[:, kt * TK:(kt + 1) * TK], pt, preferred_element_type=F32)
        else:
            pt0 = jnp.exp2(sts[0] - m_new).astype(BF16)
            pt1 = jnp.exp2(sts[1] - m_new[:, half:]).astype(BF16)
            pv = jnp.dot(vt[:, kt * TK:kt * TK + half], pt0, preferred_element_type=F32)
            pv1 = jnp.dot(vt[:, kt * TK + half:(kt + 1) * TK], pt1, preferred_element_type=F32)
            pv = jnp.concatenate([pv[:, :half], pv[:, half:] + pv1], axis=1)
        if m is not None:
            pv = jnp.exp2(m - m_new) * acc + pv
        return m_new, pv

    cur = [logits(steps[0], hh) for hh in range(2)]
    state = [(None, None), (None, None)]
    for s, step in enumerate(steps):
        qi, kt = step
        nxt = [None, None]
        for hh in range(2):
            if s + 1 < len(steps):
                nxt[hh] = logits(steps[s + 1], hh)
            state[hh] = update(step, hh, *cur[hh], *state[hh])
        cur = nxt
        if kt == qi:
            a0, a1 = state[0][1], state[1][1]
            pad = PV_ROWS - HEAD_DIM
            out = jnp.concatenate([a0[0:HEAD_DIM] / a0[HEAD_DIM:HEAD_DIM + 1],
                                   a1[pad:] / a1[pad - 1:pad]], axis=0)
            o_ref[qi * TQ:(qi + 1) * TQ, :] = out.astype(BF16).T
            state = [(None, None), (None, None)]


def _attention(q8, k8, v8, seq):
    _, t, _ = q8.shape
    nb = t // seq
    pair_spec = pl.BlockSpec((2, seq, LANES), lambda b, p: (p, b, 0))
    return pl.pallas_call(
        _attn_kernel,
        grid=(nb, PAIRS),
        in_specs=[pair_spec, pair_spec, pair_spec],
        out_specs=pl.BlockSpec((seq, LANES), lambda b, p: (b, p)),
        out_shape=jax.ShapeDtypeStruct((t, A_WIDTH), BF16),
        compiler_params=pltpu.CompilerParams(
            dimension_semantics=("arbitrary", "arbitrary"), vmem_limit_bytes=VMEM_LIMIT),
        name="attn",
    )(q8, k8, v8)


def _tail_kernel(tiles_per_seq, o_ref, pu_ref, puh_ref, cv_ref, cvh_ref, x_ref, mod_ref,
                 gmix_ref, wgl_ref, wpool_ref, pscale_ref, convw_ref, wbr_ref, wout_ref, g_ref,
                 gpre_ref, w1_ref, w2_ref, gpost_ref, out_ref, x1_ref):
    i = pl.program_id(0)
    seq_row = pl.ds(i // tiles_per_seq, 1)
    dm = x_ref.shape[1]
    shift_m, scale_m, gate_m, shift_f, scale_f, gate_f = (
        mod_ref[0, seq_row, n * dm:(n + 1) * dm] for n in range(6))
    tile_in_seq = i % tiles_per_seq
    tm, d = x_ref.shape
    n_sub = tm // SUB
    lane = lax.broadcasted_iota(jnp.int32, (SUB + HALO, POOL_WIDTH), 1)
    row = lax.broadcasted_iota(jnp.int32, (SUB + HALO, POOL_WIDTH), 0)
    gd = POOL_WIDTH // len(POOL_WINDOWS)
    wsz = jnp.where(lane < gd, 2.0, jnp.where(lane < 2 * gd, 4.0, jnp.where(lane < 3 * gd, 8.0, 16.0)))
    cw = convw_ref[...]
    o1 = A_WIDTH + POOL_WIDTH

    def with_history(ref, halo_ref, j):
        if j == 0:
            head = jnp.where(tile_in_seq != 0, halo_ref[...].astype(F32), 0.0)
        else:
            head = ref[j * SUB - HALO:j * SUB, :].astype(F32)
        return jnp.concatenate([head, ref[j * SUB:(j + 1) * SUB, :].astype(F32)], axis=0)

    def mix(j):
        rows = slice(j * SUB, (j + 1) * SUB)
        ext = with_history(pu_ref, puh_ref, j)
        a2 = ext + pltpu.roll(ext, 1, 0)
        a4 = a2 + pltpu.roll(a2, 2, 0)
        a8 = a4 + pltpu.roll(a4, 4, 0)
        a16 = a8 + pltpu.roll(a8, 8, 0)
        win = jnp.where(lane < gd, a2, jnp.where(lane < 2 * gd, a4, jnp.where(lane < 3 * gd, a8, a16)))
        frames = (tile_in_seq * tm + j * SUB + row - (HALO - 1)).astype(F32)
        cnt = jnp.maximum(jnp.minimum(frames, wsz), 1.0)
        pm = (win / cnt - ext)[HALO:]
        br_b = jnp.dot(pm.astype(BF16), wpool_ref[...], preferred_element_type=F32) * pscale_ref[...]

        cve = with_history(cv_ref, cvh_ref, j)
        u = cve[:, 2 * CONV_WIDTH:] * cve[:, :CONV_WIDTH]
        y = cw[0:1] * pltpu.roll(u, 2, 0) + cw[1:2] * pltpu.roll(u, 1, 0) + cw[2:3] * u
        br_c = (cve[:, CONV_WIDTH:2 * CONV_WIDTH] * y)[HALO:]

        hb = (_rms(x_ref[rows, :], gmix_ref[...]) * (1.0 + scale_m) + shift_m).astype(BF16)

        def gated(n, branch, lo, hi):
            b = jnp.dot(branch, wbr_ref[lo:hi, :], preferred_element_type=F32)
            return b + b * jnp.tanh(_nt_dot(hb, wgl_ref[n * d:(n + 1) * d, :]))

        merged = gated(0, o_ref[rows, :], 0, A_WIDTH)
        merged += gated(1, br_b.astype(BF16), A_WIDTH, o1)
        merged += gated(2, br_c.astype(BF16), o1, o1 + CONV_WIDTH)
        return merged.astype(BF16)

    def project(j, merged):
        rows = slice(j * SUB, (j + 1) * SUB)
        yo = jnp.dot(merged, wout_ref[...], preferred_element_type=F32)
        x1_ref[rows, :] = x_ref[rows, :] + gate_m * _rms(yo, g_ref[...])

    cur = mix(0)
    for j in range(n_sub):
        nxt = mix(j + 1) if j + 1 < n_sub else None
        project(j, cur)
        cur = nxt

    x1 = x1_ref[...]
    hb = (_rms(x1, gpre_ref[...]) * (1.0 + scale_f) + shift_f).astype(BF16)
    dff = w1_ref.shape[1]
    acc = jnp.zeros(x1.shape, F32)
    for c in range(dff // FF_CHUNK):
        up = jnp.dot(hb, w1_ref[:, c * FF_CHUNK:(c + 1) * FF_CHUNK], preferred_element_type=F32)
        act = jnp.square(jnp.maximum(up, 0.0)).astype(BF16)
        acc += jnp.dot(act, w2_ref[c * FF_CHUNK:(c + 1) * FF_CHUNK, :], preferred_element_type=F32)
    out_ref[...] = x1 + gate_f * _rms(acc, gpost_ref[...])


def _tail(o, pu, cv, x2, mod, l, g_mix_pre, wgl_bf, wpool, pscale, convw, wbr_bf, wout_bf, g_mix_post, g_ff_pre,
          w1_bf, w2_bf, g_ff_post, seq):
    t, d = x2.shape
    nt = t // TMT
    tiles_per_seq = seq // TMT
    hb = TMT // HALO

    def halo_map(i):
        return (jnp.maximum(i * hb - 1, 0), 0)

    consts = (g_mix_pre, wgl_bf, wpool, pscale, convw, wbr_bf, wout_bf, g_mix_post, g_ff_pre, w1_bf, w2_bf,
              g_ff_post)
    per_layer = (True, False, True, True, True, False, False, True, True, False, False, True)
    return pl.pallas_call(
        functools.partial(_tail_kernel, tiles_per_seq),
        grid=(nt,),
        in_specs=[
            pl.BlockSpec((TMT, A_WIDTH), lambda i: (i, 0)),
            pl.BlockSpec((TMT, POOL_WIDTH), lambda i: (i, 0)),
            pl.BlockSpec((HALO, POOL_WIDTH), halo_map),
            pl.BlockSpec((TMT, 3 * CONV_WIDTH), lambda i: (i, 0)),
            pl.BlockSpec((HALO, 3 * CONV_WIDTH), halo_map),
            pl.BlockSpec((TMT, d), lambda i: (i, 0)),
            pl.BlockSpec((1,) + mod.shape[1:], lambda i: (l, 0, 0), pipeline_mode=pl.Buffered(1)),
        ] + [_layer_spec(a, l) if stacked else _const_spec(a.shape) for a, stacked in zip(consts, per_layer)],
        out_specs=pl.BlockSpec((TMT, d), lambda i: (i, 0)),
        out_shape=jax.ShapeDtypeStruct((t, d), F32),
        scratch_shapes=[pltpu.VMEM((TMT, d), F32)],
        compiler_params=pltpu.CompilerParams(
            dimension_semantics=("arbitrary",), vmem_limit_bytes=VMEM_LIMIT),
        name="tail",
    )(o, pu, pu, cv, cv, x2, mod, *consts)


def _block_diag(w):
    depth, g, c, dd = w.shape
    eye = jnp.eye(g, dtype=w.dtype)
    return (eye[None, :, None, :, None] * w[:, :, :, None, :]).reshape(depth, g * c, g * dd)


def kernel(x, c, w_ada, b_ada, g_mix_pre, g_mix_post, g_ff_pre, g_ff_post, w_in, b_f, w_pool, pool_scale,
           conv_w, w_branch, w_out, w_ff1, w_ff2):
    nb, seq, d = x.shape
    depth = w_ada.shape[0]
    assert all(seq % tile == 0 for tile in (TM, TMT, TQ, TK)) and TMT % SUB == 0
    assert d % LANES == 0 and d % PREP_ROWS == 0
    t = nb * seq
    assert (6 * d) % (ADA_CHUNKS * LANES) == 0
    w_in_t = jnp.swapaxes(w_in, 1, 2)
    mod, w_in_bf = _ada(c, w_ada, b_ada, w_in_t)
    x2 = x.reshape(t, d)
    stacked = {"in": w_in_t, "branch": w_branch, "out": w_out, "ff1": w_ff1, "ff2": w_ff2}
    cast_keys = [("in", l) for l in range(1, depth)]
    cast_keys += [(name, l) for l in range(depth) for name in ("branch", "out", "ff1", "ff2")]
    casts = [(stacked[name], l, 0.5 if name == "branch" else 1.0) for name, l in cast_keys]
    bf16_w = {("in", 0): w_in_bf}
    spread = np.zeros((LANES, LANES), np.float32)
    head_of_lane = np.zeros((LANES,), np.int32)
    lane_used = np.zeros((LANES,), bool)
    for hd in range(HEADS):
        spread[_aug_base(hd):_aug_base(hd) + AUG, hd] = 1.0
        head_of_lane[_aug_base(hd):_aug_base(hd) + AUG] = hd
        lane_used[_aug_base(hd):_aug_base(hd) + AUG] = True
    spread = jnp.asarray(spread, BF16)
    bf_all = jnp.where(lane_used, b_f[:, head_of_lane], 0.0)[:, None, :]
    rows = lambda a: a[:, None, :]
    w_pool_bd = _block_diag(w_pool).astype(BF16)
    for l in range(depth):
        (q8, k8, v8, pu, cv, wgl), cast = _inproj(x2, mod, l, rows(g_mix_pre), bf16_w[("in", l)], spread, bf_all,
                                                  seq, casts if l == 0 else [])
        if l == 0:
            bf16_w.update(zip(cast_keys, cast))
        o = _attention(q8, k8, v8, seq)
        x2 = _tail(o, pu, cv, x2, mod, l, rows(g_mix_pre), wgl, w_pool_bd, rows(pool_scale), conv_w,
                   bf16_w[("branch", l)], bf16_w[("out", l)], rows(g_mix_post), rows(g_ff_pre),
                   bf16_w[("ff1", l)], bf16_w[("ff2", l)], rows(g_ff_post), seq)
    return x2.reshape(nb, seq, d)
```
